```python
import jax
import jax.numpy as jnp
from jax import lax
import numpy as np

D_MODEL = 1024
BATCH = 8
SEQ = 2048
DEPTH = 4

N_MIXERS = 4
EPS = 1e-6
Q_BLOCK = 128
CHUNK = 64
ML_HEADS = 8
ML_DV = D_MODEL // ML_HEADS
ML_DQK = D_MODEL // (2 * ML_HEADS)
SB_HEADS = 16
SB_DH = D_MODEL // SB_HEADS
FX_HEADS = 16
FX_DH = D_MODEL // FX_HEADS
GLA_HEADS = 4
GLA_DK = D_MODEL // (2 * GLA_HEADS)
GLA_DV = D_MODEL // GLA_HEADS
GLA_RANK = 16
GLA_TAU = 16.0
D_FF = ((8 * D_MODEL + 3 * 256 - 1) // (3 * 256)) * 256
ML_IN = 2 * ML_HEADS * ML_DQK + 2 * ML_HEADS * ML_DV + 2 * ML_HEADS
SB_IN = 3 * SB_HEADS * SB_DH
FX_IN = 4 * FX_HEADS * FX_DH + FX_HEADS
GLA_IN = 2 * GLA_HEADS * GLA_DK + 2 * GLA_HEADS * GLA_DV + GLA_RANK

kernel_name = 'hybrid_mlstm_stickbreak_fox_gla_trunk'


def _rms(t):
    tf = t.astype(jnp.float32)
    return (tf * lax.rsqrt(jnp.mean(tf * tf, axis=-1, keepdims=True) + EPS)).astype(t.dtype)


def rmsnorm(t, g):
    return _rms(t) * g


def split_heads(t, n_heads):
    b, s, w = t.shape
    return t.reshape(b, s, n_heads, w // n_heads).transpose(0, 2, 1, 3)


def merge_heads(t):
    b, h, s, d = t.shape
    return t.transpose(0, 2, 1, 3).reshape(b, s, h * d)


def to_blocks(t, size):
    b, h, s = t.shape[:3]
    return jnp.moveaxis(t.reshape(b, h, s // size, size, *t.shape[3:]), 2, 0)


def from_blocks(t):
    nb, b, h, size = t.shape[:4]
    return jnp.moveaxis(t, 0, 2).reshape(b, h, nb * size, *t.shape[4:])


def mlstm_mixer(u, w_in, b_if, norm_g, w_out):
    nq = ML_HEADS * ML_DQK
    nv = ML_HEADS * ML_DV
    q, k, v, o_pre, if_pre = jnp.split(u @ w_in, [nq, 2 * nq, 2 * nq + nv, 2 * nq + 2 * nv], axis=-1)
    q = split_heads(q, ML_HEADS)
    k = split_heads(k, ML_HEADS) * (ML_DQK ** -0.5)
    v = split_heads(v, ML_HEADS)
    gates = (if_pre + b_if).astype(jnp.float32).transpose(0, 2, 1)
    log_i = gates[:, :ML_HEADS]
    log_f = jax.nn.log_sigmoid(gates[:, ML_HEADS:])
    causal = jnp.tril(jnp.ones((CHUNK, CHUNK), dtype=bool))
    b = u.shape[0]

    def step(carry, inp):
        c_st, n_st, m_st = carry
        qc, kc, vc, li, lf = inp
        cum_f = jnp.cumsum(lf, axis=-1)
        tot_f = cum_f[..., -1]
        log_d = jnp.where(causal, cum_f[..., :, None] - cum_f[..., None, :] + li[..., None, :], -jnp.inf)
        log_inter = cum_f + m_st[..., None]
        m_t = jnp.maximum(log_inter, jnp.max(log_d, axis=-1))
        w_inter = jnp.exp(log_inter - m_t)
        scores = jnp.einsum('bhtd,bhsd->bhts', qc, kc) * jnp.exp(log_d - m_t[..., None])
        num = w_inter[..., None] * jnp.einsum('bhtd,bhde->bhte', qc, c_st) + jnp.einsum('bhts,bhse->bhte', scores, vc)
        den = w_inter * jnp.einsum('bhtd,bhd->bht', qc, n_st) + jnp.sum(scores, axis=-1)
        h = num / jnp.maximum(jnp.abs(den), jnp.exp(-m_t))[..., None]
        log_w = tot_f[..., None] - cum_f + li
        m_new = jnp.maximum(tot_f + m_st, jnp.max(log_w, axis=-1))
        decay = jnp.exp(tot_f + m_st - m_new)
        w_k = jnp.exp(log_w - m_new[..., None])[..., None] * kc
        c_st = decay[..., None, None] * c_st + jnp.einsum('bhsd,bhse->bhde', w_k, vc)
        n_st = decay[..., None] * n_st + jnp.sum(w_k, axis=2)
        return (c_st, n_st, m_new), h

    init = (jnp.zeros((b, ML_HEADS, ML_DQK, ML_DV), jnp.float32),
            jnp.zeros((b, ML_HEADS, ML_DQK), jnp.float32),
            jnp.zeros((b, ML_HEADS), jnp.float32))
    _, h = lax.scan(step, init, (to_blocks(q, CHUNK), to_blocks(k, CHUNK), to_blocks(v, CHUNK),
                                 to_blocks(log_i, CHUNK), to_blocks(log_f, CHUNK)))
    h = merge_heads(_rms(from_blocks(h))).astype(u.dtype) * norm_g
    return (h * jax.nn.sigmoid(o_pre)) @ w_out


def stick_breaking_mixer(u, w_in, w_out):
    q, k, v = jnp.split(u @ w_in, 3, axis=-1)
    q, k, v = split_heads(q, SB_HEADS), split_heads(k, SB_HEADS), split_heads(v, SB_HEADS)
    s = u.shape[1]
    k_pos = jnp.arange(s)
    scale = SB_DH ** -0.5

    def block(args):
        q_blk, blk = args
        z = jnp.einsum('bhtd,bhsd->bhts', q_blk, k).astype(jnp.float32) * scale
        q_pos = blk * Q_BLOCK + jnp.arange(Q_BLOCK)
        mask = k_pos[None, :] < q_pos[:, None]
        log_1m = jnp.where(mask, -jax.nn.softplus(z), 0.0)
        log_keep = lax.cumsum(log_1m, axis=3, reverse=True) - log_1m
        a = jnp.where(mask, jnp.exp(jax.nn.log_sigmoid(z) + log_keep), 0.0)
        return jnp.einsum('bhts,bhsd->bhtd', a.astype(v.dtype), v)

    out = lax.map(block, (to_blocks(q, Q_BLOCK), jnp.arange(s // Q_BLOCK)))
    return merge_heads(from_blocks(out)) @ w_out


def forgetting_attn_mixer(u, w_in, b_f, q_norm_g, k_norm_g, w_out):
    w = FX_HEADS * FX_DH
    q, k, v, o_pre, f_pre = jnp.split(u @ w_in, [w, 2 * w, 3 * w, 4 * w], axis=-1)
    q = rmsnorm(split_heads(q, FX_HEADS), q_norm_g)
    k = rmsnorm(split_heads(k, FX_HEADS), k_norm_g)
    v = split_heads(v, FX_HEADS)
    log_f = jax.nn.log_sigmoid((f_pre + b_f).astype(jnp.float32)).transpose(0, 2, 1)
    cum_f = jnp.cumsum(log_f, axis=-1)
    s = u.shape[1]
    k_pos = jnp.arange(s)
    scale = FX_DH ** -0.5

    def block(args):
        q_blk, cf_blk, blk = args
        logits = (jnp.einsum('bhtd,bhsd->bhts', q_blk, k).astype(jnp.float32) * scale
                  + cf_blk[..., :, None] - cum_f[..., None, :])
        q_pos = blk * Q_BLOCK + jnp.arange(Q_BLOCK)
        mask = k_pos[None, :] <= q_pos[:, None]
        p = jax.nn.softmax(jnp.where(mask, logits, -jnp.inf), axis=-1)
        return jnp.einsum('bhts,bhsd->bhtd', p.astype(v.dtype), v)

    out = lax.map(block, (to_blocks(q, Q_BLOCK), to_blocks(cum_f, Q_BLOCK), jnp.arange(s // Q_BLOCK)))
    o = merge_heads(from_blocks(out)) * jax.nn.sigmoid(o_pre)
    return o @ w_out


def gla_mixer(u, w_in, w_gate_up, b_gate, norm_g, w_out):
    nk = GLA_HEADS * GLA_DK
    nv = GLA_HEADS * GLA_DV
    q, k, v, r, g_low = jnp.split(u @ w_in, [nk, 2 * nk, 2 * nk + nv, 2 * nk + 2 * nv], axis=-1)
    q = split_heads(q, GLA_HEADS) * (GLA_DK ** -0.5)
    k = split_heads(k, GLA_HEADS)
    v = split_heads(v, GLA_HEADS)
    log_a = jax.nn.log_sigmoid((g_low @ w_gate_up + b_gate).astype(jnp.float32)) / GLA_TAU
    log_a = split_heads(log_a, GLA_HEADS)
    causal = jnp.tril(jnp.ones((CHUNK, CHUNK), dtype=bool))
    b = u.shape[0]

    def step(state, inp):
        qc, kc, vc, la = inp
        cum_a = jnp.cumsum(la, axis=2)
        tot_a = cum_a[:, :, -1]
        inter = jnp.einsum('bhtd,bhde->bhte', qc * jnp.exp(cum_a), state)
        rel = jnp.where(causal[:, :, None], cum_a[:, :, :, None, :] - cum_a[:, :, None, :, :], -jnp.inf)
        scores = jnp.einsum('bhtd,bhsd,bhtsd->bhts', qc, kc, jnp.exp(rel))
        h = inter + jnp.einsum('bhts,bhse->bhte', scores, vc)
        state = (jnp.exp(tot_a)[..., None] * state
                 + jnp.einsum('bhsd,bhse->bhde', kc * jnp.exp(tot_a[:, :, None] - cum_a), vc))
        return state, h

    init = jnp.zeros((b, GLA_HEADS, GLA_DK, GLA_DV), jnp.float32)
    _, h = lax.scan(step, init, (to_blocks(q, CHUNK), to_blocks(k, CHUNK), to_blocks(v, CHUNK),
                                 to_blocks(log_a, CHUNK)))
    h = merge_heads(_rms(from_blocks(h))).astype(u.dtype) * norm_g
    return (h * jax.nn.silu(r)) @ w_out


def swiglu(u, w_gate, w_up, w_down):
    return (jax.nn.silu(u @ w_gate) * (u @ w_up)) @ w_down


def setup_inputs(seed: int = 0) -> dict:
    key = jax.random.key(seed)
    keys = jax.random.split(key, 26)
    f32 = jnp.float32

    def nrm(i, shape, scale):
        return jax.random.normal(keys[i], shape, f32) * scale

    n_ml = len(range(0, DEPTH, N_MIXERS))
    n_sb = len(range(1, DEPTH, N_MIXERS))
    n_fx = len(range(2, DEPTH, N_MIXERS))
    n_gla = len(range(3, DEPTH, N_MIXERS))
    d = D_MODEL
    ml_b_if = jnp.concatenate([
        jnp.broadcast_to(nrm(10, (n_ml, ML_HEADS), 0.1), (n_ml, ML_HEADS)),
        jnp.linspace(3.0, 6.0, ML_HEADS, dtype=f32)[None, :] + nrm(11, (n_ml, ML_HEADS), 0.1)], axis=-1)
    return {
        'x': nrm(0, (BATCH, SEQ, d), 1.0),
        'mix_norm_g': 1.0 + nrm(1, (DEPTH, d), 0.05),
        'ffn_norm_g': 1.0 + nrm(2, (DEPTH, d), 0.05),
        'ffn_w_gate': nrm(3, (DEPTH, d, D_FF), d ** -0.5),
        'ffn_w_up': nrm(4, (DEPTH, d, D_FF), d ** -0.5),
        'ffn_w_down': nrm(5, (DEPTH, D_FF, d), D_FF ** -0.5),
        'ml_w_in': nrm(6, (n_ml, d, ML_IN), d ** -0.5),
        'ml_b_if': ml_b_if,
        'ml_norm_g': 1.0 + nrm(7, (n_ml, ML_HEADS * ML_DV), 0.05),
        'ml_w_out': nrm(8, (n_ml, ML_HEADS * ML_DV, d), (ML_HEADS * ML_DV) ** -0.5),
        'sb_w_in': nrm(9, (n_sb, d, SB_IN), d ** -0.5),
        'sb_w_out': nrm(12, (n_sb, SB_HEADS * SB_DH, d), (SB_HEADS * SB_DH) ** -0.5),
        'fx_w_in': nrm(13, (n_fx, d, FX_IN), d ** -0.5),
        'fx_b_f': jnp.linspace(2.0, 5.0, FX_HEADS, dtype=f32)[None, :] + nrm(14, (n_fx, FX_HEADS), 0.1),
        'fx_q_norm_g': 1.0 + nrm(15, (n_fx, FX_DH), 0.05),
        'fx_k_norm_g': 1.0 + nrm(16, (n_fx, FX_DH), 0.05),
        'fx_w_out': nrm(17, (n_fx, FX_HEADS * FX_DH, d), (FX_HEADS * FX_DH) ** -0.5),
        'gla_w_in': nrm(18, (n_gla, d, GLA_IN), d ** -0.5),
        'gla_w_gate_up': nrm(19, (n_gla, GLA_RANK, GLA_HEADS * GLA_DK), GLA_RANK ** -0.5),
        'gla_b_gate': nrm(20, (n_gla, GLA_HEADS * GLA_DK), 0.1),
        'gla_norm_g': 1.0 + nrm(21, (n_gla, GLA_HEADS * GLA_DV), 0.05),
        'gla_w_out': nrm(22, (n_gla, GLA_HEADS * GLA_DV, d), (GLA_HEADS * GLA_DV) ** -0.5),
    }


def reference(x, mix_norm_g, ffn_norm_g, ffn_w_gate, ffn_w_up, ffn_w_down,
              ml_w_in, ml_b_if, ml_norm_g, ml_w_out,
              sb_w_in, sb_w_out,
              fx_w_in, fx_b_f, fx_q_norm_g, fx_k_norm_g, fx_w_out,
              gla_w_in, gla_w_gate_up, gla_b_gate, gla_norm_g, gla_w_out):
    h = x
    for layer in range(DEPTH):
        kind = layer % N_MIXERS
        j = layer // N_MIXERS
        u = rmsnorm(h, mix_norm_g[layer])
        if kind == 0:
            y = mlstm_mixer(u, ml_w_in[j], ml_b_if[j], ml_norm_g[j], ml_w_out[j])
        elif kind == 1:
            y = stick_breaking_mixer(u, sb_w_in[j], sb_w_out[j])
        elif kind == 2:
            y = forgetting_attn_mixer(u, fx_w_in[j], fx_b_f[j], fx_q_norm_g[j], fx_k_norm_g[j], fx_w_out[j])
        else:
            y = gla_mixer(u, gla_w_in[j], gla_w_gate_up[j], gla_b_gate[j], gla_norm_g[j], gla_w_out[j])
        h = h + y.astype(h.dtype)
        u = rmsnorm(h, ffn_norm_g[layer])
        h = h + swiglu(u, ffn_w_gate[layer], ffn_w_up[layer], ffn_w_down[layer]).astype(h.dtype)
    return h
```

```python
import functools

import jax
import jax.numpy as jnp
from jax import lax
from jax.experimental import pallas as pl
from jax.experimental.pallas import tpu as pltpu

F32 = jnp.float32
BF16 = jnp.bfloat16

D_MODEL = 1024
EPS = 1e-6
CHUNK = 64
ML_HEADS, ML_DQK, ML_DV = 8, 64, 128
SB_HEADS, SB_DH = 16, 64
FX_HEADS, FX_DH = 16, 64
GLA_HEADS, GLA_DK, GLA_DV = 4, 128, 256
GLA_RANK = 16
GLA_TAU = 16.0
GLA_SUB = 16
NEG = -1e30
LANES = 128
VMEM_LIMIT = 56 * 1024 * 1024


def _cparams(sem):
    return pltpu.CompilerParams(dimension_semantics=sem, vmem_limit_bytes=VMEM_LIMIT)


def _log_sigmoid(x):
    return jnp.minimum(x, 0.0) - jnp.log(1.0 + jnp.exp(-jnp.abs(x)))


def _softplus(x):
    return jnp.maximum(x, 0.0) + jnp.log(1.0 + jnp.exp(-jnp.abs(x)))


def _sigmoid(x):
    return 1.0 / (1.0 + jnp.exp(-x))


def _split_bf16(x):
    hi = x.astype(BF16)
    lo = (x - hi.astype(F32)).astype(BF16)
    return hi, lo


def _dot(a, b):
    return jnp.dot(a, b, preferred_element_type=F32)


def _dot_nt(a, b):
    return lax.dot_general(a, b, (((1,), (1,)), ((), ())), preferred_element_type=F32)


def _dot_tn(a, b):
    return lax.dot_general(a, b, (((0,), (0,)), ((), ())), preferred_element_type=F32)


def _tri_dot(tri, x):
    hi, lo = _split_bf16(x)
    return _dot(tri, hi) + _dot(tri, lo)


def _dot_tri(x, tri):
    hi, lo = _split_bf16(x)
    return _dot(hi, tri) + _dot(lo, tri)


def _in_proj_kernel(x_ref, g_ref, w_ref, wg_ref, main_ref, gate_ref, *, tn):
    x = x_ref[...]
    ms = jnp.mean(x * x, axis=-1, keepdims=True)
    u = (x * lax.rsqrt(ms + EPS) * g_ref[...]).astype(BF16)
    for c in range(main_ref.shape[-1] // tn):
        cs = slice(c * tn, (c + 1) * tn)
        main_ref[:, cs] = _dot(u, w_ref[:, cs]).astype(main_ref.dtype)
    gate_ref[...] = _dot(u, wg_ref[...])


def _in_proj(x, g, w_main, w_gate, *, tm=512, tn=512):
    n, d = x.shape
    wn = w_main.shape[1]
    return pl.pallas_call(
        functools.partial(_in_proj_kernel, tn=tn),
        grid=(n // tm,),
        in_specs=[
            pl.BlockSpec((tm, d), lambda i: (i, 0)),
            pl.BlockSpec((1, d), lambda i: (0, 0)),
            pl.BlockSpec((d, wn), lambda i: (0, 0)),
            pl.BlockSpec((d, LANES), lambda i: (0, 0)),
        ],
        out_specs=[
            pl.BlockSpec((tm, wn), lambda i: (i, 0)),
            pl.BlockSpec((tm, LANES), lambda i: (i, 0)),
        ],
        out_shape=[
            jax.ShapeDtypeStruct((n, wn), BF16),
            jax.ShapeDtypeStruct((n, LANES), F32),
        ],
        compiler_params=_cparams(("parallel",)),
        name="in_proj",
    )(x, g, w_main, w_gate)


def _out_ffn_kernel(h_ref, a_ref, wo_ref, g_ref, wg_ref, wu_ref, wd_ref, o_ref, u_s):
    f = pl.program_id(1)

    @pl.when(f == 0)
    def _():
        h1 = h_ref[...] + _dot(a_ref[...], wo_ref[...])
        o_ref[...] = h1
        ms = jnp.mean(h1 * h1, axis=-1, keepdims=True)
        u_s[...] = (h1 * lax.rsqrt(ms + EPS) * g_ref[...]).astype(BF16)

    u = u_s[...]
    gt = _dot(u, wg_ref[...])
    up = _dot(u, wu_ref[...])
    hid = (gt * _sigmoid(gt) * up).astype(BF16)
    o_ref[...] += _dot(hid, wd_ref[...])


def _out_ffn(h, a, w_out, g, w_gate, w_up, w_down, *, tm=512, tf=256):
    n, d = h.shape
    dff = w_gate.shape[1]
    return pl.pallas_call(
        _out_ffn_kernel,
        grid=(n // tm, dff // tf),
        in_specs=[
            pl.BlockSpec((tm, d), lambda i, f: (i, 0)),
            pl.BlockSpec((tm, d), lambda i, f: (i, 0)),
            pl.BlockSpec((d, d), lambda i, f: (0, 0)),
            pl.BlockSpec((1, d), lambda i, f: (0, 0)),
            pl.BlockSpec((d, tf), lambda i, f: (0, f)),
            pl.BlockSpec((d, tf), lambda i, f: (0, f)),
            pl.BlockSpec((tf, d), lambda i, f: (f, 0)),
        ],
        out_specs=pl.BlockSpec((tm, d), lambda i, f: (i, 0)),
        out_shape=jax.ShapeDtypeStruct((n, d), F32),
        scratch_shapes=[pltpu.VMEM((tm, d), BF16)],
        compiler_params=_cparams(("parallel", "arbitrary")),
        name="out_ffn",
    )(h, a, w_out, g, w_gate, w_up, w_down)


def _mlstm_kernel(main_ref, gc_ref, gr_ref, bc_ref, br_ref, ng_ref, o_ref, c_s, m_s):
    L = CHUNK
    H = ML_HEADS
    scale = ML_DQK ** -0.5

    @pl.when(pl.program_id(1) == 0)
    def _():
        c_s[...] = jnp.zeros_like(c_s)
        m_s[...] = jnp.zeros_like(m_s)

    row = lax.broadcasted_iota(jnp.int32, (L, L), 0)
    col = lax.broadcasted_iota(jnp.int32, (L, L), 1)
    causal = col <= row
    tri = causal.astype(BF16)
    tri_t = (row <= col).astype(BF16)

    gc = gc_ref[0] + bc_ref[...]
    gr = gr_ref[0, 0] + br_ref[...]
    cum_c = _tri_dot(tri, _log_sigmoid(gc))
    li_r = gr[0:H]
    cum_r = _dot_tri(_log_sigmoid(gr[H:2 * H]), tri_t)

    ones_blk = (lax.broadcasted_iota(jnp.int32, (L, ML_DV), 1) == 0).astype(BF16)

    for h in range(H):
        q = main_ref[0, :, h * ML_DQK:(h + 1) * ML_DQK]
        k = main_ref[0, :, H * ML_DQK + h * ML_DQK:H * ML_DQK + (h + 1) * ML_DQK]
        v = main_ref[0, :, 2 * H * ML_DQK + h * ML_DV:2 * H * ML_DQK + (h + 1) * ML_DV]
        op = main_ref[0, :, 2 * H * ML_DQK + H * ML_DV + h * ML_DV:
                      2 * H * ML_DQK + H * ML_DV + (h + 1) * ML_DV]
        vext = jnp.concatenate([v, ones_blk], axis=1)

        cum_c1 = cum_c[:, H + h:H + h + 1]
        li_c1 = gc[:, h:h + 1]
        cum_r1 = cum_r[h:h + 1, :]
        li_r1 = li_r[h:h + 1, :]
        m_st = m_s[h][:, 0:1]
        cext = c_s[h]

        log_d = jnp.where(causal, cum_c1 - cum_r1 + li_r1, NEG)
        m_t = jnp.maximum(cum_c1 + m_st, jnp.max(log_d, axis=-1, keepdims=True))
        dmat = jnp.exp(log_d - m_t)
        w_inter = jnp.exp(cum_c1 + m_st - m_t)
        sc = (_dot_nt(q, k) * (dmat * scale)).astype(BF16)
        tot = w_inter * _dot(q, cext.astype(BF16)) + _dot(sc, vext)
        num = tot[:, :ML_DV]
        den = tot[:, ML_DV:ML_DV + 1]
        hh = num / jnp.maximum(jnp.abs(den), jnp.exp(-m_t))
        hn = hh * lax.rsqrt(jnp.mean(hh * hh, axis=-1, keepdims=True) + EPS)
        out = hn * ng_ref[:, h * ML_DV:(h + 1) * ML_DV] * _sigmoid(op.astype(F32))
        o_ref[0, :, h * ML_DV:(h + 1) * ML_DV] = out.astype(o_ref.dtype)

        tot_f = cum_r1[:, L - 1:L]
        log_w_r = tot_f - cum_r1 + li_r1
        m_new = jnp.maximum(tot_f + m_st, jnp.max(log_w_r, axis=-1, keepdims=True))
        decay = jnp.exp(tot_f + m_st - m_new)
        w_c = jnp.exp(tot_f - cum_c1 + li_c1 - m_new) * scale
        wk = (k.astype(F32) * w_c).astype(BF16)
        c_s[h] = decay * cext + _dot_tn(wk, vext)
        m_s[h] = jnp.broadcast_to(m_new, (1, LANES))


def _mlstm_core(main, gates, b_if, norm_g):
    b, s, _ = main.shape
    nc = s // CHUNK
    h = ML_HEADS
    gr = gates[:, :, :2 * h].reshape(b, nc, CHUNK, 2 * h).transpose(0, 1, 3, 2)
    bc = jnp.zeros((1, LANES), F32).at[0, :2 * h].set(b_if)
    br = b_if.reshape(2 * h, 1)
    return pl.pallas_call(
        _mlstm_kernel,
        grid=(b, nc),
        in_specs=[
            pl.BlockSpec((1, CHUNK, main.shape[2]), lambda i, c: (i, c, 0)),
            pl.BlockSpec((1, CHUNK, LANES), lambda i, c: (i, c, 0)),
            pl.BlockSpec((1, 1, 2 * h, CHUNK), lambda i, c: (i, c, 0, 0)),
            pl.BlockSpec((1, LANES), lambda i, c: (0, 0)),
            pl.BlockSpec((2 * h, 1), lambda i, c: (0, 0)),
            pl.BlockSpec((1, h * ML_DV), lambda i, c: (0, 0)),
        ],
        out_specs=pl.BlockSpec((1, CHUNK, h * ML_DV), lambda i, c: (i, c, 0)),
        out_shape=jax.ShapeDtypeStruct((b, s, h * ML_DV), BF16),
        scratch_shapes=[
            pltpu.VMEM((h, ML_DQK, 2 * ML_DV), F32),
            pltpu.VMEM((h, 1, LANES), F32),
        ],
        compiler_params=_cparams(("parallel", "arbitrary")),
        name="mlstm_core",
    )(main, gates, gr, bc, br, norm_g.reshape(1, -1))


def _sb_kernel(q_ref, k_ref, v_ref, o_ref, *, t):
    qi = pl.program_id(2)
    lane = lax.broadcasted_iota(jnp.int32, (1, LANES), 1)
    lo = lane < SB_DH
    row = lax.broadcasted_iota(jnp.int32, (t, t), 0)
    col = lax.broadcasted_iota(jnp.int32, (t, t), 1)
    valid = col < row
    upper = (row > col).astype(BF16)
    q2 = q_ref[0] * jnp.asarray(SB_DH ** -0.5, BF16)

    def tile(qm, sel, kt, carry, acc, masked):
        k0 = pl.multiple_of(kt * t, t)
        kk = k_ref[0, pl.ds(k0, t), :]
        vv = jnp.where(sel, v_ref[0, pl.ds(k0, t), :], jnp.zeros((), BF16))
        z = _dot_nt(qm, kk)
        sp = _softplus(z)
        lp = jnp.where(valid, -sp, 0.0) if masked else -sp
        keep = _dot_tri(lp, upper) + carry
        a = jnp.exp(z - sp + keep)
        if masked:
            a = jnp.where(valid, a, 0.0)
        acc = acc + _dot(a.astype(BF16), vv)
        carry = carry + jnp.sum(lp, axis=-1, keepdims=True)
        return carry, acc

    out = jnp.zeros((t, LANES), F32)
    for hh in range(2):
        sel = lo if hh == 0 else jnp.logical_not(lo)
        qm = jnp.where(sel, q2, jnp.zeros((), BF16))
        carry0 = jnp.zeros((t, 1), F32)
        acc0 = jnp.zeros((t, LANES), F32)
        carry, acc = tile(qm, sel, qi, carry0, acc0, True)

        def body(i, ca, qm=qm, sel=sel):
            return tile(qm, sel, qi - i, ca[0], ca[1], False)

        carry, acc = lax.fori_loop(1, qi + 1, body, (carry, acc))
        out = out + acc
    o_ref[0] = out.astype(o_ref.dtype)


def _sb_core(main, *, t=128):
    b, s, _ = main.shape
    hp = SB_HEADS // 2
    return pl.pallas_call(
        functools.partial(_sb_kernel, t=t),
        grid=(b, hp, s // t),
        in_specs=[
            pl.BlockSpec((1, t, LANES), lambda i, p, j: (i, j, p)),
            pl.BlockSpec((1, s, LANES), lambda i, p, j: (i, 0, hp + p)),
            pl.BlockSpec((1, s, LANES), lambda i, p, j: (i, 0, 2 * hp + p)),
        ],
        out_specs=pl.BlockSpec((1, t, LANES), lambda i, p, j: (i, j, p)),
        out_shape=jax.ShapeDtypeStruct((b, s, SB_HEADS * SB_DH), BF16),
        compiler_params=_cparams(("parallel", "parallel", "arbitrary")),
        name="sb_core",
    )(main, main, main)


def _fox_prep_kernel(main_ref, fp_ref, bf_ref, gq_ref, gk_ref, qk_ref, cum_ref, carry_s, *, ts):
    @pl.when(pl.program_id(1) == 0)
    def _():
        carry_s[...] = jnp.zeros_like(carry_s)

    lane = lax.broadcasted_iota(jnp.int32, (1, LANES), 1)
    lo = lane < FX_DH
    n_q = FX_HEADS * FX_DH // LANES
    for c in range(2 * n_q):
        cs = slice(c * LANES, (c + 1) * LANES)
        xc = main_ref[0, :, cs].astype(F32)
        x2 = xc * xc
        s_lo = jnp.sum(jnp.where(lo, x2, 0.0), axis=-1, keepdims=True)
        s_hi = jnp.sum(jnp.where(lo, 0.0, x2), axis=-1, keepdims=True)
        r = jnp.where(lo, lax.rsqrt(s_lo / FX_DH + EPS), lax.rsqrt(s_hi / FX_DH + EPS))
        g = gq_ref[...] if c < n_q else gk_ref[...]
        qk_ref[0, :, cs] = (xc * r * g).astype(qk_ref.dtype)

    row = lax.broadcasted_iota(jnp.int32, (ts, ts), 0)
    col = lax.broadcasted_iota(jnp.int32, (ts, ts), 1)
    tri = (col <= row).astype(BF16)
    lf = _log_sigmoid(fp_ref[0] + bf_ref[...])
    cum = _tri_dot(tri, lf) + carry_s[...]
    cum_ref[0] = cum
    carry_s[...] = cum[ts - 1:ts, :]


def _fox_prep(main, f_pre, b_f, q_norm_g, k_norm_g, *, ts=256):
    b, s, _ = main.shape
    w = 2 * FX_HEADS * FX_DH
    bf = jnp.zeros((1, LANES), F32).at[0, :FX_HEADS].set(b_f)
    gq = jnp.tile(q_norm_g, 2).reshape(1, LANES) * (FX_DH ** -0.5)
    gk = jnp.tile(k_norm_g, 2).reshape(1, LANES)
    return pl.pallas_call(
        functools.partial(_fox_prep_kernel, ts=ts),
        grid=(b, s // ts),
        in_specs=[
            pl.BlockSpec((1, ts, w), lambda i, j: (i, j, 0)),
            pl.BlockSpec((1, ts, LANES), lambda i, j: (i, j, 0)),
            pl.BlockSpec((1, LANES), lambda i, j: (0, 0)),
            pl.BlockSpec((1, LANES), lambda i, j: (0, 0)),
            pl.BlockSpec((1, LANES), lambda i, j: (0, 0)),
        ],
        out_specs=[
            pl.BlockSpec((1, ts, w), lambda i, j: (i, j, 0)),
            pl.BlockSpec((1, ts, LANES), lambda i, j: (i, j, 0)),
        ],
        out_shape=[
            jax.ShapeDtypeStruct((b, s, w), BF16),
            jax.ShapeDtypeStruct((b, s, LANES), F32),
        ],
        scratch_shapes=[pltpu.VMEM((1, LANES), F32)],
        compiler_params=_cparams(("parallel", "arbitrary")),
        name="fox_prep",
    )(main, f_pre, bf, gq, gk)


def _fox_kernel(q_ref, k_ref, v_ref, cc_ref, cr_ref, op_ref, o_ref, *, t):
    hp = pl.program_id(1)
    qi = pl.program_id(2)
    lane = lax.broadcasted_iota(jnp.int32, (1, LANES), 1)
    lo = lane < FX_DH
    row = lax.broadcasted_iota(jnp.int32, (t, t), 0)
    col = lax.broadcasted_iota(jnp.int32, (t, t), 1)
    causal = col <= row
    q2 = q_ref[0]
    cc = cc_ref[0]

    def tile(qm, sel, cf_t, hh, kt, m, acc, masked):
        k0 = pl.multiple_of(kt * t, t)
        kk = k_ref[0, pl.ds(k0, t), :]
        vext = jnp.where(sel, v_ref[0, pl.ds(k0, t), :], jnp.ones((), BF16))
        s = _dot_nt(qm, kk) + (cf_t - cr_ref[0, 0, hh:hh + 1, pl.ds(k0, t)])
        if masked:
            s = jnp.where(causal, s, NEG)
        m_new = jnp.maximum(m, jnp.max(s, axis=-1, keepdims=True))
        alpha = jnp.exp(m - m_new)
        p = jnp.exp(s - m_new).astype(BF16)
        acc = alpha * acc + _dot(p, vext)
        return m_new, acc

    outs = []
    for hh in range(2):
        sel = lo if hh == 0 else jnp.logical_not(lo)
        qm = jnp.where(sel, q2, jnp.zeros((), BF16))
        cf_t = jnp.sum(jnp.where(lane == 2 * hp + hh, cc, 0.0), axis=-1, keepdims=True)
        m0 = jnp.full((t, 1), NEG, F32)
        acc0 = jnp.zeros((t, LANES), F32)

        def body(kt, ma, qm=qm, sel=sel, cf_t=cf_t, hh=hh):
            return tile(qm, sel, cf_t, hh, kt, ma[0], ma[1], False)

        m, acc = lax.fori_loop(0, qi, body, (m0, acc0))
        m, acc = tile(qm, sel, cf_t, hh, qi, m, acc, True)
        l_sum = acc[:, FX_DH:FX_DH + 1] if hh == 0 else acc[:, 0:1]
        outs.append(acc / l_sum)
    out = jnp.where(lo, outs[0], outs[1]) * _sigmoid(op_ref[0].astype(F32))
    o_ref[0] = out.astype(o_ref.dtype)


def _fox_core(main, qkn, cum, *, t=256):
    b, s, _ = main.shape
    hp = FX_HEADS // 2
    cum_r = cum[:, :, :FX_HEADS].transpose(0, 2, 1).reshape(b, hp, 2, s)
    return pl.pallas_call(
        functools.partial(_fox_kernel, t=t),
        grid=(b, hp, s // t),
        in_specs=[
            pl.BlockSpec((1, t, LANES), lambda i, p, j: (i, j, p)),
            pl.BlockSpec((1, s, LANES), lambda i, p, j: (i, 0, hp + p)),
            pl.BlockSpec((1, s, LANES), lambda i, p, j: (i, 0, 2 * hp + p)),
            pl.BlockSpec((1, t, LANES), lambda i, p, j: (i, j, 0)),
            pl.BlockSpec((1, 1, 2, s), lambda i, p, j: (i, p, 0, 0)),
            pl.BlockSpec((1, t, LANES), lambda i, p, j: (i, j, 3 * hp + p)),
        ],
        out_specs=pl.BlockSpec((1, t, LANES), lambda i, p, j: (i, j, p)),
        out_shape=jax.ShapeDtypeStruct((b, s, FX_HEADS * FX_DH), BF16),
        compiler_params=_cparams(("parallel", "parallel", "arbitrary")),
        name="fox_core",
    )(qkn, qkn, main, cum, cum_r, main)


def _gla_kernel(main_ref, gl_ref, wgu_ref, bg_ref, ng_ref, o_ref, st_s, cum_s, k_s, v_s):
    L = CHUNK
    H = GLA_HEADS
    SUB = GLA_SUB
    nk = H * GLA_DK
    nv = H * GLA_DV
    scale = GLA_DK ** -0.5

    @pl.when(pl.program_id(1) == 0)
    def _():
        st_s[...] = jnp.zeros_like(st_s)

    row = lax.broadcasted_iota(jnp.int32, (L, L), 0)
    col = lax.broadcasted_iota(jnp.int32, (L, L), 1)
    tri = (col <= row).astype(BF16)
    eye = (lax.broadcasted_iota(jnp.int32, (GLA_DK, GLA_DK), 0)
           == lax.broadcasted_iota(jnp.int32, (GLA_DK, GLA_DK), 1))
    t_sub = lax.broadcasted_iota(jnp.int32, (SUB, 1), 0)

    g_hi, g_lo = _split_bf16(gl_ref[0])
    w_hi, w_lo = _split_bf16(wgu_ref[...])
    logits = _dot(g_hi, w_hi) + _dot(g_hi, w_lo) + _dot(g_lo, w_hi)
    la = _log_sigmoid(logits + bg_ref[...]) * (1.0 / GLA_TAU)
    cum_all = _tri_dot(tri, la)

    for h in range(H):
        ks = slice(h * GLA_DK, (h + 1) * GLA_DK)
        vs = slice(h * GLA_DV, (h + 1) * GLA_DV)
        cum = cum_all[:, ks]
        tot = cum[L - 1:L, :]
        q = main_ref[0, :, ks].astype(F32) * scale
        k = main_ref[0, :, nk + h * GLA_DK:nk + (h + 1) * GLA_DK].astype(F32)
        v_bf = main_ref[0, :, 2 * nk + h * GLA_DV:2 * nk + (h + 1) * GLA_DV]
        state = st_s[h]

        inter = _dot((q * jnp.exp(cum)).astype(BF16), state.astype(BF16))

        cum_s[...] = cum
        k_s[...] = k
        v_s[...] = v_bf.astype(F32)

        for i in range(L // SUB):
            rs = slice(i * SUB, (i + 1) * SUB)
            qb = q[rs]
            cb = cum[rs]
            hb = inter[rs]
            if i > 0:
                ref_pt = cum[i * SUB - 1:i * SUB, :]
                qs = (qb * jnp.exp(cb - ref_pt)).astype(BF16)
                kprev = (k[:i * SUB] * jnp.exp(ref_pt - cum[:i * SUB])).astype(BF16)
                a_off = _dot_nt(qs, kprev)
                hb = hb + _dot(a_off.astype(BF16), v_bf[:i * SUB])
            for j in range(SUB):
                rj = i * SUB + j
                kj = k_s[rj:rj + 1, :]
                cj = cum_s[rj:rj + 1, :]
                vj = v_s[rj:rj + 1, :]
                e = jnp.exp(jnp.where(t_sub >= j, cb - cj, NEG))
                p = jnp.sum(qb * kj * e, axis=-1, keepdims=True)
                hb = hb + p * vj
            hn = hb * lax.rsqrt(jnp.mean(hb * hb, axis=-1, keepdims=True) + EPS)
            r = main_ref[0, rs, 2 * nk + nv + h * GLA_DV:2 * nk + nv + (h + 1) * GLA_DV].astype(F32)
            out = hn * ng_ref[:, vs] * (r * _sigmoid(r))
            o_ref[0, rs, vs] = out.astype(o_ref.dtype)

        kd = (k * jnp.exp(tot - cum)).astype(BF16)
        tot_col = jnp.sum(jnp.where(eye, jnp.broadcast_to(tot, (GLA_DK, GLA_DK)), 0.0),
                          axis=-1, keepdims=True)
        st_s[h] = jnp.exp(tot_col) * state + _dot_tn(kd, v_bf)


def _gla_core(main, g_low, w_gate_up, b_gate, norm_g):
    b, s, _ = main.shape
    nc = s // CHUNK
    nk = GLA_HEADS * GLA_DK
    nv = GLA_HEADS * GLA_DV
    wgu = jnp.zeros((LANES, nk), F32).at[:GLA_RANK].set(w_gate_up)
    return pl.pallas_call(
        _gla_kernel,
        grid=(b, nc),
        in_specs=[
            pl.BlockSpec((1, CHUNK, main.shape[2]), lambda i, c: (i, c, 0)),
            pl.BlockSpec((1, CHUNK, LANES), lambda i, c: (i, c, 0)),
            pl.BlockSpec((LANES, nk), lambda i, c: (0, 0)),
            pl.BlockSpec((1, nk), lambda i, c: (0, 0)),
            pl.BlockSpec((1, nv), lambda i, c: (0, 0)),
        ],
        out_specs=pl.BlockSpec((1, CHUNK, nv), lambda i, c: (i, c, 0)),
        out_shape=jax.ShapeDtypeStruct((b, s, nv), BF16),
        scratch_shapes=[
            pltpu.VMEM((GLA_HEADS, GLA_DK, GLA_DV), F32),
            pltpu.VMEM((CHUNK, GLA_DK), F32),
            pltpu.VMEM((CHUNK, GLA_DK), F32),
            pltpu.VMEM((CHUNK, GLA_DV), F32),
        ],
        compiler_params=_cparams(("parallel", "arbitrary")),
        name="gla_core",
    )(main, g_low, wgu, b_gate.reshape(1, -1), norm_g.reshape(1, -1))


def _split_w_in(w_in, n_main):
    d = w_in.shape[0]
    n_gate = w_in.shape[1] - n_main
    w_gate = jnp.zeros((d, LANES), F32).at[:, :n_gate].set(w_in[:, n_main:])
    return w_in[:, :n_main].astype(BF16), w_gate.astype(BF16)


def kernel(x, mix_norm_g, ffn_norm_g, ffn_w_gate, ffn_w_up, ffn_w_down, ml_w_in, ml_b_if, ml_norm_g, ml_w_out, sb_w_in, sb_w_out, fx_w_in, fx_b_f, fx_q_norm_g, fx_k_norm_g, fx_w_out, gla_w_in, gla_w_gate_up, gla_b_gate, gla_norm_g, gla_w_out):
    b, s, d = x.shape
    depth = mix_norm_g.shape[0]
    h = x.reshape(b * s, d)
    for layer in range(depth):
        kind = layer % 4
        j = layer // 4
        g_mix = mix_norm_g[layer].reshape(1, d)
        if kind == 0:
            w_main, w_gate = _split_w_in(ml_w_in[j], ml_w_in.shape[2] - 2 * ML_HEADS)
            main, gates = _in_proj(h, g_mix, w_main, w_gate)
            a = _mlstm_core(main.reshape(b, s, -1), gates.reshape(b, s, LANES), ml_b_if[j], ml_norm_g[j])
            w_out = ml_w_out[j]
        elif kind == 1:
            w_main, w_gate = _split_w_in(sb_w_in[j], sb_w_in.shape[2])
            main, _ = _in_proj(h, g_mix, w_main, w_gate)
            a = _sb_core(main.reshape(b, s, -1))
            w_out = sb_w_out[j]
        elif kind == 2:
            w_main, w_gate = _split_w_in(fx_w_in[j], fx_w_in.shape[2] - FX_HEADS)
            main, gates = _in_proj(h, g_mix, w_main, w_gate)
            main = main.reshape(b, s, -1)
            qkn, cum = _fox_prep(main, gates.reshape(b, s, LANES), fx_b_f[j], fx_q_norm_g[j], fx_k_norm_g[j])
            a = _fox_core(main, qkn, cum)
            w_out = fx_w_out[j]
        else:
            w_main, w_gate = _split_w_in(gla_w_in[j], gla_w_in.shape[2] - GLA_RANK)
            main, gates = _in_proj(h, g_mix, w_main, w_gate)
            a = _gla_core(main.reshape(b, s, -1), gates.reshape(b, s, LANES), gla_w_gate_up[j], gla_b_gate[j], gla_norm_g[j])
            w_out = gla_w_out[j]
        h = _out_ffn(h, a.reshape(b * s, -1), w_out.astype(BF16), ffn_norm_g[layer].reshape(1, d),
                     ffn_w_gate[layer].astype(BF16), ffn_w_up[layer].astype(BF16), ffn_w_down[layer].astype(BF16))
    return h.reshape(b, s, d)
```

```python
import functools

import jax
import jax.numpy as jnp
from jax import lax
from jax.experimental import pallas as pl
from jax.experimental.pallas import tpu as pltpu

F32 = jnp.float32
BF16 = jnp.bfloat16

D_MODEL = 1024
EPS = 1e-6
CHUNK = 64
ML_HEADS, ML_DQK, ML_DV = 8, 64, 128
SB_HEADS, SB_DH = 16, 64
FX_HEADS, FX_DH = 16, 64
GLA_HEADS, GLA_DK, GLA_DV = 4, 128, 256
GLA_RANK = 16
GLA_TAU = 16.0
GLA_SUB = 16
NEG = -1e30
LOG2E = 1.4426950408889634
SB_LOG_ZERO = -104.0
LANES = 128
VMEM_LIMIT = 56 * 1024 * 1024


def _cparams(sem):
    return pltpu.CompilerParams(dimension_semantics=sem, vmem_limit_bytes=VMEM_LIMIT)


def _log_sigmoid(x):
    return jnp.minimum(x, 0.0) - jnp.log(1.0 + jnp.exp(-jnp.abs(x)))


def _softplus(x):
    return jnp.maximum(x, 0.0) + jnp.log(1.0 + jnp.exp(-jnp.abs(x)))


def _sigmoid(x):
    return 1.0 / (1.0 + jnp.exp(-x))


def _split_bf16(x):
    hi = x.astype(BF16)
    lo = (x - hi.astype(F32)).astype(BF16)
    return hi, lo


def _dot(a, b):
    return jnp.dot(a, b, preferred_element_type=F32)


def _dot_nt(a, b):
    return lax.dot_general(a, b, (((1,), (1,)), ((), ())), preferred_element_type=F32)


def _dot_tn(a, b):
    return lax.dot_general(a, b, (((0,), (0,)), ((), ())), preferred_element_type=F32)


def _tri_dot(tri, x):
    hi, lo = _split_bf16(x)
    return _dot(tri, hi) + _dot(tri, lo)


def _dot_tri(x, tri):
    hi, lo = _split_bf16(x)
    return _dot(hi, tri) + _dot(lo, tri)


def _in_proj_kernel(x_ref, g_ref, w_ref, wg_ref, main_ref, gate_ref, *, tn):
    x = x_ref[...]
    ms = jnp.mean(x * x, axis=-1, keepdims=True)
    u = (x * lax.rsqrt(ms + EPS) * g_ref[...]).astype(BF16)
    for c in range(main_ref.shape[-1] // tn):
        cs = slice(c * tn, (c + 1) * tn)
        main_ref[:, cs] = _dot(u, w_ref[:, cs]).astype(main_ref.dtype)
    gate_ref[...] = _dot(u, wg_ref[...])


def _in_proj(x, g, w_main, w_gate, *, tm=512, tn=512):
    n, d = x.shape
    wn = w_main.shape[1]
    return pl.pallas_call(
        functools.partial(_in_proj_kernel, tn=tn),
        grid=(n // tm,),
        in_specs=[
            pl.BlockSpec((tm, d), lambda i: (i, 0)),
            pl.BlockSpec((1, d), lambda i: (0, 0)),
            pl.BlockSpec((d, wn), lambda i: (0, 0)),
            pl.BlockSpec((d, LANES), lambda i: (0, 0)),
        ],
        out_specs=[
            pl.BlockSpec((tm, wn), lambda i: (i, 0)),
            pl.BlockSpec((tm, LANES), lambda i: (i, 0)),
        ],
        out_shape=[
            jax.ShapeDtypeStruct((n, wn), BF16),
            jax.ShapeDtypeStruct((n, LANES), F32),
        ],
        compiler_params=_cparams(("parallel",)),
        name="in_proj",
    )(x, g, w_main, w_gate)


def _out_ffn_kernel(h_ref, a_ref, wo_ref, g_ref, wg_ref, wu_ref, wd_ref, o_ref, u_s):
    f = pl.program_id(1)

    @pl.when(f == 0)
    def _():
        h1 = h_ref[...] + _dot(a_ref[...], wo_ref[...])
        o_ref[...] = h1
        ms = jnp.mean(h1 * h1, axis=-1, keepdims=True)
        u_s[...] = (h1 * lax.rsqrt(ms + EPS) * g_ref[...]).astype(BF16)

    u = u_s[...]
    gt = _dot(u, wg_ref[...])
    up = _dot(u, wu_ref[...])
    hid = (gt * _sigmoid(gt) * up).astype(BF16)
    o_ref[...] += _dot(hid, wd_ref[...])


def _out_ffn(h, a, w_out, g, w_gate, w_up, w_down, *, tm=512, tf=256):
    n, d = h.shape
    dff = w_gate.shape[1]
    return pl.pallas_call(
        _out_ffn_kernel,
        grid=(n // tm, dff // tf),
        in_specs=[
            pl.BlockSpec((tm, d), lambda i, f: (i, 0)),
            pl.BlockSpec((tm, d), lambda i, f: (i, 0)),
            pl.BlockSpec((d, d), lambda i, f: (0, 0)),
            pl.BlockSpec((1, d), lambda i, f: (0, 0)),
            pl.BlockSpec((d, tf), lambda i, f: (0, f)),
            pl.BlockSpec((d, tf), lambda i, f: (0, f)),
            pl.BlockSpec((tf, d), lambda i, f: (f, 0)),
        ],
        out_specs=pl.BlockSpec((tm, d), lambda i, f: (i, 0)),
        out_shape=jax.ShapeDtypeStruct((n, d), F32),
        scratch_shapes=[pltpu.VMEM((tm, d), BF16)],
        compiler_params=_cparams(("parallel", "arbitrary")),
        name="out_ffn",
    )(h, a, w_out, g, w_gate, w_up, w_down)


def _mlstm_kernel(main_ref, gc_ref, gr_ref, bc_ref, br_ref, ng_ref, o_ref, c_s, m_s):
    L = CHUNK
    H = ML_HEADS
    scale = ML_DQK ** -0.5

    @pl.when(pl.program_id(1) == 0)
    def _():
        c_s[...] = jnp.zeros_like(c_s)
        m_s[...] = jnp.zeros_like(m_s)

    row = lax.broadcasted_iota(jnp.int32, (L, L), 0)
    col = lax.broadcasted_iota(jnp.int32, (L, L), 1)
    causal = col <= row
    tri = causal.astype(BF16)
    tri_t = (row <= col).astype(BF16)

    gc = gc_ref[0] + bc_ref[...]
    gr = gr_ref[0, 0] + br_ref[...]
    cum_c = _tri_dot(tri, _log_sigmoid(gc))
    li_r = gr[0:H]
    cum_r = _dot_tri(_log_sigmoid(gr[H:2 * H]), tri_t)

    ones_blk = (lax.broadcasted_iota(jnp.int32, (L, ML_DV), 1) == 0).astype(BF16)

    for h in range(H):
        q = main_ref[0, :, h * ML_DQK:(h + 1) * ML_DQK]
        k = main_ref[0, :, H * ML_DQK + h * ML_DQK:H * ML_DQK + (h + 1) * ML_DQK]
        v = main_ref[0, :, 2 * H * ML_DQK + h * ML_DV:2 * H * ML_DQK + (h + 1) * ML_DV]
        op = main_ref[0, :, 2 * H * ML_DQK + H * ML_DV + h * ML_DV:
                      2 * H * ML_DQK + H * ML_DV + (h + 1) * ML_DV]
        vext = jnp.concatenate([v, ones_blk], axis=1)

        cum_c1 = cum_c[:, H + h:H + h + 1]
        li_c1 = gc[:, h:h + 1]
        cum_r1 = cum_r[h:h + 1, :]
        li_r1 = li_r[h:h + 1, :]
        m_st = m_s[h][:, 0:1]
        cext = c_s[h]

        log_d = jnp.where(causal, cum_c1 - cum_r1 + li_r1, NEG)
        m_t = jnp.maximum(cum_c1 + m_st, jnp.max(log_d, axis=-1, keepdims=True))
        dmat = jnp.exp(log_d - m_t)
        w_inter = jnp.exp(cum_c1 + m_st - m_t)
        sc = (_dot_nt(q, k) * (dmat * scale)).astype(BF16)
        tot = w_inter * _dot(q, cext.astype(BF16)) + _dot(sc, vext)
        num = tot[:, :ML_DV]
        den = tot[:, ML_DV:ML_DV + 1]
        hh = num / jnp.maximum(jnp.abs(den), jnp.exp(-m_t))
        hn = hh * lax.rsqrt(jnp.mean(hh * hh, axis=-1, keepdims=True) + EPS)
        out = hn * ng_ref[:, h * ML_DV:(h + 1) * ML_DV] * _sigmoid(op.astype(F32))
        o_ref[0, :, h * ML_DV:(h + 1) * ML_DV] = out.astype(o_ref.dtype)

        tot_f = cum_r1[:, L - 1:L]
        log_w_r = tot_f - cum_r1 + li_r1
        m_new = jnp.maximum(tot_f + m_st, jnp.max(log_w_r, axis=-1, keepdims=True))
        decay = jnp.exp(tot_f + m_st - m_new)
        w_c = jnp.exp(tot_f - cum_c1 + li_c1 - m_new) * scale
        wk = (k.astype(F32) * w_c).astype(BF16)
        c_s[h] = decay * cext + _dot_tn(wk, vext)
        m_s[h] = jnp.broadcast_to(m_new, (1, LANES))


def _mlstm_core(main, gates, b_if, norm_g):
    b, s, _ = main.shape
    nc = s // CHUNK
    h = ML_HEADS
    gr = gates[:, :, :2 * h].reshape(b, nc, CHUNK, 2 * h).transpose(0, 1, 3, 2)
    bc = jnp.zeros((1, LANES), F32).at[0, :2 * h].set(b_if)
    br = b_if.reshape(2 * h, 1)
    return pl.pallas_call(
        _mlstm_kernel,
        grid=(b, nc),
        in_specs=[
            pl.BlockSpec((1, CHUNK, main.shape[2]), lambda i, c: (i, c, 0)),
            pl.BlockSpec((1, CHUNK, LANES), lambda i, c: (i, c, 0)),
            pl.BlockSpec((1, 1, 2 * h, CHUNK), lambda i, c: (i, c, 0, 0)),
            pl.BlockSpec((1, LANES), lambda i, c: (0, 0)),
            pl.BlockSpec((2 * h, 1), lambda i, c: (0, 0)),
            pl.BlockSpec((1, h * ML_DV), lambda i, c: (0, 0)),
        ],
        out_specs=pl.BlockSpec((1, CHUNK, h * ML_DV), lambda i, c: (i, c, 0)),
        out_shape=jax.ShapeDtypeStruct((b, s, h * ML_DV), BF16),
        scratch_shapes=[
            pltpu.VMEM((h, ML_DQK, 2 * ML_DV), F32),
            pltpu.VMEM((h, 1, LANES), F32),
        ],
        compiler_params=_cparams(("parallel", "arbitrary")),
        name="mlstm_core",
    )(main, gates, gr, bc, br, norm_g.reshape(1, -1))


def _sb_kernel(q_ref, k_ref, v_ref, o_ref, acc_s, *, tq, tk):
    qi = pl.program_id(2)
    nsub = tq // tk
    lane = lax.broadcasted_iota(jnp.int32, (1, LANES), 1)
    lo = lane < SB_DH
    sels = (lo, jnp.logical_not(lo))
    row = lax.broadcasted_iota(jnp.int32, (tk, tk), 0)
    col = lax.broadcasted_iota(jnp.int32, (tk, tk), 1)
    valid = col < row
    upper = (row > col).astype(BF16)
    q2 = q_ref[0] * jnp.asarray(SB_DH ** -0.5, BF16)
    qms = [jnp.where(sel, q2, jnp.zeros((), BF16)) for sel in sels]

    def tile(q_rows, kt, carries, masked):
        k0 = pl.multiple_of(kt * tk, tk)
        kk = k_ref[0, pl.ds(k0, tk), :]
        vv = v_ref[0, pl.ds(k0, tk), :]
        contrib = None
        new_carries = []
        for hh in range(2):
            z = _dot_nt(q_rows[hh], kk)
            w = jnp.log(1.0 + jnp.exp(-jnp.abs(z)))
            lp = -(jnp.maximum(z, 0.0) + w)
            if masked:
                lp = jnp.where(valid, lp, 0.0)
            keep = _dot_tri(lp, upper) + carries[hh]
            a = jnp.exp(jnp.minimum(z, 0.0) - w + keep)
            if masked:
                a = jnp.where(valid, a, 0.0)
            c = _dot(a.astype(BF16), jnp.where(sels[hh], vv, jnp.zeros((), BF16)))
            contrib = c if contrib is None else contrib + c
            new_carries.append(carries[hh] + jnp.sum(lp, axis=-1, keepdims=True))
        return new_carries, contrib

    zero_c = jnp.zeros((tk, 1), F32)
    blk_carries, blk_acc = [], []
    for rb in range(nsub):
        q_rows = [qm[rb * tk:(rb + 1) * tk] for qm in qms]
        cs, acc = tile(q_rows, nsub * qi + rb, [zero_c, zero_c], True)
        for kb in range(rb - 1, -1, -1):
            cs, c2 = tile(q_rows, nsub * qi + kb, cs, False)
            acc = acc + c2
        blk_carries.append(cs)
        blk_acc.append(acc)
    c0 = jnp.concatenate([cs[0] for cs in blk_carries], axis=0)
    c1 = jnp.concatenate([cs[1] for cs in blk_carries], axis=0)
    acc_s[...] = jnp.concatenate(blk_acc, axis=0)

    def live(c0, c1):
        return jnp.maximum(jnp.max(c0), jnp.max(c1))

    def cond(st):
        return jnp.logical_and(st[0] >= 0, st[1] > SB_LOG_ZERO)

    def body(st):
        kt, _, c0, c1 = st
        (c0, c1), contrib = tile(qms, kt, [c0, c1], False)
        acc_s[...] += contrib
        return kt - 1, live(c0, c1), c0, c1

    lax.while_loop(cond, body, (nsub * qi - 1, live(c0, c1), c0, c1))
    o_ref[0] = acc_s[...].astype(o_ref.dtype)


def _sb_core(main, *, tq=256, tk=128):
    b, s, _ = main.shape
    hp = SB_HEADS // 2
    return pl.pallas_call(
        functools.partial(_sb_kernel, tq=tq, tk=tk),
        grid=(b, hp, s // tq),
        in_specs=[
            pl.BlockSpec((1, tq, LANES), lambda i, p, j: (i, j, p)),
            pl.BlockSpec((1, s, LANES), lambda i, p, j: (i, 0, hp + p)),
            pl.BlockSpec((1, s, LANES), lambda i, p, j: (i, 0, 2 * hp + p)),
        ],
        out_specs=pl.BlockSpec((1, tq, LANES), lambda i, p, j: (i, j, p)),
        out_shape=jax.ShapeDtypeStruct((b, s, SB_HEADS * SB_DH), BF16),
        scratch_shapes=[pltpu.VMEM((tq, LANES), F32)],
        compiler_params=_cparams(("parallel", "parallel", "arbitrary")),
        name="sb_core",
    )(main, main, main)


def _fox_prep_kernel(main_ref, fp_ref, bf_ref, gq_ref, gk_ref, qa_ref, ka_ref, carry_s, *, ts):
    @pl.when(pl.program_id(1) == 0)
    def _():
        carry_s[...] = jnp.zeros_like(carry_s)

    row = lax.broadcasted_iota(jnp.int32, (ts, ts), 0)
    col = lax.broadcasted_iota(jnp.int32, (ts, ts), 1)
    tri = (col <= row).astype(BF16)
    lf = _log_sigmoid(fp_ref[0] + bf_ref[...])
    cum = _tri_dot(tri, lf) + carry_s[...]
    carry_s[...] = cum[ts - 1:ts, :]
    c2 = cum * LOG2E
    c_hi = c2.astype(BF16).astype(F32)
    c_mid = (c2 - c_hi).astype(BF16).astype(F32)
    c_lo = (c2 - c_hi - c_mid).astype(BF16).astype(F32)

    lane = lax.broadcasted_iota(jnp.int32, (1, LANES), 1)
    lo = lane < FX_DH
    a0, a1, a2, a3, a4, a5 = (lane == FX_DH + i for i in range(6))
    n_q = FX_HEADS * FX_DH // LANES
    for c in range(2 * n_q):
        is_q = c < n_q
        xc = main_ref[0, :, c * LANES:(c + 1) * LANES].astype(F32)
        x2 = xc * xc
        s_lo = jnp.sum(jnp.where(lo, x2, 0.0), axis=-1, keepdims=True)
        s_hi = jnp.sum(jnp.where(lo, 0.0, x2), axis=-1, keepdims=True)
        r = jnp.where(lo, lax.rsqrt(s_lo / FX_DH + EPS), lax.rsqrt(s_hi / FX_DH + EPS))
        xn = xc * r * (gq_ref[...] if is_q else gk_ref[...])
        for hh in range(2):
            head = 2 * (c % n_q) + hh
            xh = xn if hh == 0 else pltpu.roll(xn, FX_DH, axis=1)
            b_hi = c_hi[:, head:head + 1]
            b_mid = c_mid[:, head:head + 1]
            b_lo = c_lo[:, head:head + 1]
            if is_q:
                aug = jnp.where(a0, b_hi, jnp.where(a1, b_mid, jnp.where(a2, b_lo, jnp.where(a3 | a4 | a5, 1.0, 0.0))))
                qa_ref[0, head] = jnp.where(lo, xh, aug).astype(qa_ref.dtype)
            else:
                aug = jnp.where(a3, -b_hi, jnp.where(a4, -b_mid, jnp.where(a5, -b_lo, jnp.where(a0 | a1 | a2, 1.0, 0.0))))
                ka_ref[0, head] = jnp.where(lo, xh, aug).astype(ka_ref.dtype)


def _fox_prep(main, f_pre, b_f, q_norm_g, k_norm_g, *, ts=256):
    b, s, _ = main.shape
    w = 2 * FX_HEADS * FX_DH
    bf = jnp.zeros((1, LANES), F32).at[0, :FX_HEADS].set(b_f)
    gq = jnp.tile(q_norm_g, 2).reshape(1, LANES) * (FX_DH ** -0.5 * LOG2E)
    gk = jnp.tile(k_norm_g, 2).reshape(1, LANES)
    head_rows = pl.BlockSpec((1, FX_HEADS, ts, LANES), lambda i, j: (i, 0, j, 0))
    return pl.pallas_call(
        functools.partial(_fox_prep_kernel, ts=ts),
        grid=(b, s // ts),
        in_specs=[
            pl.BlockSpec((1, ts, w), lambda i, j: (i, j, 0)),
            pl.BlockSpec((1, ts, LANES), lambda i, j: (i, j, 0)),
            pl.BlockSpec((1, LANES), lambda i, j: (0, 0)),
            pl.BlockSpec((1, LANES), lambda i, j: (0, 0)),
            pl.BlockSpec((1, LANES), lambda i, j: (0, 0)),
        ],
        out_specs=[head_rows, head_rows],
        out_shape=[jax.ShapeDtypeStruct((b, FX_HEADS, s, LANES), BF16)] * 2,
        scratch_shapes=[pltpu.VMEM((1, LANES), F32)],
        compiler_params=_cparams(("parallel", "arbitrary")),
        name="fox_prep",
    )(main, f_pre, bf, gq, gk)


def _fox_kernel(q_ref, k_ref, v_ref, op_ref, o_ref, acc_s, *, t):
    qi = pl.program_id(2)
    lane = lax.broadcasted_iota(jnp.int32, (1, LANES), 1)
    lo = lane < FX_DH
    sels = (lo, jnp.logical_not(lo))
    row = lax.broadcasted_iota(jnp.int32, (t, t), 0)
    col = lax.broadcasted_iota(jnp.int32, (t, t), 1)
    causal = col <= row
    qs = (q_ref[0, 0], q_ref[0, 1])

    def tile(kt, ms, masked):
        k0 = pl.multiple_of(kt * t, t)
        vv = v_ref[0, pl.ds(k0, t), :]
        new_ms = []
        for hh in range(2):
            s = _dot_nt(qs[hh], k_ref[0, hh, pl.ds(k0, t), :])
            if masked:
                s = jnp.where(causal, s, NEG)
            m_new = jnp.maximum(ms[hh], jnp.max(s, axis=-1, keepdims=True))
            alpha = jnp.exp2(ms[hh] - m_new)
            p = jnp.exp2(s - m_new).astype(BF16)
            vext = jnp.where(sels[hh], vv, jnp.ones((), BF16))
            acc_s[hh] = alpha * acc_s[hh] + _dot(p, vext)
            new_ms.append(m_new)
        return tuple(new_ms)

    acc_s[...] = jnp.zeros_like(acc_s)
    m0 = jnp.full((t, 1), NEG, F32)
    ms = lax.fori_loop(0, qi, lambda kt, ms: tile(kt, ms, False), (m0, m0))
    tile(qi, ms, True)
    acc0 = acc_s[0]
    acc1 = acc_s[1]
    out = jnp.where(lo, acc0 / acc0[:, FX_DH:FX_DH + 1], acc1 / acc1[:, 0:1])
    o_ref[0] = (out * _sigmoid(op_ref[0].astype(F32))).astype(o_ref.dtype)


def _fox_core(main, qa, ka, *, t=256):
    b, s, _ = main.shape
    hp = FX_HEADS // 2
    return pl.pallas_call(
        functools.partial(_fox_kernel, t=t),
        grid=(b, hp, s // t),
        in_specs=[
            pl.BlockSpec((1, 2, t, LANES), lambda i, p, j: (i, p, j, 0)),
            pl.BlockSpec((1, 2, s, LANES), lambda i, p, j: (i, p, 0, 0)),
            pl.BlockSpec((1, s, LANES), lambda i, p, j: (i, 0, 2 * hp + p)),
            pl.BlockSpec((1, t, LANES), lambda i, p, j: (i, j, 3 * hp + p)),
        ],
        out_specs=pl.BlockSpec((1, t, LANES), lambda i, p, j: (i, j, p)),
        out_shape=jax.ShapeDtypeStruct((b, s, FX_HEADS * FX_DH), BF16),
        scratch_shapes=[pltpu.VMEM((2, t, LANES), F32)],
        compiler_params=_cparams(("parallel", "parallel", "arbitrary")),
        name="fox_core",
    )(qa, ka, main, main)


def _gla_kernel(main_ref, gl_ref, wgu_ref, bg_ref, ng_ref, o_ref, st_s, cum_s, k_s, v_s):
    L = CHUNK
    H = GLA_HEADS
    SUB = GLA_SUB
    nk = H * GLA_DK
    nv = H * GLA_DV
    scale = GLA_DK ** -0.5

    @pl.when(pl.program_id(1) == 0)
    def _():
        st_s[...] = jnp.zeros_like(st_s)

    row = lax.broadcasted_iota(jnp.int32, (L, L), 0)
    col = lax.broadcasted_iota(jnp.int32, (L, L), 1)
    tri = (col <= row).astype(BF16)
    eye = (lax.broadcasted_iota(jnp.int32, (GLA_DK, GLA_DK), 0)
           == lax.broadcasted_iota(jnp.int32, (GLA_DK, GLA_DK), 1))
    t_sub = lax.broadcasted_iota(jnp.int32, (SUB, 1), 0)

    g_hi, g_lo = _split_bf16(gl_ref[0])
    w_hi, w_lo = _split_bf16(wgu_ref[...])
    logits = _dot(g_hi, w_hi) + _dot(g_hi, w_lo) + _dot(g_lo, w_hi)
    la = _log_sigmoid(logits + bg_ref[...]) * (1.0 / GLA_TAU)
    cum_all = _tri_dot(tri, la)

    for h in range(H):
        ks = slice(h * GLA_DK, (h + 1) * GLA_DK)
        vs = slice(h * GLA_DV, (h + 1) * GLA_DV)
        cum = cum_all[:, ks]
        tot = cum[L - 1:L, :]
        q = main_ref[0, :, ks].astype(F32) * scale
        k = main_ref[0, :, nk + h * GLA_DK:nk + (h + 1) * GLA_DK].astype(F32)
        v_bf = main_ref[0, :, 2 * nk + h * GLA_DV:2 * nk + (h + 1) * GLA_DV]
        state = st_s[h]

        inter = _dot((q * jnp.exp(cum)).astype(BF16), state.astype(BF16))

        cum_s[...] = cum
        k_s[...] = k
        v_s[...] = v_bf.astype(F32)

        for i in range(L // SUB):
            rs = slice(i * SUB, (i + 1) * SUB)
            qb = q[rs]
            cb = cum[rs]
            hb = inter[rs]
            if i > 0:
                ref_pt = cum[i * SUB - 1:i * SUB, :]
                qs = (qb * jnp.exp(cb - ref_pt)).astype(BF16)
                kprev = (k[:i * SUB] * jnp.exp(ref_pt - cum[:i * SUB])).astype(BF16)
                a_off = _dot_nt(qs, kprev)
                hb = hb + _dot(a_off.astype(BF16), v_bf[:i * SUB])
            for j in range(SUB):
                rj = i * SUB + j
                kj = k_s[rj:rj + 1, :]
                cj = cum_s[rj:rj + 1, :]
                vj = v_s[rj:rj + 1, :]
                e = jnp.exp(jnp.where(t_sub >= j, cb - cj, NEG))
                p = jnp.sum(qb * kj * e, axis=-1, keepdims=True)
                hb = hb + p * vj
            hn = hb * lax.rsqrt(jnp.mean(hb * hb, axis=-1, keepdims=True) + EPS)
            r = main_ref[0, rs, 2 * nk + nv + h * GLA_DV:2 * nk + nv + (h + 1) * GLA_DV].astype(F32)
            out = hn * ng_ref[:, vs] * (r * _sigmoid(r))
            o_ref[0, rs, vs] = out.astype(o_ref.dtype)

        kd = (k * jnp.exp(tot - cum)).astype(BF16)
        tot_col = jnp.sum(jnp.where(eye, jnp.broadcast_to(tot, (GLA_DK, GLA_DK)), 0.0),
                          axis=-1, keepdims=True)
        st_s[h] = jnp.exp(tot_col) * state + _dot_tn(kd, v_bf)


def _gla_core(main, g_low, w_gate_up, b_gate, norm_g):
    b, s, _ = main.shape
    nc = s // CHUNK
    nk = GLA_HEADS * GLA_DK
    nv = GLA_HEADS * GLA_DV
    wgu = jnp.zeros((LANES, nk), F32).at[:GLA_RANK].set(w_gate_up)
    return pl.pallas_call(
        _gla_kernel,
        grid=(b, nc),
        in_specs=[
            pl.BlockSpec((1, CHUNK, main.shape[2]), lambda i, c: (i, c, 0)),
            pl.BlockSpec((1, CHUNK, LANES), lambda i, c: (i, c, 0)),
            pl.BlockSpec((LANES, nk), lambda i, c: (0, 0)),
            pl.BlockSpec((1, nk), lambda i, c: (0, 0)),
            pl.BlockSpec((1, nv), lambda i, c: (0, 0)),
        ],
        out_specs=pl.BlockSpec((1, CHUNK, nv), lambda i, c: (i, c, 0)),
        out_shape=jax.ShapeDtypeStruct((b, s, nv), BF16),
        scratch_shapes=[
            pltpu.VMEM((GLA_HEADS, GLA_DK, GLA_DV), F32),
            pltpu.VMEM((CHUNK, GLA_DK), F32),
            pltpu.VMEM((CHUNK, GLA_DK), F32),
            pltpu.VMEM((CHUNK, GLA_DV), F32),
        ],
        compiler_params=_cparams(("parallel", "arbitrary")),
        name="gla_core",
    )(main, g_low, wgu, b_gate.reshape(1, -1), norm_g.reshape(1, -1))


def _split_w_in(w_in, n_main):
    d = w_in.shape[0]
    n_gate = w_in.shape[1] - n_main
    w_gate = jnp.zeros((d, LANES), F32).at[:, :n_gate].set(w_in[:, n_main:])
    return w_in[:, :n_main].astype(BF16), w_gate.astype(BF16)


def kernel(x, mix_norm_g, ffn_norm_g, ffn_w_gate, ffn_w_up, ffn_w_down, ml_w_in, ml_b_if, ml_norm_g, ml_w_out, sb_w_in, sb_w_out, fx_w_in, fx_b_f, fx_q_norm_g, fx_k_norm_g, fx_w_out, gla_w_in, gla_w_gate_up, gla_b_gate, gla_norm_g, gla_w_out):
    b, s, d = x.shape
    depth = mix_norm_g.shape[0]
    h = x.reshape(b * s, d)
    for layer in range(depth):
        kind = layer % 4
        j = layer // 4
        g_mix = mix_norm_g[layer].reshape(1, d)
        if kind == 0:
            w_main, w_gate = _split_w_in(ml_w_in[j], ml_w_in.shape[2] - 2 * ML_HEADS)
            main, gates = _in_proj(h, g_mix, w_main, w_gate)
            a = _mlstm_core(main.reshape(b, s, -1), gates.reshape(b, s, LANES), ml_b_if[j], ml_norm_g[j])
            w_out = ml_w_out[j]
        elif kind == 1:
            w_main, w_gate = _split_w_in(sb_w_in[j], sb_w_in.shape[2])
            main, _ = _in_proj(h, g_mix, w_main, w_gate)
            a = _sb_core(main.reshape(b, s, -1))
            w_out = sb_w_out[j]
        elif kind == 2:
            w_main, w_gate = _split_w_in(fx_w_in[j], fx_w_in.shape[2] - FX_HEADS)
            main, gates = _in_proj(h, g_mix, w_main, w_gate)
            main = main.reshape(b, s, -1)
            qa, ka = _fox_prep(main, gates.reshape(b, s, LANES), fx_b_f[j], fx_q_norm_g[j], fx_k_norm_g[j])
            a = _fox_core(main, qa, ka)
            w_out = fx_w_out[j]
        else:
            w_main, w_gate = _split_w_in(gla_w_in[j], gla_w_in.shape[2] - GLA_RANK)
            main, gates = _in_proj(h, g_mix, w_main, w_gate)
            a = _gla_core(main.reshape(b, s, -1), gates.reshape(b, s, LANES), gla_w_gate_up[j], gla_b_gate[j], gla_norm_g[j])
            w_out = gla_w_out[j]
        h = _out_ffn(h, a.reshape(b * s, -1), w_out.astype(BF16), ffn_norm_g[layer].reshape(1, d),
                     ffn_w_gate[layer].astype(BF16), ffn_w_up[layer].astype(BF16), ffn_w_down[layer].astype(BF16))
    return h.reshape(b, s, d)
```

```python
import functools

import jax
import jax.numpy as jnp
from jax import lax
from jax.experimental import pallas as pl
from jax.experimental.pallas import tpu as pltpu

F32 = jnp.float32
BF16 = jnp.bfloat16

D_MODEL = 1024
EPS = 1e-6
CHUNK = 64
ML_HEADS, ML_DQK, ML_DV = 8, 64, 128
SB_HEADS, SB_DH = 16, 64
FX_HEADS, FX_DH = 16, 64
GLA_HEADS, GLA_DK, GLA_DV = 4, 128, 256
GLA_RANK = 16
GLA_TAU = 16.0
GLA_SUB = 16
NEG = -1e30
LOG2E = 1.4426950408889634
FX_MAX_SHIFT = 60.0
SB_LOG_ZERO = -104.0
LANES = 128
VMEM_LIMIT = 56 * 1024 * 1024


def _cparams(sem):
    return pltpu.CompilerParams(dimension_semantics=sem, vmem_limit_bytes=VMEM_LIMIT)


def _resident(shape):
    return pl.BlockSpec(shape, lambda i: (0,) * len(shape), pipeline_mode=pl.Buffered(1))


def _log_sigmoid(x):
    return jnp.minimum(x, 0.0) - jnp.log(1.0 + jnp.exp(-jnp.abs(x)))


def _softplus(x):
    return jnp.maximum(x, 0.0) + jnp.log(1.0 + jnp.exp(-jnp.abs(x)))


def _sigmoid(x):
    return 1.0 / (1.0 + jnp.exp(-x))


def _split_bf16(x):
    hi = x.astype(BF16)
    lo = (x - hi.astype(F32)).astype(BF16)
    return hi, lo


def _dot(a, b):
    return jnp.dot(a, b, preferred_element_type=F32)


def _dot_nt(a, b):
    return lax.dot_general(a, b, (((1,), (1,)), ((), ())), preferred_element_type=F32)


def _dot_tn(a, b):
    return lax.dot_general(a, b, (((0,), (0,)), ((), ())), preferred_element_type=F32)


def _tri_dot(tri, x):
    hi, lo = _split_bf16(x)
    return _dot(tri, hi) + _dot(tri, lo)


def _dot_tri(x, tri):
    hi, lo = _split_bf16(x)
    return _dot(hi, tri) + _dot(lo, tri)


def _in_proj_kernel(x_ref, g_ref, w_ref, wg_ref, main_ref, gate_ref, *, tn):
    x = x_ref[...]
    ms = jnp.mean(x * x, axis=-1, keepdims=True)
    u = (x * lax.rsqrt(ms + EPS) * g_ref[...]).astype(BF16)
    for c in range(main_ref.shape[-1] // tn):
        cs = slice(c * tn, (c + 1) * tn)
        main_ref[:, cs] = _dot(u, w_ref[:, cs]).astype(main_ref.dtype)
    gate_ref[...] = _dot(u, wg_ref[...])


def _in_proj(x, g, w_main, w_gate, *, tm=512, tn=512):
    n, d = x.shape
    wn = w_main.shape[1]
    return pl.pallas_call(
        functools.partial(_in_proj_kernel, tn=tn),
        grid=(n // tm,),
        in_specs=[
            pl.BlockSpec((tm, d), lambda i: (i, 0)),
            _resident((1, d)),
            _resident((d, wn)),
            _resident((d, LANES)),
        ],
        out_specs=[
            pl.BlockSpec((tm, wn), lambda i: (i, 0)),
            pl.BlockSpec((tm, LANES), lambda i: (i, 0)),
        ],
        out_shape=[
            jax.ShapeDtypeStruct((n, wn), BF16),
            jax.ShapeDtypeStruct((n, LANES), F32),
        ],
        compiler_params=_cparams(("parallel",)),
        name="in_proj",
    )(x, g, w_main, w_gate)


def _out_ffn_kernel(h_ref, a_ref, wo_ref, g_ref, wg_ref, wu_ref, wd_ref, o_ref, *, tf):
    h1 = h_ref[...] + _dot(a_ref[...], wo_ref[...])
    ms = jnp.mean(h1 * h1, axis=-1, keepdims=True)
    u = (h1 * lax.rsqrt(ms + EPS) * g_ref[...]).astype(BF16)
    acc = h1
    for c in range(wg_ref.shape[1] // tf):
        cs = slice(c * tf, (c + 1) * tf)
        gt = _dot(u, wg_ref[:, cs])
        up = _dot(u, wu_ref[:, cs])
        hid = (gt * _sigmoid(gt) * up).astype(BF16)
        acc = acc + _dot(hid, wd_ref[cs, :])
    o_ref[...] = acc


def _out_ffn(h, a, w_out, g, w_gate, w_up, w_down, *, tm=512, tf=256):
    n, d = h.shape
    dff = w_gate.shape[1]
    return pl.pallas_call(
        functools.partial(_out_ffn_kernel, tf=tf),
        grid=(n // tm,),
        in_specs=[
            pl.BlockSpec((tm, d), lambda i: (i, 0)),
            pl.BlockSpec((tm, d), lambda i: (i, 0)),
            _resident((d, d)),
            _resident((1, d)),
            _resident((d, dff)),
            _resident((d, dff)),
            _resident((dff, d)),
        ],
        out_specs=pl.BlockSpec((tm, d), lambda i: (i, 0)),
        out_shape=jax.ShapeDtypeStruct((n, d), F32),
        compiler_params=_cparams(("parallel",)),
        name="out_ffn",
    )(h, a, w_out, g, w_gate, w_up, w_down)


def _mlstm_kernel(main_ref, gc_ref, gr_ref, bc_ref, br_ref, ng_ref, o_ref, c_s, m_s):
    L = CHUNK
    H = ML_HEADS
    scale = ML_DQK ** -0.5

    @pl.when(pl.program_id(1) == 0)
    def _():
        c_s[...] = jnp.zeros_like(c_s)
        m_s[...] = jnp.zeros_like(m_s)

    row = lax.broadcasted_iota(jnp.int32, (L, L), 0)
    col = lax.broadcasted_iota(jnp.int32, (L, L), 1)
    causal = col <= row
    tri = causal.astype(BF16)
    tri_t = (row <= col).astype(BF16)

    gc = gc_ref[0] + bc_ref[...]
    gr = gr_ref[0, 0] + br_ref[...]
    cum_c = _tri_dot(tri, _log_sigmoid(gc))
    li_r = gr[0:H]
    cum_r = _dot_tri(_log_sigmoid(gr[H:2 * H]), tri_t)

    ones_blk = (lax.broadcasted_iota(jnp.int32, (L, ML_DV), 1) == 0).astype(BF16)

    for h in range(H):
        q = main_ref[0, :, h * ML_DQK:(h + 1) * ML_DQK]
        k = main_ref[0, :, H * ML_DQK + h * ML_DQK:H * ML_DQK + (h + 1) * ML_DQK]
        v = main_ref[0, :, 2 * H * ML_DQK + h * ML_DV:2 * H * ML_DQK + (h + 1) * ML_DV]
        op = main_ref[0, :, 2 * H * ML_DQK + H * ML_DV + h * ML_DV:
                      2 * H * ML_DQK + H * ML_DV + (h + 1) * ML_DV]
        vext = jnp.concatenate([v, ones_blk], axis=1)

        cum_c1 = cum_c[:, H + h:H + h + 1]
        li_c1 = gc[:, h:h + 1]
        cum_r1 = cum_r[h:h + 1, :]
        li_r1 = li_r[h:h + 1, :]
        m_st = m_s[h][:, 0:1]
        cext = c_s[h]

        log_d = jnp.where(causal, cum_c1 - cum_r1 + li_r1, NEG)
        m_t = jnp.maximum(cum_c1 + m_st, jnp.max(log_d, axis=-1, keepdims=True))
        dmat = jnp.exp(log_d - m_t)
        w_inter = jnp.exp(cum_c1 + m_st - m_t)
        sc = (_dot_nt(q, k) * (dmat * scale)).astype(BF16)
        tot = w_inter * _dot(q, cext.astype(BF16)) + _dot(sc, vext)
        num = tot[:, :ML_DV]
        den = tot[:, ML_DV:ML_DV + 1]
        hh = num / jnp.maximum(jnp.abs(den), jnp.exp(-m_t))
        hn = hh * lax.rsqrt(jnp.mean(hh * hh, axis=-1, keepdims=True) + EPS)
        out = hn * ng_ref[:, h * ML_DV:(h + 1) * ML_DV] * _sigmoid(op.astype(F32))
        o_ref[0, :, h * ML_DV:(h + 1) * ML_DV] = out.astype(o_ref.dtype)

        tot_f = cum_r1[:, L - 1:L]
        log_w_r = tot_f - cum_r1 + li_r1
        m_new = jnp.maximum(tot_f + m_st, jnp.max(log_w_r, axis=-1, keepdims=True))
        decay = jnp.exp(tot_f + m_st - m_new)
        w_c = jnp.exp(tot_f - cum_c1 + li_c1 - m_new) * scale
        wk = (k.astype(F32) * w_c).astype(BF16)
        c_s[h] = decay * cext + _dot_tn(wk, vext)
        m_s[h] = jnp.broadcast_to(m_new, (1, LANES))


def _mlstm_core(main, gates, b_if, norm_g):
    b, s, _ = main.shape
    nc = s // CHUNK
    h = ML_HEADS
    gr = gates[:, :, :2 * h].reshape(b, nc, CHUNK, 2 * h).transpose(0, 1, 3, 2)
    bc = jnp.zeros((1, LANES), F32).at[0, :2 * h].set(b_if)
    br = b_if.reshape(2 * h, 1)
    return pl.pallas_call(
        _mlstm_kernel,
        grid=(b, nc),
        in_specs=[
            pl.BlockSpec((1, CHUNK, main.shape[2]), lambda i, c: (i, c, 0)),
            pl.BlockSpec((1, CHUNK, LANES), lambda i, c: (i, c, 0)),
            pl.BlockSpec((1, 1, 2 * h, CHUNK), lambda i, c: (i, c, 0, 0)),
            pl.BlockSpec((1, LANES), lambda i, c: (0, 0)),
            pl.BlockSpec((2 * h, 1), lambda i, c: (0, 0)),
            pl.BlockSpec((1, h * ML_DV), lambda i, c: (0, 0)),
        ],
        out_specs=pl.BlockSpec((1, CHUNK, h * ML_DV), lambda i, c: (i, c, 0)),
        out_shape=jax.ShapeDtypeStruct((b, s, h * ML_DV), BF16),
        scratch_shapes=[
            pltpu.VMEM((h, ML_DQK, 2 * ML_DV), F32),
            pltpu.VMEM((h, 1, LANES), F32),
        ],
        compiler_params=_cparams(("parallel", "arbitrary")),
        name="mlstm_core",
    )(main, gates, gr, bc, br, norm_g.reshape(1, -1))


def _sb_kernel(q_ref, k_ref, v_ref, o_ref, acc_s, *, tq, tk):
    qi = pl.program_id(2)
    nsub = tq // tk
    lane = lax.broadcasted_iota(jnp.int32, (1, LANES), 1)
    lo = lane < SB_DH
    sels = (lo, jnp.logical_not(lo))
    row = lax.broadcasted_iota(jnp.int32, (tk, tk), 0)
    col = lax.broadcasted_iota(jnp.int32, (tk, tk), 1)
    valid = col < row
    upper = (row > col).astype(BF16)
    q2 = q_ref[0] * jnp.asarray(SB_DH ** -0.5, BF16)
    qms = [jnp.where(sel, q2, jnp.zeros((), BF16)) for sel in sels]

    def tile(q_rows, kt, carries, masked):
        k0 = pl.multiple_of(kt * tk, tk)
        kk = k_ref[0, pl.ds(k0, tk), :]
        vv = v_ref[0, pl.ds(k0, tk), :]
        contrib = None
        new_carries = []
        for hh in range(2):
            z = _dot_nt(q_rows[hh], kk)
            w = jnp.log(1.0 + jnp.exp(-jnp.abs(z)))
            lp = -(jnp.maximum(z, 0.0) + w)
            if masked:
                lp = jnp.where(valid, lp, 0.0)
            keep = _dot_tri(lp, upper) + carries[hh]
            a = jnp.exp(jnp.minimum(z, 0.0) - w + keep)
            if masked:
                a = jnp.where(valid, a, 0.0)
            c = _dot(a.astype(BF16), jnp.where(sels[hh], vv, jnp.zeros((), BF16)))
            contrib = c if contrib is None else contrib + c
            new_carries.append(carries[hh] + jnp.sum(lp, axis=-1, keepdims=True))
        return new_carries, contrib

    zero_c = jnp.zeros((tk, 1), F32)
    blk_carries, blk_acc = [], []
    for rb in range(nsub):
        q_rows = [qm[rb * tk:(rb + 1) * tk] for qm in qms]
        cs, acc = tile(q_rows, nsub * qi + rb, [zero_c, zero_c], True)
        for kb in range(rb - 1, -1, -1):
            cs, c2 = tile(q_rows, nsub * qi + kb, cs, False)
            acc = acc + c2
        blk_carries.append(cs)
        blk_acc.append(acc)
    c0 = jnp.concatenate([cs[0] for cs in blk_carries], axis=0)
    c1 = jnp.concatenate([cs[1] for cs in blk_carries], axis=0)
    acc_s[...] = jnp.concatenate(blk_acc, axis=0)

    def live(c0, c1):
        return jnp.maximum(jnp.max(c0), jnp.max(c1))

    def cond(st):
        return jnp.logical_and(st[0] >= 0, st[1] > SB_LOG_ZERO)

    def body(st):
        kt, _, c0, c1 = st
        (c0, c1), contrib = tile(qms, kt, [c0, c1], False)
        acc_s[...] += contrib
        return kt - 1, live(c0, c1), c0, c1

    lax.while_loop(cond, body, (nsub * qi - 1, live(c0, c1), c0, c1))
    o_ref[0] = acc_s[...].astype(o_ref.dtype)


def _sb_core(main, *, tq=256, tk=128):
    b, s, _ = main.shape
    hp = SB_HEADS // 2
    return pl.pallas_call(
        functools.partial(_sb_kernel, tq=tq, tk=tk),
        grid=(b, hp, s // tq),
        in_specs=[
            pl.BlockSpec((1, tq, LANES), lambda i, p, j: (i, j, p)),
            pl.BlockSpec((1, s, LANES), lambda i, p, j: (i, 0, hp + p)),
            pl.BlockSpec((1, s, LANES), lambda i, p, j: (i, 0, 2 * hp + p)),
        ],
        out_specs=pl.BlockSpec((1, tq, LANES), lambda i, p, j: (i, j, p)),
        out_shape=jax.ShapeDtypeStruct((b, s, SB_HEADS * SB_DH), BF16),
        scratch_shapes=[pltpu.VMEM((tq, LANES), F32)],
        compiler_params=_cparams(("parallel", "parallel", "arbitrary")),
        name="sb_core",
    )(main, main, main)


def _fox_prep_kernel(main_ref, fp_ref, bf_ref, gq_ref, gk_ref, sh_ref, qa_ref, ka_ref, carry_s, *, ts):
    @pl.when(pl.program_id(1) == 0)
    def _():
        carry_s[...] = jnp.zeros_like(carry_s)

    row = lax.broadcasted_iota(jnp.int32, (ts, ts), 0)
    col = lax.broadcasted_iota(jnp.int32, (ts, ts), 1)
    tri = (col <= row).astype(BF16)
    lf = _log_sigmoid(fp_ref[0] + bf_ref[...])
    cum = _tri_dot(tri, lf) + carry_s[...]
    carry_s[...] = cum[ts - 1:ts, :]

    def split3(c):
        hi = c.astype(BF16).astype(F32)
        mid = (c - hi).astype(BF16).astype(F32)
        return hi, mid, (c - hi - mid).astype(BF16).astype(F32)

    c_k = split3(cum * LOG2E)
    c_q = split3(cum * LOG2E - sh_ref[...])

    lane = lax.broadcasted_iota(jnp.int32, (1, LANES), 1)
    lo = lane < FX_DH
    a0, a1, a2, a3, a4, a5 = (lane == FX_DH + i for i in range(6))
    n_q = FX_HEADS * FX_DH // LANES
    for c in range(2 * n_q):
        is_q = c < n_q
        xc = main_ref[0, :, c * LANES:(c + 1) * LANES].astype(F32)
        x2 = xc * xc
        s_lo = jnp.sum(jnp.where(lo, x2, 0.0), axis=-1, keepdims=True)
        s_hi = jnp.sum(jnp.where(lo, 0.0, x2), axis=-1, keepdims=True)
        r = jnp.where(lo, lax.rsqrt(s_lo / FX_DH + EPS), lax.rsqrt(s_hi / FX_DH + EPS))
        xn = xc * r * (gq_ref[...] if is_q else gk_ref[...])
        for hh in range(2):
            head = 2 * (c % n_q) + hh
            xh = xn if hh == 0 else pltpu.roll(xn, FX_DH, axis=1)
            b_hi, b_mid, b_lo = (t3[:, head:head + 1] for t3 in (c_q if is_q else c_k))
            if is_q:
                aug = jnp.where(a0, b_hi, jnp.where(a1, b_mid, jnp.where(a2, b_lo, jnp.where(a3 | a4 | a5, 1.0, 0.0))))
                qa_ref[0, head] = jnp.where(lo, xh, aug).astype(qa_ref.dtype)
            else:
                aug = jnp.where(a3, -b_hi, jnp.where(a4, -b_mid, jnp.where(a5, -b_lo, jnp.where(a0 | a1 | a2, 1.0, 0.0))))
                ka_ref[0, head] = jnp.where(lo, xh, aug).astype(ka_ref.dtype)


def _fox_prep(main, f_pre, b_f, q_norm_g, k_norm_g, shift, *, ts=256):
    b, s, _ = main.shape
    w = 2 * FX_HEADS * FX_DH
    bf = jnp.zeros((1, LANES), F32).at[0, :FX_HEADS].set(b_f)
    sh = jnp.full((1, LANES), shift, F32)
    gq = jnp.tile(q_norm_g, 2).reshape(1, LANES) * (FX_DH ** -0.5 * LOG2E)
    gk = jnp.tile(k_norm_g, 2).reshape(1, LANES)
    head_rows = pl.BlockSpec((1, FX_HEADS, ts, LANES), lambda i, j: (i, 0, j, 0))
    return pl.pallas_call(
        functools.partial(_fox_prep_kernel, ts=ts),
        grid=(b, s // ts),
        in_specs=[
            pl.BlockSpec((1, ts, w), lambda i, j: (i, j, 0)),
            pl.BlockSpec((1, ts, LANES), lambda i, j: (i, j, 0)),
            pl.BlockSpec((1, LANES), lambda i, j: (0, 0)),
            pl.BlockSpec((1, LANES), lambda i, j: (0, 0)),
            pl.BlockSpec((1, LANES), lambda i, j: (0, 0)),
            pl.BlockSpec((1, LANES), lambda i, j: (0, 0)),
        ],
        out_specs=[head_rows, head_rows],
        out_shape=[jax.ShapeDtypeStruct((b, FX_HEADS, s, LANES), BF16)] * 2,
        scratch_shapes=[pltpu.VMEM((1, LANES), F32)],
        compiler_params=_cparams(("parallel", "arbitrary")),
        name="fox_prep",
    )(main, f_pre, bf, gq, gk, sh)


def _fox_kernel(bounded_ref, q_ref, k_ref, v_ref, op_ref, o_ref, acc_s, *, t):
    qi = pl.program_id(2)
    lane = lax.broadcasted_iota(jnp.int32, (1, LANES), 1)
    lo = lane < FX_DH
    sels = (lo, jnp.logical_not(lo))
    row = lax.broadcasted_iota(jnp.int32, (t, t), 0)
    col = lax.broadcasted_iota(jnp.int32, (t, t), 1)
    causal = col <= row
    qs = (q_ref[0, 0], q_ref[0, 1])
    bounded = bounded_ref[0] != 0

    def scores(hh, k0, width, masked):
        s = _dot_nt(qs[hh], k_ref[0, hh, pl.ds(k0, width), :])
        return jnp.where(causal, s, NEG) if masked else s

    def vext(hh, k0, width):
        return jnp.where(sels[hh], v_ref[0, pl.ds(k0, width), :], jnp.ones((), BF16))

    def plain_tile(k0, width, masked):
        for hh in range(2):
            p = jnp.exp2(scores(hh, k0, width, masked)).astype(BF16)
            acc_s[hh] += _dot(p, vext(hh, k0, width))

    def online_tile(kt, ms, masked):
        k0 = pl.multiple_of(kt * t, t)
        new_ms = []
        for hh in range(2):
            s = scores(hh, k0, t, masked)
            m_new = jnp.maximum(ms[hh], jnp.max(s, axis=-1, keepdims=True))
            alpha = jnp.exp2(ms[hh] - m_new)
            p = jnp.exp2(s - m_new).astype(BF16)
            acc_s[hh] = alpha * acc_s[hh] + _dot(p, vext(hh, k0, t))
            new_ms.append(m_new)
        return tuple(new_ms)

    acc_s[...] = jnp.zeros_like(acc_s)

    @pl.when(bounded)
    def _():
        def step(kt, carry):
            plain_tile(pl.multiple_of(kt * t, t), t, False)
            return carry

        lax.fori_loop(0, qi, step, 0)
        plain_tile(pl.multiple_of(qi * t, t), t, True)

    @pl.when(jnp.logical_not(bounded))
    def _():
        m0 = jnp.full((t, 1), NEG, F32)
        ms = lax.fori_loop(0, qi, lambda kt, ms: online_tile(kt, ms, False), (m0, m0))
        online_tile(qi, ms, True)

    acc0 = acc_s[0]
    acc1 = acc_s[1]
    out = jnp.where(lo, acc0 / acc0[:, FX_DH:FX_DH + 1], acc1 / acc1[:, 0:1])
    o_ref[0] = (out * _sigmoid(op_ref[0].astype(F32))).astype(o_ref.dtype)


def _fox_logit_bound(q_norm_g, k_norm_g):
    return (FX_DH ** 0.5 * LOG2E) * jnp.max(jnp.abs(q_norm_g)) * jnp.max(jnp.abs(k_norm_g))


def _fox_core(main, qa, ka, bounded, *, t=512):
    b, s, _ = main.shape
    hp = FX_HEADS // 2
    return pl.pallas_call(
        functools.partial(_fox_kernel, t=t),
        grid=(b, hp, s // t),
        in_specs=[
            pl.BlockSpec(memory_space=pltpu.SMEM),
            pl.BlockSpec((1, 2, t, LANES), lambda i, p, j: (i, p, j, 0)),
            pl.BlockSpec((1, 2, s, LANES), lambda i, p, j: (i, p, 0, 0)),
            pl.BlockSpec((1, s, LANES), lambda i, p, j: (i, 0, 2 * hp + p)),
            pl.BlockSpec((1, t, LANES), lambda i, p, j: (i, j, 3 * hp + p)),
        ],
        out_specs=pl.BlockSpec((1, t, LANES), lambda i, p, j: (i, j, p)),
        out_shape=jax.ShapeDtypeStruct((b, s, FX_HEADS * FX_DH), BF16),
        scratch_shapes=[pltpu.VMEM((2, t, LANES), F32)],
        compiler_params=_cparams(("parallel", "parallel", "arbitrary")),
        name="fox_core",
    )(bounded.astype(jnp.int32).reshape(1), qa, ka, main, main)


def _gla_kernel(main_ref, gl_ref, wgu_ref, bg_ref, ng_ref, o_ref, st_s, cum_s, k_s, v_s):
    L = CHUNK
    H = GLA_HEADS
    SUB = GLA_SUB
    nk = H * GLA_DK
    nv = H * GLA_DV
    scale = GLA_DK ** -0.5

    @pl.when(pl.program_id(1) == 0)
    def _():
        st_s[...] = jnp.zeros_like(st_s)

    row = lax.broadcasted_iota(jnp.int32, (L, L), 0)
    col = lax.broadcasted_iota(jnp.int32, (L, L), 1)
    tri = (col <= row).astype(BF16)
    eye = (lax.broadcasted_iota(jnp.int32, (GLA_DK, GLA_DK), 0)
           == lax.broadcasted_iota(jnp.int32, (GLA_DK, GLA_DK), 1))
    t_sub = lax.broadcasted_iota(jnp.int32, (SUB, 1), 0)

    g_hi, g_lo = _split_bf16(gl_ref[0])
    w_hi, w_lo = _split_bf16(wgu_ref[...])
    logits = _dot(g_hi, w_hi) + _dot(g_hi, w_lo) + _dot(g_lo, w_hi)
    la = _log_sigmoid(logits + bg_ref[...]) * (1.0 / GLA_TAU)
    cum_all = _tri_dot(tri, la)

    for h in range(H):
        ks = slice(h * GLA_DK, (h + 1) * GLA_DK)
        vs = slice(h * GLA_DV, (h + 1) * GLA_DV)
        cum = cum_all[:, ks]
        tot = cum[L - 1:L, :]
        q = main_ref[0, :, ks].astype(F32) * scale
        k = main_ref[0, :, nk + h * GLA_DK:nk + (h + 1) * GLA_DK].astype(F32)
        v_bf = main_ref[0, :, 2 * nk + h * GLA_DV:2 * nk + (h + 1) * GLA_DV]
        state = st_s[h]

        inter = _dot((q * jnp.exp(cum)).astype(BF16), state.astype(BF16))

        cum_s[...] = cum
        k_s[...] = k
        v_s[...] = v_bf.astype(F32)

        for i in range(L // SUB):
            rs = slice(i * SUB, (i + 1) * SUB)
            qb = q[rs]
            cb = cum[rs]
            hb = inter[rs]
            if i > 0:
                ref_pt = cum[i * SUB - 1:i * SUB, :]
                qs = (qb * jnp.exp(cb - ref_pt)).astype(BF16)
                kprev = (k[:i * SUB] * jnp.exp(ref_pt - cum[:i * SUB])).astype(BF16)
                a_off = _dot_nt(qs, kprev)
                hb = hb + _dot(a_off.astype(BF16), v_bf[:i * SUB])
            for j in range(SUB):
                rj = i * SUB + j
                kj = k_s[rj:rj + 1, :]
                cj = cum_s[rj:rj + 1, :]
                vj = v_s[rj:rj + 1, :]
                e = jnp.exp(jnp.where(t_sub >= j, cb - cj, NEG))
                p = jnp.sum(qb * kj * e, axis=-1, keepdims=True)
                hb = hb + p * vj
            hn = hb * lax.rsqrt(jnp.mean(hb * hb, axis=-1, keepdims=True) + EPS)
            r = main_ref[0, rs, 2 * nk + nv + h * GLA_DV:2 * nk + nv + (h + 1) * GLA_DV].astype(F32)
            out = hn * ng_ref[:, vs] * (r * _sigmoid(r))
            o_ref[0, rs, vs] = out.astype(o_ref.dtype)

        kd = (k * jnp.exp(tot - cum)).astype(BF16)
        tot_col = jnp.sum(jnp.where(eye, jnp.broadcast_to(tot, (GLA_DK, GLA_DK)), 0.0),
                          axis=-1, keepdims=True)
        st_s[h] = jnp.exp(tot_col) * state + _dot_tn(kd, v_bf)


def _gla_core(main, g_low, w_gate_up, b_gate, norm_g):
    b, s, _ = main.shape
    nc = s // CHUNK
    nk = GLA_HEADS * GLA_DK
    nv = GLA_HEADS * GLA_DV
    wgu = jnp.zeros((LANES, nk), F32).at[:GLA_RANK].set(w_gate_up)
    return pl.pallas_call(
        _gla_kernel,
        grid=(b, nc),
        in_specs=[
            pl.BlockSpec((1, CHUNK, main.shape[2]), lambda i, c: (i, c, 0)),
            pl.BlockSpec((1, CHUNK, LANES), lambda i, c: (i, c, 0)),
            pl.BlockSpec((LANES, nk), lambda i, c: (0, 0)),
            pl.BlockSpec((1, nk), lambda i, c: (0, 0)),
            pl.BlockSpec((1, nv), lambda i, c: (0, 0)),
        ],
        out_specs=pl.BlockSpec((1, CHUNK, nv), lambda i, c: (i, c, 0)),
        out_shape=jax.ShapeDtypeStruct((b, s, nv), BF16),
        scratch_shapes=[
            pltpu.VMEM((GLA_HEADS, GLA_DK, GLA_DV), F32),
            pltpu.VMEM((CHUNK, GLA_DK), F32),
            pltpu.VMEM((CHUNK, GLA_DK), F32),
            pltpu.VMEM((CHUNK, GLA_DV), F32),
        ],
        compiler_params=_cparams(("parallel", "arbitrary")),
        name="gla_core",
    )(main, g_low, wgu, b_gate.reshape(1, -1), norm_g.reshape(1, -1))


def _split_w_in(w_in, n_main):
    d = w_in.shape[0]
    n_gate = w_in.shape[1] - n_main
    w_gate = jnp.zeros((d, LANES), F32).at[:, :n_gate].set(w_in[:, n_main:])
    return w_in[:, :n_main].astype(BF16), w_gate.astype(BF16)


def kernel(x, mix_norm_g, ffn_norm_g, ffn_w_gate, ffn_w_up, ffn_w_down, ml_w_in, ml_b_if, ml_norm_g, ml_w_out, sb_w_in, sb_w_out, fx_w_in, fx_b_f, fx_q_norm_g, fx_k_norm_g, fx_w_out, gla_w_in, gla_w_gate_up, gla_b_gate, gla_norm_g, gla_w_out):
    b, s, d = x.shape
    depth = mix_norm_g.shape[0]
    h = x.reshape(b * s, d)
    for layer in range(depth):
        kind = layer % 4
        j = layer // 4
        g_mix = mix_norm_g[layer].reshape(1, d)
        if kind == 0:
            w_main, w_gate = _split_w_in(ml_w_in[j], ml_w_in.shape[2] - 2 * ML_HEADS)
            main, gates = _in_proj(h, g_mix, w_main, w_gate)
            a = _mlstm_core(main.reshape(b, s, -1), gates.reshape(b, s, LANES), ml_b_if[j], ml_norm_g[j])
            w_out = ml_w_out[j]
        elif kind == 1:
            w_main, w_gate = _split_w_in(sb_w_in[j], sb_w_in.shape[2])
            main, _ = _in_proj(h, g_mix, w_main, w_gate)
            a = _sb_core(main.reshape(b, s, -1))
            w_out = sb_w_out[j]
        elif kind == 2:
            w_main, w_gate = _split_w_in(fx_w_in[j], fx_w_in.shape[2] - FX_HEADS)
            main, gates = _in_proj(h, g_mix, w_main, w_gate)
            main = main.reshape(b, s, -1)
            bound = _fox_logit_bound(fx_q_norm_g[j], fx_k_norm_g[j])
            bounded = bound <= FX_MAX_SHIFT
            qa, ka = _fox_prep(main, gates.reshape(b, s, LANES), fx_b_f[j], fx_q_norm_g[j], fx_k_norm_g[j],
                               jnp.where(bounded, bound, 0.0))
            a = _fox_core(main, qa, ka, bounded)
            w_out = fx_w_out[j]
        else:
            w_main, w_gate = _split_w_in(gla_w_in[j], gla_w_in.shape[2] - GLA_RANK)
            main, gates = _in_proj(h, g_mix, w_main, w_gate)
            a = _gla_core(main.reshape(b, s, -1), gates.reshape(b, s, LANES), gla_w_gate_up[j], gla_b_gate[j], gla_norm_g[j])
            w_out = gla_w_out[j]
        h = _out_ffn(h, a.reshape(b * s, -1), w_out.astype(BF16), ffn_norm_g[layer].reshape(1, d),
                     ffn_w_gate[layer].astype(BF16), ffn_w_up[layer].astype(BF16), ffn_w_down[layer].astype(BF16))
    return h.reshape(b, s, d)
```

```python
import functools

import jax
import jax.numpy as jnp
import numpy as np
from jax import lax
from jax.experimental import pallas as pl
from jax.experimental.pallas import tpu as pltpu

F32 = jnp.float32
BF16 = jnp.bfloat16

D_MODEL = 1024
EPS = 1e-6
CHUNK = 64
ML_HEADS, ML_DQK, ML_DV = 8, 64, 128
SB_HEADS, SB_DH = 16, 64
FX_HEADS, FX_DH = 16, 64
GLA_HEADS, GLA_DK, GLA_DV = 4, 128, 256
GLA_RANK = 16
GLA_TAU = 16.0
GLA_SUB = 16
NEG = -1e30
LOG2E = 1.4426950408889634
FX_MAX_SHIFT = 60.0
SB_LOG2_ZERO = -150.0
LANES = 128
VMEM_LIMIT = 56 * 1024 * 1024


def _cparams(sem):
    return pltpu.CompilerParams(dimension_semantics=sem, vmem_limit_bytes=VMEM_LIMIT)


def _resident(shape):
    return pl.BlockSpec(shape, lambda i: (0,) * len(shape), pipeline_mode=pl.Buffered(1))


def _log_sigmoid(x):
    return jnp.minimum(x, 0.0) - jnp.log(1.0 + jnp.exp(-jnp.abs(x)))


def _softplus(x):
    return jnp.maximum(x, 0.0) + jnp.log(1.0 + jnp.exp(-jnp.abs(x)))


def _sigmoid(x):
    return 1.0 / (1.0 + jnp.exp(-x))


def _split_bf16(x):
    hi = x.astype(BF16)
    lo = (x - hi.astype(F32)).astype(BF16)
    return hi, lo


def _dot(a, b):
    return jnp.dot(a, b, preferred_element_type=F32)


def _dot_nt(a, b):
    return lax.dot_general(a, b, (((1,), (1,)), ((), ())), preferred_element_type=F32)


def _dot_tn(a, b):
    return lax.dot_general(a, b, (((0,), (0,)), ((), ())), preferred_element_type=F32)


def _tri_dot(tri, x):
    hi, lo = _split_bf16(x)
    return _dot(tri, hi) + _dot(tri, lo)


def _dot_tri(x, tri):
    hi, lo = _split_bf16(x)
    return _dot(hi, tri) + _dot(lo, tri)


def _in_proj_kernel(x_ref, g_ref, w_ref, wg_ref, main_ref, gate_ref, *, tn):
    x = x_ref[...]
    ms = jnp.mean(x * x, axis=-1, keepdims=True)
    u = (x * lax.rsqrt(ms + EPS) * g_ref[...]).astype(BF16)
    for c in range(main_ref.shape[-1] // tn):
        cs = slice(c * tn, (c + 1) * tn)
        main_ref[:, cs] = _dot(u, w_ref[:, cs]).astype(main_ref.dtype)
    gate_ref[...] = _dot(u, wg_ref[...])


def _in_proj(x, g, w_main, w_gate, *, tm=512, tn=512):
    n, d = x.shape
    wn = w_main.shape[1]
    return pl.pallas_call(
        functools.partial(_in_proj_kernel, tn=tn),
        grid=(n // tm,),
        in_specs=[
            pl.BlockSpec((tm, d), lambda i: (i, 0)),
            _resident((1, d)),
            _resident((d, wn)),
            _resident((d, LANES)),
        ],
        out_specs=[
            pl.BlockSpec((tm, wn), lambda i: (i, 0)),
            pl.BlockSpec((tm, LANES), lambda i: (i, 0)),
        ],
        out_shape=[
            jax.ShapeDtypeStruct((n, wn), BF16),
            jax.ShapeDtypeStruct((n, LANES), F32),
        ],
        compiler_params=_cparams(("parallel",)),
        name="in_proj",
    )(x, g, w_main, w_gate)


def _out_ffn_kernel(h_ref, a_ref, wo_ref, g_ref, wg_ref, wu_ref, wd_ref, o_ref, *, tf):
    h1 = h_ref[...] + _dot(a_ref[...], wo_ref[...])
    ms = jnp.mean(h1 * h1, axis=-1, keepdims=True)
    u = (h1 * lax.rsqrt(ms + EPS) * g_ref[...]).astype(BF16)
    acc = h1
    for c in range(wg_ref.shape[1] // tf):
        cs = slice(c * tf, (c + 1) * tf)
        gt = _dot(u, wg_ref[:, cs])
        up = _dot(u, wu_ref[:, cs])
        hid = (gt * _sigmoid(gt) * up).astype(BF16)
        acc = acc + _dot(hid, wd_ref[cs, :])
    o_ref[...] = acc


def _out_ffn(h, a, w_out, g, w_gate, w_up, w_down, *, tm=512, tf=256):
    n, d = h.shape
    dff = w_gate.shape[1]
    return pl.pallas_call(
        functools.partial(_out_ffn_kernel, tf=tf),
        grid=(n // tm,),
        in_specs=[
            pl.BlockSpec((tm, d), lambda i: (i, 0)),
            pl.BlockSpec((tm, d), lambda i: (i, 0)),
            _resident((d, d)),
            _resident((1, d)),
            _resident((d, dff)),
            _resident((d, dff)),
            _resident((dff, d)),
        ],
        out_specs=pl.BlockSpec((tm, d), lambda i: (i, 0)),
        out_shape=jax.ShapeDtypeStruct((n, d), F32),
        compiler_params=_cparams(("parallel",)),
        name="out_ffn",
    )(h, a, w_out, g, w_gate, w_up, w_down)


def _mlstm_kernel(main_ref, gc_ref, gr_ref, bc_ref, br_ref, ng_ref, o_ref, c_s, m_s, *, nb):
    L = CHUNK
    H = ML_HEADS

    @pl.when(pl.program_id(1) == 0)
    def _():
        c_s[...] = jnp.zeros_like(c_s)
        m_s[...] = jnp.zeros_like(m_s)

    for bb in range(nb):
        _mlstm_chunk(bb, main_ref, gc_ref, gr_ref, bc_ref, br_ref, ng_ref, o_ref, c_s, m_s)


def _mlstm_chunk(bb, main_ref, gc_ref, gr_ref, bc_ref, br_ref, ng_ref, o_ref, c_s, m_s):
    L = CHUNK
    H = ML_HEADS
    scale = ML_DQK ** -0.5
    row = lax.broadcasted_iota(jnp.int32, (L, L), 0)
    col = lax.broadcasted_iota(jnp.int32, (L, L), 1)
    causal = col <= row
    tri = causal.astype(BF16)
    tri_t = (row <= col).astype(BF16)

    gc = gc_ref[bb] + bc_ref[...]
    gr = gr_ref[bb, 0] + br_ref[...]
    cum_c = _tri_dot(tri, _log_sigmoid(gc))
    li_r = gr[0:H]
    cum_r = _dot_tri(_log_sigmoid(gr[H:2 * H]), tri_t)

    def lane_replicate(x, base):
        gate = lax.broadcasted_iota(jnp.int32, (LANES, H * LANES), 0)
        blk = lax.broadcasted_iota(jnp.int32, (LANES, H * LANES), 1) >> 7
        return _dot_tri(x, (gate == blk + base).astype(BF16))

    cum_b_all = lane_replicate(cum_c, H)
    li_b_all = lane_replicate(gc, 0)
    ones_blk = jnp.ones((L, ML_DV), BF16)
    nq = H * ML_DQK
    nv = H * ML_DV

    qs = [main_ref[bb, :, h * ML_DQK:(h + 1) * ML_DQK] for h in range(H)]
    ks = [main_ref[bb, :, nq + h * ML_DQK:nq + (h + 1) * ML_DQK] for h in range(H)]
    s_qks = [_dot_nt(qs[h], ks[h]) for h in range(H)]
    inters = [_dot(qs[h], c_s[bb * H + h].astype(BF16)) for h in range(H)]

    for h in range(H):
        k = ks[h]
        v = main_ref[bb, :, 2 * nq + h * ML_DV:2 * nq + (h + 1) * ML_DV]
        op = main_ref[bb, :, 2 * nq + nv + h * ML_DV:2 * nq + nv + (h + 1) * ML_DV]
        vext = jnp.concatenate([v, ones_blk], axis=1)
        cum_b = cum_b_all[:, h * LANES:(h + 1) * LANES]
        li_b = li_b_all[:, h * LANES:(h + 1) * LANES]
        cum_r1 = cum_r[h:h + 1, :]
        li_r1 = li_r[h:h + 1, :]
        m_st = m_s[bb * H + h][:, 0:1]
        cext = c_s[bb * H + h]

        log_d = jnp.where(causal, cum_b[:, :L] - cum_r1 + li_r1, NEG)
        m_t = jnp.maximum(cum_b + m_st, jnp.max(log_d, axis=-1, keepdims=True))
        dmat = jnp.exp(log_d - m_t[:, :L])
        w_inter = jnp.exp(cum_b + m_st - m_t)
        sc = (s_qks[h] * (dmat * scale)).astype(BF16)
        inter = inters[h]
        intra = _dot(sc, vext)
        num = w_inter * inter[:, :ML_DV] + intra[:, :ML_DV]
        den = w_inter * inter[:, ML_DV:] + intra[:, ML_DV:]
        hh = num / jnp.maximum(jnp.abs(den), jnp.exp(-m_t))
        hn = hh * lax.rsqrt(jnp.mean(hh * hh, axis=-1, keepdims=True) + EPS)
        out = hn * ng_ref[:, h * ML_DV:(h + 1) * ML_DV] * _sigmoid(op.astype(F32))
        o_ref[bb, :, h * ML_DV:(h + 1) * ML_DV] = out.astype(o_ref.dtype)

        tot_f = cum_r1[:, L - 1:L]
        log_w_r = tot_f - cum_r1 + li_r1
        m_new = jnp.maximum(tot_f + m_st, jnp.max(log_w_r, axis=-1, keepdims=True))
        decay = jnp.exp(tot_f + m_st - m_new)
        w_b = jnp.exp(tot_f - cum_b + li_b - m_new) * scale
        wk = (k.astype(F32) * w_b[:, :ML_DQK]).astype(BF16)
        c_s[bb * H + h] = decay * cext + _dot_tn(wk, vext)
        m_s[bb * H + h] = jnp.broadcast_to(m_new, (1, LANES))


def _mlstm_core(main, gates, b_if, norm_g, *, nb=2):
    b, s, _ = main.shape
    nc = s // CHUNK
    h = ML_HEADS
    gr = gates[:, :, :2 * h].reshape(b, nc, CHUNK, 2 * h).transpose(0, 1, 3, 2)
    bc = jnp.zeros((1, LANES), F32).at[0, :2 * h].set(b_if)
    br = b_if.reshape(2 * h, 1)
    return pl.pallas_call(
        functools.partial(_mlstm_kernel, nb=nb),
        grid=(b // nb, nc),
        in_specs=[
            pl.BlockSpec((nb, CHUNK, main.shape[2]), lambda i, c: (i, c, 0)),
            pl.BlockSpec((nb, CHUNK, LANES), lambda i, c: (i, c, 0)),
            pl.BlockSpec((nb, 1, 2 * h, CHUNK), lambda i, c: (i, c, 0, 0)),
            pl.BlockSpec((1, LANES), lambda i, c: (0, 0)),
            pl.BlockSpec((2 * h, 1), lambda i, c: (0, 0)),
            pl.BlockSpec((1, h * ML_DV), lambda i, c: (0, 0)),
        ],
        out_specs=pl.BlockSpec((nb, CHUNK, h * ML_DV), lambda i, c: (i, c, 0)),
        out_shape=jax.ShapeDtypeStruct((b, s, h * ML_DV), BF16),
        scratch_shapes=[
            pltpu.VMEM((nb * h, ML_DQK, 2 * ML_DV), F32),
            pltpu.VMEM((nb * h, 1, LANES), F32),
        ],
        compiler_params=_cparams(("parallel", "arbitrary")),
        name="mlstm_core",
    )(main, gates, gr, bc, br, norm_g.reshape(1, -1))


def _sb_kernel(q_ref, k_ref, v_ref, o_ref, acc_s, *, t, heads):
    qi = pl.program_id(2)
    lane = lax.broadcasted_iota(jnp.int32, (1, LANES), 1)
    lo = lane < SB_DH
    sels = (lo, jnp.logical_not(lo))
    row = lax.broadcasted_iota(jnp.int32, (t, t), 0)
    col = lax.broadcasted_iota(jnp.int32, (t, t), 1)
    valid = col < row
    lower = (row > col).astype(BF16)
    groups = [slice(g * LANES, (g + 1) * LANES) for g in range(heads // 2)]
    zero = jnp.zeros((), BF16)
    qms = [jnp.where(sels[h % 2], q_ref[0, :, groups[h // 2]], zero) for h in range(heads)]

    def tile(kt, carries, masked):
        k0 = pl.multiple_of(kt * t, t)
        new_carries = []
        for g, lanes in enumerate(groups):
            kk = k_ref[0, pl.ds(k0, t), lanes]
            vv = v_ref[0, pl.ds(k0, t), lanes]
            contrib = None
            for h in (2 * g, 2 * g + 1):
                z = _dot_nt(qms[h], kk)
                nlp = jnp.maximum(z, 0.0) + jnp.log2(1.0 + jnp.exp2(-jnp.abs(z)))
                if masked:
                    nlp = jnp.where(valid, nlp, 0.0)
                later = _dot_tri(nlp, lower)
                a = jnp.exp2(z - nlp - later + carries[h])
                if masked:
                    a = jnp.where(valid, a, 0.0)
                c = _dot(a.astype(BF16), jnp.where(sels[h % 2], vv, zero))
                contrib = c if contrib is None else contrib + c
                new_carries.append(carries[h] - jnp.sum(nlp, axis=-1, keepdims=True))
            acc_s[:, lanes] += contrib
        return tuple(new_carries)

    acc_s[...] = jnp.zeros_like(acc_s)
    carries = tile(qi, (jnp.zeros((t, 1), F32),) * heads, True)

    def live(carries):
        return functools.reduce(jnp.maximum, [jnp.max(c) for c in carries])

    def cond(st):
        return jnp.logical_and(st[0] >= 0, st[1] > SB_LOG2_ZERO)

    def body(st):
        carries = tile(st[0], st[2], False)
        return st[0] - 1, live(carries), carries

    lax.while_loop(cond, body, (qi - 1, live(carries), carries))
    o_ref[0] = acc_s[...].astype(o_ref.dtype)


def _sb_core(main, *, t=256, heads=4):
    b, s, _ = main.shape
    ng = SB_HEADS // heads
    w = heads * SB_DH
    return pl.pallas_call(
        functools.partial(_sb_kernel, t=t, heads=heads),
        grid=(b, ng, s // t),
        in_specs=[
            pl.BlockSpec((1, t, w), lambda i, p, j: (i, j, p)),
            pl.BlockSpec((1, s, w), lambda i, p, j: (i, 0, ng + p)),
            pl.BlockSpec((1, s, w), lambda i, p, j: (i, 0, 2 * ng + p)),
        ],
        out_specs=pl.BlockSpec((1, t, w), lambda i, p, j: (i, j, p)),
        out_shape=jax.ShapeDtypeStruct((b, s, SB_HEADS * SB_DH), BF16),
        scratch_shapes=[pltpu.VMEM((t, w), F32)],
        compiler_params=_cparams(("parallel", "parallel", "arbitrary")),
        name="sb_core",
    )(main, main, main)


def _fox_scatter_matrices():
    pq = np.zeros((FX_HEADS // 2, LANES, 2 * LANES), np.float32)
    pk = np.zeros((FX_HEADS // 2, LANES, 2 * LANES), np.float32)
    for p in range(FX_HEADS // 2):
        for hh, base in ((0, FX_DH), (1, LANES)):
            head = 2 * p + hh
            for i in range(3):
                pq[p, 16 * i + head, base + i] = 1.0
                pq[p, 48, base + 3 + i] = 1.0
                pk[p, 48, base + i] = 1.0
                pk[p, 16 * i + head, base + 3 + i] = -1.0
    return jnp.asarray(pq, BF16), jnp.asarray(pk, BF16)


def _fox_prep_kernel(main_ref, fp_ref, bf_ref, gq_ref, gk_ref, sh_ref, pq_ref, pk_ref, qa_ref, ka_ref,
                     carry_s, *, ts):
    @pl.when(pl.program_id(1) == 0)
    def _():
        carry_s[...] = jnp.zeros_like(carry_s)

    row = lax.broadcasted_iota(jnp.int32, (ts, ts), 0)
    col = lax.broadcasted_iota(jnp.int32, (ts, ts), 1)
    tri = (col <= row).astype(BF16)
    lf = _log_sigmoid(fp_ref[0] + bf_ref[...])
    cum = _tri_dot(tri, lf) + carry_s[...]
    carry_s[...] = cum[ts - 1:ts, :]

    lane = lax.broadcasted_iota(jnp.int32, (1, LANES), 1)
    lo = lane < FX_DH
    head_lane = lane < FX_HEADS

    def packed(c):
        c = jnp.where(head_lane, c, 0.0)
        hi = c.astype(BF16).astype(F32)
        mid = (c - hi).astype(BF16).astype(F32)
        low = (c - hi - mid).astype(BF16).astype(F32)
        out = hi + pltpu.roll(mid, 16, axis=1) + pltpu.roll(low, 32, axis=1)
        return jnp.where(lane == 48, 1.0, out).astype(BF16)

    pack_q = packed(cum * LOG2E - sh_ref[...])
    pack_k = packed(cum * LOG2E)

    n_q = FX_HEADS * FX_DH // LANES
    for c in range(2 * n_q):
        is_q = c < n_q
        p = c % n_q
        xc = main_ref[0, :, c * LANES:(c + 1) * LANES].astype(F32)
        x2 = xc * xc
        s_lo = jnp.sum(jnp.where(lo, x2, 0.0), axis=-1, keepdims=True)
        s_hi = jnp.sum(jnp.where(lo, 0.0, x2), axis=-1, keepdims=True)
        r = jnp.where(lo, lax.rsqrt(s_lo / FX_DH + EPS), lax.rsqrt(s_hi / FX_DH + EPS))
        xn = xc * r * (gq_ref[...] if is_q else gk_ref[...])
        aug = _dot(pack_q, pq_ref[p]) if is_q else _dot(pack_k, pk_ref[p])
        out_ref = qa_ref if is_q else ka_ref
        out_ref[0, 2 * p] = jnp.where(lo, xn, aug[:, :LANES]).astype(out_ref.dtype)
        out_ref[0, 2 * p + 1] = jnp.where(lo, aug[:, LANES:], xn).astype(out_ref.dtype)


def _fox_prep(main, f_pre, b_f, q_norm_g, k_norm_g, shift, *, ts=256):
    b, s, _ = main.shape
    w = 2 * FX_HEADS * FX_DH
    bf = jnp.zeros((1, LANES), F32).at[0, :FX_HEADS].set(b_f)
    sh = jnp.full((1, LANES), shift, F32)
    gq = jnp.tile(q_norm_g, 2).reshape(1, LANES) * (FX_DH ** -0.5 * LOG2E)
    gk = jnp.tile(k_norm_g, 2).reshape(1, LANES)
    pq, pk = _fox_scatter_matrices()
    head_rows = pl.BlockSpec((1, FX_HEADS, ts, LANES), lambda i, j: (i, 0, j, 0))
    const_row = pl.BlockSpec((1, LANES), lambda i, j: (0, 0))
    scatter = pl.BlockSpec(pq.shape, lambda i, j: (0, 0, 0))
    return pl.pallas_call(
        functools.partial(_fox_prep_kernel, ts=ts),
        grid=(b, s // ts),
        in_specs=[
            pl.BlockSpec((1, ts, w), lambda i, j: (i, j, 0)),
            pl.BlockSpec((1, ts, LANES), lambda i, j: (i, j, 0)),
            const_row, const_row, const_row, const_row, scatter, scatter,
        ],
        out_specs=[head_rows, head_rows],
        out_shape=[jax.ShapeDtypeStruct((b, FX_HEADS, s, LANES), BF16)] * 2,
        scratch_shapes=[pltpu.VMEM((1, LANES), F32)],
        compiler_params=_cparams(("parallel", "arbitrary")),
        name="fox_prep",
    )(main, f_pre, bf, gq, gk, sh, pq, pk)


def _fox_kernel(bounded_ref, q_ref, k_ref, v_ref, op_ref, o_ref, acc_s, *, t):
    qi = pl.program_id(2)
    lane = lax.broadcasted_iota(jnp.int32, (1, LANES), 1)
    lo = lane < FX_DH
    sels = (lo, jnp.logical_not(lo))
    row = lax.broadcasted_iota(jnp.int32, (t, t), 0)
    col = lax.broadcasted_iota(jnp.int32, (t, t), 1)
    causal = col <= row
    qs = (q_ref[0, 0], q_ref[0, 1])
    bounded = bounded_ref[0] != 0

    def scores(hh, k0, width, masked):
        s = _dot_nt(qs[hh], k_ref[0, hh, pl.ds(k0, width), :])
        return jnp.where(causal, s, NEG) if masked else s

    def vext(hh, k0, width):
        return jnp.where(sels[hh], v_ref[0, pl.ds(k0, width), :], jnp.ones((), BF16))

    def plain_tile(k0, width, masked):
        for hh in range(2):
            p = jnp.exp2(scores(hh, k0, width, masked)).astype(BF16)
            acc_s[hh] += _dot(p, vext(hh, k0, width))

    def online_tile(kt, ms, masked):
        k0 = pl.multiple_of(kt * t, t)
        new_ms = []
        for hh in range(2):
            s = scores(hh, k0, t, masked)
            m_new = jnp.maximum(ms[hh], jnp.max(s, axis=-1, keepdims=True))
            alpha = jnp.exp2(ms[hh] - m_new)
            p = jnp.exp2(s - m_new).astype(BF16)
            acc_s[hh] = alpha * acc_s[hh] + _dot(p, vext(hh, k0, t))
            new_ms.append(m_new)
        return tuple(new_ms)

    acc_s[...] = jnp.zeros_like(acc_s)

    @pl.when(bounded)
    def _():
        def step(kt, carry):
            plain_tile(pl.multiple_of(kt * t, t), t, False)
            return carry

        lax.fori_loop(0, qi, step, 0)
        plain_tile(pl.multiple_of(qi * t, t), t, True)

    @pl.when(jnp.logical_not(bounded))
    def _():
        m0 = jnp.full((t, 1), NEG, F32)
        ms = lax.fori_loop(0, qi, lambda kt, ms: online_tile(kt, ms, False), (m0, m0))
        online_tile(qi, ms, True)

    acc0 = acc_s[0]
    acc1 = acc_s[1]
    out = jnp.where(lo, acc0 / acc0[:, FX_DH:FX_DH + 1], acc1 / acc1[:, 0:1])
    o_ref[0] = (out * _sigmoid(op_ref[0].astype(F32))).astype(o_ref.dtype)


def _fox_logit_bound(q_norm_g, k_norm_g):
    return (FX_DH ** 0.5 * LOG2E) * jnp.max(jnp.abs(q_norm_g)) * jnp.max(jnp.abs(k_norm_g))


def _fox_core(main, qa, ka, bounded, *, t=512):
    b, s, _ = main.shape
    hp = FX_HEADS // 2
    return pl.pallas_call(
        functools.partial(_fox_kernel, t=t),
        grid=(b, hp, s // t),
        in_specs=[
            pl.BlockSpec(memory_space=pltpu.SMEM),
            pl.BlockSpec((1, 2, t, LANES), lambda i, p, j: (i, p, j, 0)),
            pl.BlockSpec((1, 2, s, LANES), lambda i, p, j: (i, p, 0, 0)),
            pl.BlockSpec((1, s, LANES), lambda i, p, j: (i, 0, 2 * hp + p)),
            pl.BlockSpec((1, t, LANES), lambda i, p, j: (i, j, 3 * hp + p)),
        ],
        out_specs=pl.BlockSpec((1, t, LANES), lambda i, p, j: (i, j, p)),
        out_shape=jax.ShapeDtypeStruct((b, s, FX_HEADS * FX_DH), BF16),
        scratch_shapes=[pltpu.VMEM((2, t, LANES), F32)],
        compiler_params=_cparams(("parallel", "parallel", "arbitrary")),
        name="fox_core",
    )(bounded.astype(jnp.int32).reshape(1), qa, ka, main, main)


def _gla_kernel(main_ref, gl_ref, wgu_ref, bg_ref, ng_ref, o_ref, st_s, cum_s, k_s, v_s):
    L = CHUNK
    H = GLA_HEADS
    SUB = GLA_SUB
    nk = H * GLA_DK
    nv = H * GLA_DV
    scale = GLA_DK ** -0.5

    @pl.when(pl.program_id(1) == 0)
    def _():
        st_s[...] = jnp.zeros_like(st_s)

    row = lax.broadcasted_iota(jnp.int32, (L, L), 0)
    col = lax.broadcasted_iota(jnp.int32, (L, L), 1)
    tri = (col <= row).astype(BF16)
    eye = (lax.broadcasted_iota(jnp.int32, (GLA_DK, GLA_DK), 0)
           == lax.broadcasted_iota(jnp.int32, (GLA_DK, GLA_DK), 1))

    g_hi, g_lo = _split_bf16(gl_ref[0])
    w_hi, w_lo = _split_bf16(wgu_ref[...])
    logits = _dot(g_hi, w_hi) + _dot(g_hi, w_lo) + _dot(g_lo, w_hi)
    la = _log_sigmoid(logits + bg_ref[...]) * (1.0 / GLA_TAU)
    cum_all = _tri_dot(tri, la)

    heads = range(H)
    nsub = L // SUB
    half = SUB // 2
    qs, v_bfs, cums, decays = [], [], [], []

    inters, updates, a_offs = [], [], []
    for h in heads:
        ks = slice(h * GLA_DK, (h + 1) * GLA_DK)
        cum = cum_all[:, ks]
        tot = cum[L - 1:L, :]
        q = main_ref[0, :, ks].astype(F32) * scale
        k = main_ref[0, :, nk + h * GLA_DK:nk + (h + 1) * GLA_DK].astype(F32)
        v_bf = main_ref[0, :, 2 * nk + h * GLA_DV:2 * nk + (h + 1) * GLA_DV]
        inters.append(_dot((q * jnp.exp(cum)).astype(BF16), st_s[h].astype(BF16)))
        updates.append(_dot_tn((k * jnp.exp(tot - cum)).astype(BF16), v_bf))
        offs = []
        for i in range(1, nsub):
            rs = slice(i * SUB, (i + 1) * SUB)
            ref_pt = cum[i * SUB - 1:i * SUB, :]
            q_rel = (q[rs] * jnp.exp(cum[rs] - ref_pt)).astype(BF16)
            k_rel = (k[:i * SUB] * jnp.exp(ref_pt - cum[:i * SUB])).astype(BF16)
            offs.append(_dot_nt(q_rel, k_rel))
        a_offs.append(offs)
        tot_col = jnp.sum(jnp.where(eye, jnp.broadcast_to(tot, (GLA_DK, GLA_DK)), 0.0),
                          axis=-1, keepdims=True)
        decays.append(jnp.exp(tot_col))
        cum_s[h] = cum
        k_s[h] = k
        v_s[h] = v_bf.astype(F32)
        qs.append(q)
        v_bfs.append(v_bf)
        cums.append(cum)

    h_offs = [[_dot(a_offs[h][i - 1].astype(BF16), v_bfs[h][:i * SUB]) for i in range(1, nsub)] for h in heads]

    t_half = lax.broadcasted_iota(jnp.int32, (half, 1), 0)
    for h in heads:
        vs = slice(h * GLA_DV, (h + 1) * GLA_DV)
        for i in range(nsub):
            rs = slice(i * SUB, (i + 1) * SUB)
            q_top, q_bot = qs[h][i * SUB:i * SUB + half], qs[h][i * SUB + half:(i + 1) * SUB]
            c_top, c_bot = cums[h][i * SUB:i * SUB + half], cums[h][i * SUB + half:(i + 1) * SUB]
            acc_top = jnp.zeros((half, GLA_DV), F32)
            acc_bot = jnp.zeros((half, GLA_DV), F32)
            for j in range(SUB):
                rj = i * SUB + j
                kj = k_s[h, rj:rj + 1, :]
                cj = cum_s[h, rj:rj + 1, :]
                vj = v_s[h, rj:rj + 1, :]
                if j < half:
                    e_top = jnp.exp(jnp.where(t_half >= j, c_top - cj, NEG))
                    acc_top = acc_top + jnp.sum(q_top * kj * e_top, axis=-1, keepdims=True) * vj
                    e_bot = jnp.exp(c_bot - cj)
                else:
                    e_bot = jnp.exp(jnp.where(t_half >= j - half, c_bot - cj, NEG))
                acc_bot = acc_bot + jnp.sum(q_bot * kj * e_bot, axis=-1, keepdims=True) * vj
            hb = inters[h][rs] + jnp.concatenate([acc_top, acc_bot], axis=0)
            if i > 0:
                hb = hb + h_offs[h][i - 1]
            hn = hb * lax.rsqrt(jnp.mean(hb * hb, axis=-1, keepdims=True) + EPS)
            r = main_ref[0, rs, 2 * nk + nv + h * GLA_DV:2 * nk + nv + (h + 1) * GLA_DV].astype(F32)
            out = hn * ng_ref[:, vs] * (r * _sigmoid(r))
            o_ref[0, rs, vs] = out.astype(o_ref.dtype)

    for h in heads:
        st_s[h] = decays[h] * st_s[h] + updates[h]


def _gla_core(main, g_low, w_gate_up, b_gate, norm_g):
    b, s, _ = main.shape
    nc = s // CHUNK
    nk = GLA_HEADS * GLA_DK
    nv = GLA_HEADS * GLA_DV
    wgu = jnp.zeros((LANES, nk), F32).at[:GLA_RANK].set(w_gate_up)
    return pl.pallas_call(
        _gla_kernel,
        grid=(b, nc),
        in_specs=[
            pl.BlockSpec((1, CHUNK, main.shape[2]), lambda i, c: (i, c, 0)),
            pl.BlockSpec((1, CHUNK, LANES), lambda i, c: (i, c, 0)),
            pl.BlockSpec((LANES, nk), lambda i, c: (0, 0)),
            pl.BlockSpec((1, nk), lambda i, c: (0, 0)),
            pl.BlockSpec((1, nv), lambda i, c: (0, 0)),
        ],
        out_specs=pl.BlockSpec((1, CHUNK, nv), lambda i, c: (i, c, 0)),
        out_shape=jax.ShapeDtypeStruct((b, s, nv), BF16),
        scratch_shapes=[
            pltpu.VMEM((GLA_HEADS, GLA_DK, GLA_DV), F32),
            pltpu.VMEM((GLA_HEADS, CHUNK, GLA_DK), F32),
            pltpu.VMEM((GLA_HEADS, CHUNK, GLA_DK), F32),
            pltpu.VMEM((GLA_HEADS, CHUNK, GLA_DV), F32),
        ],
        compiler_params=_cparams(("parallel", "arbitrary")),
        name="gla_core",
    )(main, g_low, wgu, b_gate.reshape(1, -1), norm_g.reshape(1, -1))


def _split_w_in(w_in, n_main):
    d = w_in.shape[0]
    n_gate = w_in.shape[1] - n_main
    w_gate = jnp.zeros((d, LANES), F32).at[:, :n_gate].set(w_in[:, n_main:])
    return w_in[:, :n_main].astype(BF16), w_gate.astype(BF16)


def kernel(x, mix_norm_g, ffn_norm_g, ffn_w_gate, ffn_w_up, ffn_w_down, ml_w_in, ml_b_if, ml_norm_g, ml_w_out, sb_w_in, sb_w_out, fx_w_in, fx_b_f, fx_q_norm_g, fx_k_norm_g, fx_w_out, gla_w_in, gla_w_gate_up, gla_b_gate, gla_norm_g, gla_w_out):
    b, s, d = x.shape
    depth = mix_norm_g.shape[0]
    h = x.reshape(b * s, d)
    for layer in range(depth):
        kind = layer % 4
        j = layer // 4
        g_mix = mix_norm_g[layer].reshape(1, d)
        if kind == 0:
            w_main, w_gate = _split_w_in(ml_w_in[j], ml_w_in.shape[2] - 2 * ML_HEADS)
            main, gates = _in_proj(h, g_mix, w_main, w_gate)
            a = _mlstm_core(main.reshape(b, s, -1), gates.reshape(b, s, LANES), ml_b_if[j], ml_norm_g[j])
            w_out = ml_w_out[j]
        elif kind == 1:
            nq = SB_HEADS * SB_DH
            w_sb = jnp.concatenate([sb_w_in[j][:, :nq] * (SB_DH ** -0.5 * LOG2E), sb_w_in[j][:, nq:]], axis=1)
            w_main, w_gate = _split_w_in(w_sb, w_sb.shape[1])
            main, _ = _in_proj(h, g_mix, w_main, w_gate)
            a = _sb_core(main.reshape(b, s, -1))
            w_out = sb_w_out[j]
        elif kind == 2:
            w_main, w_gate = _split_w_in(fx_w_in[j], fx_w_in.shape[2] - FX_HEADS)
            main, gates = _in_proj(h, g_mix, w_main, w_gate)
            main = main.reshape(b, s, -1)
            bound = _fox_logit_bound(fx_q_norm_g[j], fx_k_norm_g[j])
            bounded = bound <= FX_MAX_SHIFT
            qa, ka = _fox_prep(main, gates.reshape(b, s, LANES), fx_b_f[j], fx_q_norm_g[j], fx_k_norm_g[j],
                               jnp.where(bounded, bound, 0.0))
            a = _fox_core(main, qa, ka, bounded)
            w_out = fx_w_out[j]
        else:
            w_main, w_gate = _split_w_in(gla_w_in[j], gla_w_in.shape[2] - GLA_RANK)
            main, gates = _in_proj(h, g_mix, w_main, w_gate)
            a = _gla_core(main.reshape(b, s, -1), gates.reshape(b, s, LANES), gla_w_gate_up[j], gla_b_gate[j], gla_norm_g[j])
            w_out = gla_w_out[j]
        h = _out_ffn(h, a.reshape(b * s, -1), w_out.astype(BF16), ffn_norm_g[layer].reshape(1, d),
                     ffn_w_gate[layer].astype(BF16), ffn_w_up[layer].astype(BF16), ffn_w_down[layer].astype(BF16))
    return h.reshape(b, s, d)
```

```python
import functools

import jax
import jax.numpy as jnp
import numpy as np
from jax import lax
from jax.experimental import pallas as pl
from jax.experimental.pallas import tpu as pltpu

F32 = jnp.float32
BF16 = jnp.bfloat16

D_MODEL = 1024
EPS = 1e-6
CHUNK = 64
ML_HEADS, ML_DQK, ML_DV = 8, 64, 128
SB_HEADS, SB_DH = 16, 64
FX_HEADS, FX_DH = 16, 64
GLA_HEADS, GLA_DK, GLA_DV = 4, 128, 256
GLA_RANK = 16
GLA_TAU = 16.0
GLA_SUB = 16
NEG = -1e30
LOG2E = 1.4426950408889634
FX_MAX_SHIFT = 60.0
SB_LOG2_ZERO = -150.0
SB_Z_CLAMP = 100.0
LANES = 128
VMEM_LIMIT = 56 * 1024 * 1024


def _cparams(sem):
    return pltpu.CompilerParams(dimension_semantics=sem, vmem_limit_bytes=VMEM_LIMIT)


def _resident(shape):
    return pl.BlockSpec(shape, lambda i: (0,) * len(shape), pipeline_mode=pl.Buffered(1))


def _log_sigmoid(x):
    return jnp.minimum(x, 0.0) - jnp.log(1.0 + jnp.exp(-jnp.abs(x)))


def _softplus(x):
    return jnp.maximum(x, 0.0) + jnp.log(1.0 + jnp.exp(-jnp.abs(x)))


def _sigmoid(x):
    return 1.0 / (1.0 + jnp.exp(-x))


def _split_bf16(x):
    hi = x.astype(BF16)
    lo = (x - hi.astype(F32)).astype(BF16)
    return hi, lo


def _dot(a, b):
    return jnp.dot(a, b, preferred_element_type=F32)


def _dot_nt(a, b):
    return lax.dot_general(a, b, (((1,), (1,)), ((), ())), preferred_element_type=F32)


def _dot_tn(a, b):
    return lax.dot_general(a, b, (((0,), (0,)), ((), ())), preferred_element_type=F32)


def _tri_dot(tri, x):
    hi, lo = _split_bf16(x)
    return _dot(tri, hi) + _dot(tri, lo)


def _dot_tri(x, tri):
    hi, lo = _split_bf16(x)
    return _dot(hi, tri) + _dot(lo, tri)


def _in_proj_kernel(x_ref, g_ref, w_ref, wg_ref, main_ref, gate_ref, *, tn):
    x = x_ref[...]
    ms = jnp.mean(x * x, axis=-1, keepdims=True)
    u = (x * lax.rsqrt(ms + EPS) * g_ref[...]).astype(BF16)
    for c in range(main_ref.shape[-1] // tn):
        cs = slice(c * tn, (c + 1) * tn)
        main_ref[:, cs] = _dot(u, w_ref[:, cs]).astype(main_ref.dtype)
    gate_ref[...] = _dot(u, wg_ref[...])


def _in_proj(x, g, w_main, w_gate, *, tm=512, tn=512):
    n, d = x.shape
    wn = w_main.shape[1]
    return pl.pallas_call(
        functools.partial(_in_proj_kernel, tn=tn),
        grid=(n // tm,),
        in_specs=[
            pl.BlockSpec((tm, d), lambda i: (i, 0)),
            _resident((1, d)),
            _resident((d, wn)),
            _resident((d, LANES)),
        ],
        out_specs=[
            pl.BlockSpec((tm, wn), lambda i: (i, 0)),
            pl.BlockSpec((tm, LANES), lambda i: (i, 0)),
        ],
        out_shape=[
            jax.ShapeDtypeStruct((n, wn), BF16),
            jax.ShapeDtypeStruct((n, LANES), F32),
        ],
        compiler_params=_cparams(("parallel",)),
        name="in_proj",
    )(x, g, w_main, w_gate)


def _out_ffn_kernel(h_ref, a_ref, wo_ref, g_ref, wg_ref, wu_ref, wd_ref, o_ref, *, tf):
    h1 = h_ref[...] + _dot(a_ref[...], wo_ref[...])
    ms = jnp.mean(h1 * h1, axis=-1, keepdims=True)
    u = (h1 * lax.rsqrt(ms + EPS) * g_ref[...]).astype(BF16)
    acc = h1
    for c in range(wg_ref.shape[1] // tf):
        cs = slice(c * tf, (c + 1) * tf)
        gt = _dot(u, wg_ref[:, cs])
        up = _dot(u, wu_ref[:, cs])
        hid = (gt * _sigmoid(gt) * up).astype(BF16)
        acc = acc + _dot(hid, wd_ref[cs, :])
    o_ref[...] = acc


def _out_ffn(h, a, w_out, g, w_gate, w_up, w_down, *, tm=512, tf=256):
    n, d = h.shape
    dff = w_gate.shape[1]
    return pl.pallas_call(
        functools.partial(_out_ffn_kernel, tf=tf),
        grid=(n // tm,),
        in_specs=[
            pl.BlockSpec((tm, d), lambda i: (i, 0)),
            pl.BlockSpec((tm, d), lambda i: (i, 0)),
            _resident((d, d)),
            _resident((1, d)),
            _resident((d, dff)),
            _resident((d, dff)),
            _resident((dff, d)),
        ],
        out_specs=pl.BlockSpec((tm, d), lambda i: (i, 0)),
        out_shape=jax.ShapeDtypeStruct((n, d), F32),
        compiler_params=_cparams(("parallel",)),
        name="out_ffn",
    )(h, a, w_out, g, w_gate, w_up, w_down)


def _mlstm_kernel(main_ref, gc_ref, gr_ref, bc_ref, br_ref, ng_ref, o_ref, c_s, m_s, *, nb):
    L = CHUNK
    H = ML_HEADS

    @pl.when(pl.program_id(1) == 0)
    def _():
        c_s[...] = jnp.zeros_like(c_s)
        m_s[...] = jnp.zeros_like(m_s)

    for bb in range(nb):
        _mlstm_chunk(bb, main_ref, gc_ref, gr_ref, bc_ref, br_ref, ng_ref, o_ref, c_s, m_s)


def _mlstm_chunk(bb, main_ref, gc_ref, gr_ref, bc_ref, br_ref, ng_ref, o_ref, c_s, m_s):
    L = CHUNK
    H = ML_HEADS
    scale = ML_DQK ** -0.5
    row = lax.broadcasted_iota(jnp.int32, (L, L), 0)
    col = lax.broadcasted_iota(jnp.int32, (L, L), 1)
    causal = col <= row
    tri = causal.astype(BF16)
    tri_t = (row <= col).astype(BF16)

    gc = gc_ref[bb] + bc_ref[...]
    gr = gr_ref[bb, 0] + br_ref[...]
    cum_c = _tri_dot(tri, _log_sigmoid(gc))
    li_r = gr[0:H]
    cum_r = _dot_tri(_log_sigmoid(gr[H:2 * H]), tri_t)

    def lane_replicate(x, base):
        gate = lax.broadcasted_iota(jnp.int32, (LANES, H * LANES), 0)
        blk = lax.broadcasted_iota(jnp.int32, (LANES, H * LANES), 1) >> 7
        return _dot_tri(x, (gate == blk + base).astype(BF16))

    cum_b_all = lane_replicate(cum_c, H)
    li_b_all = lane_replicate(gc, 0)
    ones_blk = jnp.ones((L, ML_DV), BF16)
    nq = H * ML_DQK
    nv = H * ML_DV

    qs = [main_ref[bb, :, h * ML_DQK:(h + 1) * ML_DQK] for h in range(H)]
    ks = [main_ref[bb, :, nq + h * ML_DQK:nq + (h + 1) * ML_DQK] for h in range(H)]
    s_qks = [_dot_nt(qs[h], ks[h]) for h in range(H)]
    inters = [_dot(qs[h], c_s[bb * H + h].astype(BF16)) for h in range(H)]

    for h in range(H):
        k = ks[h]
        v = main_ref[bb, :, 2 * nq + h * ML_DV:2 * nq + (h + 1) * ML_DV]
        op = main_ref[bb, :, 2 * nq + nv + h * ML_DV:2 * nq + nv + (h + 1) * ML_DV]
        vext = jnp.concatenate([v, ones_blk], axis=1)
        cum_b = cum_b_all[:, h * LANES:(h + 1) * LANES]
        li_b = li_b_all[:, h * LANES:(h + 1) * LANES]
        cum_r1 = cum_r[h:h + 1, :]
        li_r1 = li_r[h:h + 1, :]
        m_st = m_s[bb * H + h][:, 0:1]
        cext = c_s[bb * H + h]

        log_d = jnp.where(causal, cum_b[:, :L] - cum_r1 + li_r1, NEG)
        m_t = jnp.maximum(cum_b + m_st, jnp.max(log_d, axis=-1, keepdims=True))
        dmat = jnp.exp(log_d - m_t[:, :L])
        w_inter = jnp.exp(cum_b + m_st - m_t)
        sc = (s_qks[h] * (dmat * scale)).astype(BF16)
        inter = inters[h]
        intra = _dot(sc, vext)
        num = w_inter * inter[:, :ML_DV] + intra[:, :ML_DV]
        den = w_inter * inter[:, ML_DV:] + intra[:, ML_DV:]
        hh = num / jnp.maximum(jnp.abs(den), jnp.exp(-m_t))
        hn = hh * lax.rsqrt(jnp.mean(hh * hh, axis=-1, keepdims=True) + EPS)
        out = hn * ng_ref[:, h * ML_DV:(h + 1) * ML_DV] * _sigmoid(op.astype(F32))
        o_ref[bb, :, h * ML_DV:(h + 1) * ML_DV] = out.astype(o_ref.dtype)

        tot_f = cum_r1[:, L - 1:L]
        log_w_r = tot_f - cum_r1 + li_r1
        m_new = jnp.maximum(tot_f + m_st, jnp.max(log_w_r, axis=-1, keepdims=True))
        decay = jnp.exp(tot_f + m_st - m_new)
        w_b = jnp.exp(tot_f - cum_b + li_b - m_new) * scale
        wk = (k.astype(F32) * w_b[:, :ML_DQK]).astype(BF16)
        c_s[bb * H + h] = decay * cext + _dot_tn(wk, vext)
        m_s[bb * H + h] = jnp.broadcast_to(m_new, (1, LANES))


def _mlstm_core(main, gates, b_if, norm_g, *, nb=2):
    b, s, _ = main.shape
    nc = s // CHUNK
    h = ML_HEADS
    gr = gates[:, :, :2 * h].reshape(b, nc, CHUNK, 2 * h).transpose(0, 1, 3, 2)
    bc = jnp.zeros((1, LANES), F32).at[0, :2 * h].set(b_if)
    br = b_if.reshape(2 * h, 1)
    return pl.pallas_call(
        functools.partial(_mlstm_kernel, nb=nb),
        grid=(b // nb, nc),
        in_specs=[
            pl.BlockSpec((nb, CHUNK, main.shape[2]), lambda i, c: (i, c, 0)),
            pl.BlockSpec((nb, CHUNK, LANES), lambda i, c: (i, c, 0)),
            pl.BlockSpec((nb, 1, 2 * h, CHUNK), lambda i, c: (i, c, 0, 0)),
            pl.BlockSpec((1, LANES), lambda i, c: (0, 0)),
            pl.BlockSpec((2 * h, 1), lambda i, c: (0, 0)),
            pl.BlockSpec((1, h * ML_DV), lambda i, c: (0, 0)),
        ],
        out_specs=pl.BlockSpec((nb, CHUNK, h * ML_DV), lambda i, c: (i, c, 0)),
        out_shape=jax.ShapeDtypeStruct((b, s, h * ML_DV), BF16),
        scratch_shapes=[
            pltpu.VMEM((nb * h, ML_DQK, 2 * ML_DV), F32),
            pltpu.VMEM((nb * h, 1, LANES), F32),
        ],
        compiler_params=_cparams(("parallel", "arbitrary")),
        name="mlstm_core",
    )(main, gates, gr, bc, br, norm_g.reshape(1, -1))


def _sb_kernel(q_ref, k_ref, v_ref, o_ref, acc_s, *, t, heads):
    qi = pl.program_id(2)
    lane = lax.broadcasted_iota(jnp.int32, (1, LANES), 1)
    lo = lane < SB_DH
    sels = (lo, jnp.logical_not(lo))
    row = lax.broadcasted_iota(jnp.int32, (t, t), 0)
    col = lax.broadcasted_iota(jnp.int32, (t, t), 1)
    valid = col < row
    lower = (row > col).astype(BF16)
    groups = [slice(g * LANES, (g + 1) * LANES) for g in range(heads // 2)]
    zero = jnp.zeros((), BF16)
    qms = [jnp.where(sels[h % 2], q_ref[0, :, groups[h // 2]], zero) for h in range(heads)]

    def tile(kt, carries, masked):
        k0 = pl.multiple_of(kt * t, t)
        hs = range(heads)
        zs = [_dot_nt(qms[h], k_ref[0, pl.ds(k0, t), groups[h // 2]]) for h in hs]
        nlps = []
        for h in hs:
            nlp = jnp.maximum(jnp.log2(1.0 + jnp.exp2(jnp.minimum(zs[h], SB_Z_CLAMP))), zs[h])
            nlps.append(jnp.where(valid, nlp, 0.0) if masked else nlp)
        laters = [_dot(nlps[h].astype(BF16), lower) for h in hs]
        weights = []
        for h in hs:
            a = jnp.exp2(zs[h] - nlps[h] - laters[h] + carries[h])
            weights.append((jnp.where(valid, a, 0.0) if masked else a).astype(BF16))
        for g, lanes in enumerate(groups):
            vv = v_ref[0, pl.ds(k0, t), lanes]
            acc_s[:, lanes] += (_dot(weights[2 * g], jnp.where(sels[0], vv, zero))
                                + _dot(weights[2 * g + 1], jnp.where(sels[1], vv, zero)))
        return tuple(carries[h] - jnp.sum(nlps[h], axis=-1, keepdims=True) for h in hs)

    acc_s[...] = jnp.zeros_like(acc_s)
    carries = tile(qi, (jnp.zeros((t, 1), F32),) * heads, True)

    def live(carries):
        return functools.reduce(jnp.maximum, [jnp.max(c) for c in carries])

    def cond(st):
        return jnp.logical_and(st[0] >= 0, st[1] > SB_LOG2_ZERO)

    def body(st):
        carries = tile(st[0], st[2], False)
        return st[0] - 1, live(carries), carries

    lax.while_loop(cond, body, (qi - 1, live(carries), carries))
    o_ref[0] = acc_s[...].astype(o_ref.dtype)


def _sb_core(main, *, t=256, heads=4):
    b, s, _ = main.shape
    ng = SB_HEADS // heads
    w = heads * SB_DH
    return pl.pallas_call(
        functools.partial(_sb_kernel, t=t, heads=heads),
        grid=(b, ng, s // t),
        in_specs=[
            pl.BlockSpec((1, t, w), lambda i, p, j: (i, j, p)),
            pl.BlockSpec((1, s, w), lambda i, p, j: (i, 0, ng + p)),
            pl.BlockSpec((1, s, w), lambda i, p, j: (i, 0, 2 * ng + p)),
        ],
        out_specs=pl.BlockSpec((1, t, w), lambda i, p, j: (i, j, p)),
        out_shape=jax.ShapeDtypeStruct((b, s, SB_HEADS * SB_DH), BF16),
        scratch_shapes=[pltpu.VMEM((t, w), F32)],
        compiler_params=_cparams(("parallel", "parallel", "arbitrary")),
        name="sb_core",
    )(main, main, main)


def _fox_scatter_matrices():
    pq = np.zeros((FX_HEADS // 2, LANES, 2 * LANES), np.float32)
    pk = np.zeros((FX_HEADS // 2, LANES, 2 * LANES), np.float32)
    for p in range(FX_HEADS // 2):
        for hh, base in ((0, FX_DH), (1, LANES)):
            head = 2 * p + hh
            for i in range(3):
                pq[p, 16 * i + head, base + i] = 1.0
                pq[p, 48, base + 3 + i] = 1.0
                pk[p, 48, base + i] = 1.0
                pk[p, 16 * i + head, base + 3 + i] = -1.0
    return jnp.asarray(pq, BF16), jnp.asarray(pk, BF16)


def _fox_prep_kernel(main_ref, fp_ref, bf_ref, gq_ref, gk_ref, sh_ref, pq_ref, pk_ref, qa_ref, ka_ref,
                     carry_s, *, ts):
    @pl.when(pl.program_id(1) == 0)
    def _():
        carry_s[...] = jnp.zeros_like(carry_s)

    row = lax.broadcasted_iota(jnp.int32, (ts, ts), 0)
    col = lax.broadcasted_iota(jnp.int32, (ts, ts), 1)
    tri = (col <= row).astype(BF16)
    lf = _log_sigmoid(fp_ref[0] + bf_ref[...])
    cum = _tri_dot(tri, lf) + carry_s[...]
    carry_s[...] = cum[ts - 1:ts, :]

    lane = lax.broadcasted_iota(jnp.int32, (1, LANES), 1)
    lo = lane < FX_DH
    head_lane = lane < FX_HEADS

    def packed(c):
        c = jnp.where(head_lane, c, 0.0)
        hi = c.astype(BF16).astype(F32)
        mid = (c - hi).astype(BF16).astype(F32)
        low = (c - hi - mid).astype(BF16).astype(F32)
        out = hi + pltpu.roll(mid, 16, axis=1) + pltpu.roll(low, 32, axis=1)
        return jnp.where(lane == 48, 1.0, out).astype(BF16)

    pack_q = packed(cum * LOG2E - sh_ref[...])
    pack_k = packed(cum * LOG2E)

    n_q = FX_HEADS * FX_DH // LANES
    for c in range(2 * n_q):
        is_q = c < n_q
        p = c % n_q
        xc = main_ref[0, :, c * LANES:(c + 1) * LANES].astype(F32)
        x2 = xc * xc
        s_lo = jnp.sum(jnp.where(lo, x2, 0.0), axis=-1, keepdims=True)
        s_hi = jnp.sum(jnp.where(lo, 0.0, x2), axis=-1, keepdims=True)
        r = jnp.where(lo, lax.rsqrt(s_lo / FX_DH + EPS), lax.rsqrt(s_hi / FX_DH + EPS))
        xn = xc * r * (gq_ref[...] if is_q else gk_ref[...])
        aug = _dot(pack_q, pq_ref[p]) if is_q else _dot(pack_k, pk_ref[p])
        out_ref = qa_ref if is_q else ka_ref
        out_ref[0, 2 * p] = jnp.where(lo, xn, aug[:, :LANES]).astype(out_ref.dtype)
        out_ref[0, 2 * p + 1] = jnp.where(lo, aug[:, LANES:], xn).astype(out_ref.dtype)


def _fox_prep(main, f_pre, b_f, q_norm_g, k_norm_g, shift, *, ts=256):
    b, s, _ = main.shape
    w = 2 * FX_HEADS * FX_DH
    bf = jnp.zeros((1, LANES), F32).at[0, :FX_HEADS].set(b_f)
    sh = jnp.full((1, LANES), shift, F32)
    gq = jnp.tile(q_norm_g, 2).reshape(1, LANES) * (FX_DH ** -0.5 * LOG2E)
    gk = jnp.tile(k_norm_g, 2).reshape(1, LANES)
    pq, pk = _fox_scatter_matrices()
    head_rows = pl.BlockSpec((1, FX_HEADS, ts, LANES), lambda i, j: (i, 0, j, 0))
    const_row = pl.BlockSpec((1, LANES), lambda i, j: (0, 0))
    scatter = pl.BlockSpec(pq.shape, lambda i, j: (0, 0, 0))
    return pl.pallas_call(
        functools.partial(_fox_prep_kernel, ts=ts),
        grid=(b, s // ts),
        in_specs=[
            pl.BlockSpec((1, ts, w), lambda i, j: (i, j, 0)),
            pl.BlockSpec((1, ts, LANES), lambda i, j: (i, j, 0)),
            const_row, const_row, const_row, const_row, scatter, scatter,
        ],
        out_specs=[head_rows, head_rows],
        out_shape=[jax.ShapeDtypeStruct((b, FX_HEADS, s, LANES), BF16)] * 2,
        scratch_shapes=[pltpu.VMEM((1, LANES), F32)],
        compiler_params=_cparams(("parallel", "arbitrary")),
        name="fox_prep",
    )(main, f_pre, bf, gq, gk, sh, pq, pk)


def _fox_kernel(bounded_ref, q_ref, k_ref, v_ref, op_ref, o_ref, acc_s, *, t):
    qi = pl.program_id(2)
    lane = lax.broadcasted_iota(jnp.int32, (1, LANES), 1)
    lo = lane < FX_DH
    sels = (lo, jnp.logical_not(lo))
    row = lax.broadcasted_iota(jnp.int32, (t, t), 0)
    col = lax.broadcasted_iota(jnp.int32, (t, t), 1)
    causal = col <= row
    qs = (q_ref[0, 0], q_ref[0, 1])
    bounded = bounded_ref[0] != 0

    def scores(hh, k0, width, masked):
        s = _dot_nt(qs[hh], k_ref[0, hh, pl.ds(k0, width), :])
        return jnp.where(causal, s, NEG) if masked else s

    def vext(hh, k0, width):
        return jnp.where(sels[hh], v_ref[0, pl.ds(k0, width), :], jnp.ones((), BF16))

    def plain_tile(k0, width):
        ss = [scores(hh, k0, width, False) for hh in range(2)]
        ps = [jnp.exp2(s).astype(BF16) for s in ss]
        for hh in range(2):
            acc_s[hh] += _dot(ps[hh], vext(hh, k0, width))

    def plain_diagonal(k0):
        half = t // 2
        for r0, width in ((0, half), (half, t)):
            keep = (lax.broadcasted_iota(jnp.int32, (half, width), 1)
                    <= lax.broadcasted_iota(jnp.int32, (half, width), 0) + r0)
            ss = [_dot_nt(qs[hh][r0:r0 + half], k_ref[0, hh, pl.ds(k0, width), :]) for hh in range(2)]
            ps = [jnp.exp2(jnp.where(keep, s, NEG)).astype(BF16) for s in ss]
            for hh in range(2):
                acc_s[hh, r0:r0 + half] += _dot(ps[hh], vext(hh, k0, width))

    def online_tile(kt, ms, masked):
        k0 = pl.multiple_of(kt * t, t)
        new_ms = []
        for hh in range(2):
            s = scores(hh, k0, t, masked)
            m_new = jnp.maximum(ms[hh], jnp.max(s, axis=-1, keepdims=True))
            alpha = jnp.exp2(ms[hh] - m_new)
            p = jnp.exp2(s - m_new).astype(BF16)
            acc_s[hh] = alpha * acc_s[hh] + _dot(p, vext(hh, k0, t))
            new_ms.append(m_new)
        return tuple(new_ms)

    acc_s[...] = jnp.zeros_like(acc_s)

    @pl.when(bounded)
    def _():
        def step(kt, carry):
            plain_tile(pl.multiple_of(kt * t, t), t)
            return carry

        lax.fori_loop(0, qi, step, 0)
        plain_diagonal(pl.multiple_of(qi * t, t))

    @pl.when(jnp.logical_not(bounded))
    def _():
        m0 = jnp.full((t, 1), NEG, F32)
        ms = lax.fori_loop(0, qi, lambda kt, ms: online_tile(kt, ms, False), (m0, m0))
        online_tile(qi, ms, True)

    acc0 = acc_s[0]
    acc1 = acc_s[1]
    out = jnp.where(lo, acc0 / acc0[:, FX_DH:FX_DH + 1], acc1 / acc1[:, 0:1])
    o_ref[0] = (out * _sigmoid(op_ref[0].astype(F32))).astype(o_ref.dtype)


def _fox_logit_bound(q_norm_g, k_norm_g):
    return (FX_DH ** 0.5 * LOG2E) * jnp.max(jnp.abs(q_norm_g)) * jnp.max(jnp.abs(k_norm_g))


def _fox_core(main, qa, ka, bounded, *, t=512):
    b, s, _ = main.shape
    hp = FX_HEADS // 2
    return pl.pallas_call(
        functools.partial(_fox_kernel, t=t),
        grid=(b, hp, s // t),
        in_specs=[
            pl.BlockSpec(memory_space=pltpu.SMEM),
            pl.BlockSpec((1, 2, t, LANES), lambda i, p, j: (i, p, j, 0)),
            pl.BlockSpec((1, 2, s, LANES), lambda i, p, j: (i, p, 0, 0)),
            pl.BlockSpec((1, s, LANES), lambda i, p, j: (i, 0, 2 * hp + p)),
            pl.BlockSpec((1, t, LANES), lambda i, p, j: (i, j, 3 * hp + p)),
        ],
        out_specs=pl.BlockSpec((1, t, LANES), lambda i, p, j: (i, j, p)),
        out_shape=jax.ShapeDtypeStruct((b, s, FX_HEADS * FX_DH), BF16),
        scratch_shapes=[pltpu.VMEM((2, t, LANES), F32)],
        compiler_params=_cparams(("parallel", "parallel", "arbitrary")),
        name="fox_core",
    )(bounded.astype(jnp.int32).reshape(1), qa, ka, main, main)


def _gla_kernel(main_ref, gl_ref, wgu_ref, bg_ref, ng_ref, o_ref, st_s, cum_s, k_s):
    L = CHUNK
    H = GLA_HEADS
    SUB = GLA_SUB
    nk = H * GLA_DK
    nv = H * GLA_DV
    scale = GLA_DK ** -0.5

    @pl.when(pl.program_id(1) == 0)
    def _():
        st_s[...] = jnp.zeros_like(st_s)

    row = lax.broadcasted_iota(jnp.int32, (L, L), 0)
    col = lax.broadcasted_iota(jnp.int32, (L, L), 1)
    tri = (col <= row).astype(BF16)
    eye = (lax.broadcasted_iota(jnp.int32, (GLA_DK, GLA_DK), 0)
           == lax.broadcasted_iota(jnp.int32, (GLA_DK, GLA_DK), 1))

    g_hi, g_lo = _split_bf16(gl_ref[0])
    w_hi, w_lo = _split_bf16(wgu_ref[...])
    logits = _dot(g_hi, w_hi) + _dot(g_hi, w_lo) + _dot(g_lo, w_hi)
    la = _log_sigmoid(logits + bg_ref[...]) * (1.0 / GLA_TAU)
    cum_all = _tri_dot(tri, la)

    heads = range(H)
    nsub = L // SUB
    half = SUB // 2
    qs, v_bfs, cums, decays = [], [], [], []

    inters, updates, a_offs = [], [], []
    for h in heads:
        ks = slice(h * GLA_DK, (h + 1) * GLA_DK)
        cum = cum_all[:, ks]
        tot = cum[L - 1:L, :]
        q = main_ref[0, :, ks].astype(F32) * scale
        k = main_ref[0, :, nk + h * GLA_DK:nk + (h + 1) * GLA_DK].astype(F32)
        v_bf = main_ref[0, :, 2 * nk + h * GLA_DV:2 * nk + (h + 1) * GLA_DV]
        inters.append(_dot((q * jnp.exp(cum)).astype(BF16), st_s[h].astype(BF16)))
        updates.append(_dot_tn((k * jnp.exp(tot - cum)).astype(BF16), v_bf))
        offs = []
        for i in range(1, nsub):
            rs = slice(i * SUB, (i + 1) * SUB)
            ref_pt = cum[i * SUB - 1:i * SUB, :]
            q_rel = (q[rs] * jnp.exp(cum[rs] - ref_pt)).astype(BF16)
            k_rel = (k[:i * SUB] * jnp.exp(ref_pt - cum[:i * SUB])).astype(BF16)
            offs.append(_dot_nt(q_rel, k_rel))
        a_offs.append(offs)
        tot_col = jnp.sum(jnp.where(eye, jnp.broadcast_to(tot, (GLA_DK, GLA_DK)), 0.0),
                          axis=-1, keepdims=True)
        decays.append(jnp.exp(tot_col))
        cum_s[h] = cum
        k_s[h] = k
        qs.append(q)
        v_bfs.append(v_bf)
        cums.append(cum)

    h_offs = [[_dot(a_offs[h][i - 1].astype(BF16), v_bfs[h][:i * SUB]) for i in range(1, nsub)] for h in heads]

    t_half = lax.broadcasted_iota(jnp.int32, (half, 1), 0)
    lane = lax.broadcasted_iota(jnp.int32, (1, LANES), 1)
    for h in heads:
        vs = slice(h * GLA_DV, (h + 1) * GLA_DV)
        for i in range(nsub):
            rs = slice(i * SUB, (i + 1) * SUB)
            q_top, q_bot = qs[h][i * SUB:i * SUB + half], qs[h][i * SUB + half:(i + 1) * SUB]
            c_top, c_bot = cums[h][i * SUB:i * SUB + half], cums[h][i * SUB + half:(i + 1) * SUB]
            a_top = jnp.zeros((half, LANES), F32)
            a_bot = jnp.zeros((half, LANES), F32)
            for j in range(SUB):
                rj = i * SUB + j
                kj = k_s[h, rj:rj + 1, :]
                cj = cum_s[h, rj:rj + 1, :]
                if j < half:
                    e_top = jnp.exp(jnp.where(t_half >= j, c_top - cj, NEG))
                    a_top = jnp.where(lane == j, jnp.sum(q_top * kj * e_top, axis=-1, keepdims=True), a_top)
                    e_bot = jnp.exp(c_bot - cj)
                else:
                    e_bot = jnp.exp(jnp.where(t_half >= j - half, c_bot - cj, NEG))
                a_bot = jnp.where(lane == j, jnp.sum(q_bot * kj * e_bot, axis=-1, keepdims=True), a_bot)
            a_diag = jnp.concatenate([a_top, a_bot], axis=0)[:, :SUB].astype(BF16)
            hb = inters[h][rs] + _dot(a_diag, v_bfs[h][rs])
            if i > 0:
                hb = hb + h_offs[h][i - 1]
            hn = hb * lax.rsqrt(jnp.mean(hb * hb, axis=-1, keepdims=True) + EPS)
            r = main_ref[0, rs, 2 * nk + nv + h * GLA_DV:2 * nk + nv + (h + 1) * GLA_DV].astype(F32)
            out = hn * ng_ref[:, vs] * (r * _sigmoid(r))
            o_ref[0, rs, vs] = out.astype(o_ref.dtype)

    for h in heads:
        st_s[h] = decays[h] * st_s[h] + updates[h]


def _gla_core(main, g_low, w_gate_up, b_gate, norm_g):
    b, s, _ = main.shape
    nc = s // CHUNK
    nk = GLA_HEADS * GLA_DK
    nv = GLA_HEADS * GLA_DV
    wgu = jnp.zeros((LANES, nk), F32).at[:GLA_RANK].set(w_gate_up)
    return pl.pallas_call(
        _gla_kernel,
        grid=(b, nc),
        in_specs=[
            pl.BlockSpec((1, CHUNK, main.shape[2]), lambda i, c: (i, c, 0)),
            pl.BlockSpec((1, CHUNK, LANES), lambda i, c: (i, c, 0)),
            pl.BlockSpec((LANES, nk), lambda i, c: (0, 0)),
            pl.BlockSpec((1, nk), lambda i, c: (0, 0)),
            pl.BlockSpec((1, nv), lambda i, c: (0, 0)),
        ],
        out_specs=pl.BlockSpec((1, CHUNK, nv), lambda i, c: (i, c, 0)),
        out_shape=jax.ShapeDtypeStruct((b, s, nv), BF16),
        scratch_shapes=[
            pltpu.VMEM((GLA_HEADS, GLA_DK, GLA_DV), F32),
            pltpu.VMEM((GLA_HEADS, CHUNK, GLA_DK), F32),
            pltpu.VMEM((GLA_HEADS, CHUNK, GLA_DK), F32),
        ],
        compiler_params=_cparams(("parallel", "arbitrary")),
        name="gla_core",
    )(main, g_low, wgu, b_gate.reshape(1, -1), norm_g.reshape(1, -1))


def _split_w_in(w_in, n_main):
    d = w_in.shape[0]
    n_gate = w_in.shape[1] - n_main
    w_gate = jnp.zeros((d, LANES), F32).at[:, :n_gate].set(w_in[:, n_main:])
    return w_in[:, :n_main].astype(BF16), w_gate.astype(BF16)


def kernel(x, mix_norm_g, ffn_norm_g, ffn_w_gate, ffn_w_up, ffn_w_down, ml_w_in, ml_b_if, ml_norm_g, ml_w_out, sb_w_in, sb_w_out, fx_w_in, fx_b_f, fx_q_norm_g, fx_k_norm_g, fx_w_out, gla_w_in, gla_w_gate_up, gla_b_gate, gla_norm_g, gla_w_out):
    b, s, d = x.shape
    depth = mix_norm_g.shape[0]
    h = x.reshape(b * s, d)
    for layer in range(depth):
        kind = layer % 4
        j = layer // 4
        g_mix = mix_norm_g[layer].reshape(1, d)
        if kind == 0:
            w_main, w_gate = _split_w_in(ml_w_in[j], ml_w_in.shape[2] - 2 * ML_HEADS)
            main, gates = _in_proj(h, g_mix, w_main, w_gate)
            a = _mlstm_core(main.reshape(b, s, -1), gates.reshape(b, s, LANES), ml_b_if[j], ml_norm_g[j])
            w_out = ml_w_out[j]
        elif kind == 1:
            nq = SB_HEADS * SB_DH
            w_sb = jnp.concatenate([sb_w_in[j][:, :nq] * (SB_DH ** -0.5 * LOG2E), sb_w_in[j][:, nq:]], axis=1)
            w_main, w_gate = _split_w_in(w_sb, w_sb.shape[1])
            main, _ = _in_proj(h, g_mix, w_main, w_gate)
            a = _sb_core(main.reshape(b, s, -1))
            w_out = sb_w_out[j]
        elif kind == 2:
            w_main, w_gate = _split_w_in(fx_w_in[j], fx_w_in.shape[2] - FX_HEADS)
            main, gates = _in_proj(h, g_mix, w_main, w_gate)
            main = main.reshape(b, s, -1)
            bound = _fox_logit_bound(fx_q_norm_g[j], fx_k_norm_g[j])
            bounded = bound <= FX_MAX_SHIFT
            qa, ka = _fox_prep(main, gates.reshape(b, s, LANES), fx_b_f[j], fx_q_norm_g[j], fx_k_norm_g[j],
                               jnp.where(bounded, bound, 0.0))
            a = _fox_core(main, qa, ka, bounded)
            w_out = fx_w_out[j]
        else:
            w_main, w_gate = _split_w_in(gla_w_in[j], gla_w_in.shape[2] - GLA_RANK)
            main, gates = _in_proj(h, g_mix, w_main, w_gate)
            a = _gla_core(main.reshape(b, s, -1), gates.reshape(b, s, LANES), gla_w_gate_up[j], gla_b_gate[j], gla_norm_g[j])
            w_out = gla_w_out[j]
        h = _out_ffn(h, a.reshape(b * s, -1), w_out.astype(BF16), ffn_norm_g[layer].reshape(1, d),
                     ffn_w_gate[layer].astype(BF16), ffn_w_up[layer].astype(BF16), ffn_w_down[layer].astype(BF16))
    return h.reshape(b, s, d)
```

```python
import functools

import jax
import jax.numpy as jnp
import numpy as np
from jax import lax
from jax.experimental import pallas as pl
from jax.experimental.pallas import tpu as pltpu

F32 = jnp.float32
BF16 = jnp.bfloat16

D_MODEL = 1024
EPS = 1e-6
CHUNK = 64
ML_HEADS, ML_DQK, ML_DV = 8, 64, 128
SB_HEADS, SB_DH = 16, 64
FX_HEADS, FX_DH = 16, 64
GLA_HEADS, GLA_DK, GLA_DV = 4, 128, 256
GLA_RANK = 16
GLA_TAU = 16.0
GLA_SUB = 16
NEG = -1e30
LOG2E = 1.4426950408889634
FX_MAX_SHIFT = 60.0
SB_LOG2_ZERO = -150.0
SB_Z_CLAMP = 100.0
LANES = 128
VMEM_LIMIT = 56 * 1024 * 1024


def _cparams(sem):
    return pltpu.CompilerParams(dimension_semantics=sem, vmem_limit_bytes=VMEM_LIMIT)


def _resident(shape):
    return pl.BlockSpec(shape, lambda i: (0,) * len(shape), pipeline_mode=pl.Buffered(1))


def _log_sigmoid(x):
    return jnp.minimum(x, 0.0) - jnp.log(1.0 + jnp.exp(-jnp.abs(x)))


def _softplus(x):
    return jnp.maximum(x, 0.0) + jnp.log(1.0 + jnp.exp(-jnp.abs(x)))


def _sigmoid(x):
    return 1.0 / (1.0 + jnp.exp(-x))


def _split_bf16(x):
    hi = x.astype(BF16)
    lo = (x - hi.astype(F32)).astype(BF16)
    return hi, lo


def _dot(a, b):
    return jnp.dot(a, b, preferred_element_type=F32)


def _dot_nt(a, b):
    return lax.dot_general(a, b, (((1,), (1,)), ((), ())), preferred_element_type=F32)


def _dot_tn(a, b):
    return lax.dot_general(a, b, (((0,), (0,)), ((), ())), preferred_element_type=F32)


def _tri_dot(tri, x):
    hi, lo = _split_bf16(x)
    return _dot(tri, hi) + _dot(tri, lo)


def _dot_tri(x, tri):
    hi, lo = _split_bf16(x)
    return _dot(hi, tri) + _dot(lo, tri)


def _in_proj_kernel(x_ref, g_ref, w_ref, *rest, tr, tn):
    wg_ref, main_ref, gate_ref = rest if len(rest) == 3 else (None, rest[0], None)

    def normed(rows):
        x = x_ref[rows, :]
        ms = jnp.mean(x * x, axis=-1, keepdims=True)
        return (x * lax.rsqrt(ms + EPS) * g_ref[...]).astype(BF16)

    blocks = [slice(r * tr, (r + 1) * tr) for r in range(x_ref.shape[0] // tr)]
    u = normed(blocks[0])
    for r, rows in enumerate(blocks):
        u_next = None
        for c in range(main_ref.shape[-1] // tn):
            cs = slice(c * tn, (c + 1) * tn)
            main_ref[rows, cs] = _dot(u, w_ref[:, cs]).astype(main_ref.dtype)
            if c == 0 and r + 1 < len(blocks):
                u_next = normed(blocks[r + 1])
        if gate_ref is not None:
            gate_ref[rows, :] = _dot(u, wg_ref[...])
        u = u_next


def _in_proj(x, g, w_main, w_gate=None, *, tm=1024, tr=256, tn=512):
    n, d = x.shape
    wn = w_main.shape[1]
    in_specs = [pl.BlockSpec((tm, d), lambda i: (i, 0)), _resident((1, d)), _resident((d, wn))]
    out_specs = [pl.BlockSpec((tm, wn), lambda i: (i, 0))]
    out_shape = [jax.ShapeDtypeStruct((n, wn), BF16)]
    operands = [x, g, w_main]
    if w_gate is not None:
        in_specs.append(_resident((d, LANES)))
        out_specs.append(pl.BlockSpec((tm, LANES), lambda i: (i, 0)))
        out_shape.append(jax.ShapeDtypeStruct((n, LANES), F32))
        operands.append(w_gate)
    return pl.pallas_call(
        functools.partial(_in_proj_kernel, tr=tr, tn=tn),
        grid=(n // tm,),
        in_specs=in_specs,
        out_specs=out_specs,
        out_shape=out_shape,
        compiler_params=_cparams(("parallel",)),
        name="in_proj",
    )(*operands)


def _out_ffn_kernel(h_ref, a_ref, wo_ref, g_ref, wg_ref, wu_ref, wd_ref, o_ref, *, tf):
    h1 = h_ref[...] + _dot(a_ref[...], wo_ref[...])
    ms = jnp.mean(h1 * h1, axis=-1, keepdims=True)
    u = (h1 * lax.rsqrt(ms + EPS) * g_ref[...]).astype(BF16)
    acc = h1
    for c in range(wg_ref.shape[1] // tf):
        cs = slice(c * tf, (c + 1) * tf)
        gt = _dot(u, wg_ref[:, cs])
        up = _dot(u, wu_ref[:, cs])
        hid = (gt * _sigmoid(gt) * up).astype(BF16)
        acc = acc + _dot(hid, wd_ref[cs, :])
    o_ref[...] = acc


def _out_ffn(h, a, w_out, g, w_gate, w_up, w_down, *, tm=512, tf=256):
    n, d = h.shape
    dff = w_gate.shape[1]
    return pl.pallas_call(
        functools.partial(_out_ffn_kernel, tf=tf),
        grid=(n // tm,),
        in_specs=[
            pl.BlockSpec((tm, d), lambda i: (i, 0)),
            pl.BlockSpec((tm, d), lambda i: (i, 0)),
            _resident((d, d)),
            _resident((1, d)),
            _resident((d, dff)),
            _resident((d, dff)),
            _resident((dff, d)),
        ],
        out_specs=pl.BlockSpec((tm, d), lambda i: (i, 0)),
        out_shape=jax.ShapeDtypeStruct((n, d), F32),
        compiler_params=_cparams(("parallel",)),
        name="out_ffn",
    )(h, a, w_out, g, w_gate, w_up, w_down)


def _mlstm_kernel(main_ref, gc_ref, gr_ref, bc_ref, br_ref, ng_ref, o_ref, c_s, m_s, *, nb):
    L = CHUNK
    H = ML_HEADS

    @pl.when(pl.program_id(1) == 0)
    def _():
        c_s[...] = jnp.zeros_like(c_s)
        m_s[...] = jnp.zeros_like(m_s)

    for bb in range(nb):
        _mlstm_chunk(bb, main_ref, gc_ref, gr_ref, bc_ref, br_ref, ng_ref, o_ref, c_s, m_s)


def _mlstm_chunk(bb, main_ref, gc_ref, gr_ref, bc_ref, br_ref, ng_ref, o_ref, c_s, m_s):
    L = CHUNK
    H = ML_HEADS
    scale = ML_DQK ** -0.5
    row = lax.broadcasted_iota(jnp.int32, (L, L), 0)
    col = lax.broadcasted_iota(jnp.int32, (L, L), 1)
    causal = col <= row
    tri = causal.astype(BF16)
    tri_t = (row <= col).astype(BF16)

    gc = gc_ref[bb] + bc_ref[...]
    gr = gr_ref[bb, 0] + br_ref[...]
    cum_c = _tri_dot(tri, _log_sigmoid(gc))
    li_r = gr[0:H]
    cum_r = _dot_tri(_log_sigmoid(gr[H:2 * H]), tri_t)

    def lane_replicate(x, base):
        gate = lax.broadcasted_iota(jnp.int32, (LANES, H * LANES), 0)
        blk = lax.broadcasted_iota(jnp.int32, (LANES, H * LANES), 1) >> 7
        return _dot_tri(x, (gate == blk + base).astype(BF16))

    cum_b_all = lane_replicate(cum_c, H)
    li_b_all = lane_replicate(gc, 0)
    ones_blk = jnp.ones((L, ML_DV), BF16)
    nq = H * ML_DQK
    nv = H * ML_DV

    qs = [main_ref[bb, :, h * ML_DQK:(h + 1) * ML_DQK] for h in range(H)]
    ks = [main_ref[bb, :, nq + h * ML_DQK:nq + (h + 1) * ML_DQK] for h in range(H)]
    s_qks = [_dot_nt(qs[h], ks[h]) for h in range(H)]
    inters = [_dot(qs[h], c_s[bb * H + h].astype(BF16)) for h in range(H)]

    for h in range(H):
        k = ks[h]
        v = main_ref[bb, :, 2 * nq + h * ML_DV:2 * nq + (h + 1) * ML_DV]
        op = main_ref[bb, :, 2 * nq + nv + h * ML_DV:2 * nq + nv + (h + 1) * ML_DV]
        vext = jnp.concatenate([v, ones_blk], axis=1)
        cum_b = cum_b_all[:, h * LANES:(h + 1) * LANES]
        li_b = li_b_all[:, h * LANES:(h + 1) * LANES]
        cum_r1 = cum_r[h:h + 1, :]
        li_r1 = li_r[h:h + 1, :]
        m_st = m_s[bb * H + h][:, 0:1]
        cext = c_s[bb * H + h]

        log_d = jnp.where(causal, cum_b[:, :L] - cum_r1 + li_r1, NEG)
        m_t = jnp.maximum(cum_b + m_st, jnp.max(log_d, axis=-1, keepdims=True))
        dmat = jnp.exp(log_d - m_t[:, :L])
        w_inter = jnp.exp(cum_b + m_st - m_t)
        sc = (s_qks[h] * (dmat * scale)).astype(BF16)
        inter = inters[h]
        intra = _dot(sc, vext)
        num = w_inter * inter[:, :ML_DV] + intra[:, :ML_DV]
        den = w_inter * inter[:, ML_DV:] + intra[:, ML_DV:]
        hh = num / jnp.maximum(jnp.abs(den), jnp.exp(-m_t))
        hn = hh * lax.rsqrt(jnp.mean(hh * hh, axis=-1, keepdims=True) + EPS)
        out = hn * ng_ref[:, h * ML_DV:(h + 1) * ML_DV] * _sigmoid(op.astype(F32))
        o_ref[bb, :, h * ML_DV:(h + 1) * ML_DV] = out.astype(o_ref.dtype)

        tot_f = cum_r1[:, L - 1:L]
        log_w_r = tot_f - cum_r1 + li_r1
        m_new = jnp.maximum(tot_f + m_st, jnp.max(log_w_r, axis=-1, keepdims=True))
        decay = jnp.exp(tot_f + m_st - m_new)
        w_b = jnp.exp(tot_f - cum_b + li_b - m_new) * scale
        wk = (k.astype(F32) * w_b[:, :ML_DQK]).astype(BF16)
        c_s[bb * H + h] = decay * cext + _dot_tn(wk, vext)
        m_s[bb * H + h] = jnp.broadcast_to(m_new, (1, LANES))


def _mlstm_core(main, gates, b_if, norm_g, *, nb=2):
    b, s, _ = main.shape
    nc = s // CHUNK
    h = ML_HEADS
    gr = gates[:, :, :2 * h].reshape(b, nc, CHUNK, 2 * h).transpose(0, 1, 3, 2)
    bc = jnp.zeros((1, LANES), F32).at[0, :2 * h].set(b_if)
    br = b_if.reshape(2 * h, 1)
    return pl.pallas_call(
        functools.partial(_mlstm_kernel, nb=nb),
        grid=(b // nb, nc),
        in_specs=[
            pl.BlockSpec((nb, CHUNK, main.shape[2]), lambda i, c: (i, c, 0)),
            pl.BlockSpec((nb, CHUNK, LANES), lambda i, c: (i, c, 0)),
            pl.BlockSpec((nb, 1, 2 * h, CHUNK), lambda i, c: (i, c, 0, 0)),
            pl.BlockSpec((1, LANES), lambda i, c: (0, 0)),
            pl.BlockSpec((2 * h, 1), lambda i, c: (0, 0)),
            pl.BlockSpec((1, h * ML_DV), lambda i, c: (0, 0)),
        ],
        out_specs=pl.BlockSpec((nb, CHUNK, h * ML_DV), lambda i, c: (i, c, 0)),
        out_shape=jax.ShapeDtypeStruct((b, s, h * ML_DV), BF16),
        scratch_shapes=[
            pltpu.VMEM((nb * h, ML_DQK, 2 * ML_DV), F32),
            pltpu.VMEM((nb * h, 1, LANES), F32),
        ],
        compiler_params=_cparams(("parallel", "arbitrary")),
        name="mlstm_core",
    )(main, gates, gr, bc, br, norm_g.reshape(1, -1))


def _sb_kernel(q_ref, k_ref, v_ref, o_ref, acc_s, carry_s, *, t, heads):
    qi = pl.program_id(2)
    lane = lax.broadcasted_iota(jnp.int32, (1, LANES), 1)
    lo = lane < SB_DH
    sels = (lo, jnp.logical_not(lo))
    row = lax.broadcasted_iota(jnp.int32, (t, t), 0)
    col = lax.broadcasted_iota(jnp.int32, (t, t), 1)
    valid = col < row
    lower = (row > col).astype(BF16)
    groups = [slice(g * LANES, (g + 1) * LANES) for g in range(heads // 2)]
    zero = jnp.zeros((), BF16)
    qms = [jnp.where(sels[h % 2], q_ref[0, :, groups[h // 2]], zero) for h in range(heads)]

    def tiles(kts, carries, first_masked):
        hs = range(heads)
        k0s = [pl.multiple_of(kt * t, t) for kt in kts]
        masks = [first_masked and i == 0 for i in range(len(kts))]
        zs = [[_dot_nt(qms[h], k_ref[0, pl.ds(k0, t), groups[h // 2]]) for h in hs] for k0 in k0s]
        nlps = []
        for zt, masked in zip(zs, masks):
            row_nlps = []
            for h in hs:
                nlp = jnp.maximum(jnp.log2(1.0 + jnp.exp2(jnp.minimum(zt[h], SB_Z_CLAMP))), zt[h])
                row_nlps.append(jnp.where(valid, nlp, 0.0) if masked else nlp)
            nlps.append(row_nlps)
        laters = [[_dot(nt[h].astype(BF16), lower) for h in hs] for nt in nlps]
        for zt, nt, lt, masked, k0 in zip(zs, nlps, laters, masks, k0s):
            weights = []
            for h in hs:
                a = jnp.exp2(zt[h] - nt[h] - lt[h] + carries[h])
                weights.append((jnp.where(valid, a, 0.0) if masked else a).astype(BF16))
            for g, lanes in enumerate(groups):
                vv = v_ref[0, pl.ds(k0, t), lanes]
                acc_s[:, lanes] += (_dot(weights[2 * g], jnp.where(sels[0], vv, zero))
                                    + _dot(weights[2 * g + 1], jnp.where(sels[1], vv, zero)))
            carries = tuple(carries[h] - jnp.sum(nt[h], axis=-1, keepdims=True) for h in hs)
        return carries

    acc_s[...] = jnp.zeros_like(acc_s)
    zero_carries = (jnp.zeros((t, 1), F32),) * heads

    @pl.when(qi > 0)
    def _():
        for h, c in enumerate(tiles([qi, qi - 1], zero_carries, True)):
            carry_s[h] = c

    @pl.when(qi == 0)
    def _():
        for h, c in enumerate(tiles([qi], zero_carries, True)):
            carry_s[h] = c

    carries = tuple(carry_s[h] for h in range(heads))

    def live(carries):
        return functools.reduce(jnp.maximum, [jnp.max(c) for c in carries])

    def cond(st):
        return jnp.logical_and(st[0] >= 0, st[1] > SB_LOG2_ZERO)

    def body(st):
        carries = tiles([st[0]], st[2], False)
        return st[0] - 1, live(carries), carries

    lax.while_loop(cond, body, (qi - 2, live(carries), carries))
    o_ref[0] = acc_s[...].astype(o_ref.dtype)


def _sb_core(main, *, t=256, heads=4):
    b, s, _ = main.shape
    ng = SB_HEADS // heads
    w = heads * SB_DH
    return pl.pallas_call(
        functools.partial(_sb_kernel, t=t, heads=heads),
        grid=(b, ng, s // t),
        in_specs=[
            pl.BlockSpec((1, t, w), lambda i, p, j: (i, j, p)),
            pl.BlockSpec((1, s, w), lambda i, p, j: (i, 0, ng + p)),
            pl.BlockSpec((1, s, w), lambda i, p, j: (i, 0, 2 * ng + p)),
        ],
        out_specs=pl.BlockSpec((1, t, w), lambda i, p, j: (i, j, p)),
        out_shape=jax.ShapeDtypeStruct((b, s, SB_HEADS * SB_DH), BF16),
        scratch_shapes=[pltpu.VMEM((t, w), F32), pltpu.VMEM((heads, t, 1), F32)],
        compiler_params=_cparams(("parallel", "parallel", "arbitrary")),
        name="sb_core",
    )(main, main, main)


def _fox_scatter_matrices():
    pq = np.zeros((FX_HEADS // 2, LANES, 2 * LANES), np.float32)
    pk = np.zeros((FX_HEADS // 2, LANES, 2 * LANES), np.float32)
    for p in range(FX_HEADS // 2):
        for hh, base in ((0, FX_DH), (1, LANES)):
            head = 2 * p + hh
            for i in range(3):
                pq[p, 16 * i + head, base + i] = 1.0
                pq[p, 48, base + 3 + i] = 1.0
                pk[p, 48, base + i] = 1.0
                pk[p, 16 * i + head, base + 3 + i] = -1.0
    return jnp.asarray(pq, BF16), jnp.asarray(pk, BF16)


def _fox_prep_kernel(main_ref, fp_ref, bf_ref, gq_ref, gk_ref, sh_ref, pq_ref, pk_ref, qa_ref, ka_ref,
                     carry_s, *, ts):
    @pl.when(pl.program_id(1) == 0)
    def _():
        carry_s[...] = jnp.zeros_like(carry_s)

    row = lax.broadcasted_iota(jnp.int32, (ts, ts), 0)
    col = lax.broadcasted_iota(jnp.int32, (ts, ts), 1)
    tri = (col <= row).astype(BF16)
    lf = _log_sigmoid(fp_ref[0] + bf_ref[...])
    cum = _tri_dot(tri, lf) + carry_s[...]
    carry_s[...] = cum[ts - 1:ts, :]

    lane = lax.broadcasted_iota(jnp.int32, (1, LANES), 1)
    lo = lane < FX_DH
    head_lane = lane < FX_HEADS

    def packed(c):
        c = jnp.where(head_lane, c, 0.0)
        hi = c.astype(BF16).astype(F32)
        mid = (c - hi).astype(BF16).astype(F32)
        low = (c - hi - mid).astype(BF16).astype(F32)
        out = hi + pltpu.roll(mid, 16, axis=1) + pltpu.roll(low, 32, axis=1)
        return jnp.where(lane == 48, 1.0, out).astype(BF16)

    pack_q = packed(cum * LOG2E - sh_ref[...])
    pack_k = packed(cum * LOG2E)

    n_q = FX_HEADS * FX_DH // LANES
    for c in range(2 * n_q):
        is_q = c < n_q
        p = c % n_q
        xc = main_ref[0, :, c * LANES:(c + 1) * LANES].astype(F32)
        x2 = xc * xc
        s_lo = jnp.sum(jnp.where(lo, x2, 0.0), axis=-1, keepdims=True)
        s_hi = jnp.sum(jnp.where(lo, 0.0, x2), axis=-1, keepdims=True)
        r = jnp.where(lo, lax.rsqrt(s_lo / FX_DH + EPS), lax.rsqrt(s_hi / FX_DH + EPS))
        xn = xc * r * (gq_ref[...] if is_q else gk_ref[...])
        aug = _dot(pack_q, pq_ref[p]) if is_q else _dot(pack_k, pk_ref[p])
        out_ref = qa_ref if is_q else ka_ref
        out_ref[0, 2 * p] = jnp.where(lo, xn, aug[:, :LANES]).astype(out_ref.dtype)
        out_ref[0, 2 * p + 1] = jnp.where(lo, aug[:, LANES:], xn).astype(out_ref.dtype)


def _fox_prep(main, f_pre, b_f, q_norm_g, k_norm_g, shift, *, ts=256):
    b, s, _ = main.shape
    w = 2 * FX_HEADS * FX_DH
    bf = jnp.zeros((1, LANES), F32).at[0, :FX_HEADS].set(b_f)
    sh = jnp.full((1, LANES), shift, F32)
    gq = jnp.tile(q_norm_g, 2).reshape(1, LANES) * (FX_DH ** -0.5 * LOG2E)
    gk = jnp.tile(k_norm_g, 2).reshape(1, LANES)
    pq, pk = _fox_scatter_matrices()
    head_rows = pl.BlockSpec((1, FX_HEADS, ts, LANES), lambda i, j: (i, 0, j, 0))
    const_row = pl.BlockSpec((1, LANES), lambda i, j: (0, 0))
    scatter = pl.BlockSpec(pq.shape, lambda i, j: (0, 0, 0))
    return pl.pallas_call(
        functools.partial(_fox_prep_kernel, ts=ts),
        grid=(b, s // ts),
        in_specs=[
            pl.BlockSpec((1, ts, w), lambda i, j: (i, j, 0)),
            pl.BlockSpec((1, ts, LANES), lambda i, j: (i, j, 0)),
            const_row, const_row, const_row, const_row, scatter, scatter,
        ],
        out_specs=[head_rows, head_rows],
        out_shape=[jax.ShapeDtypeStruct((b, FX_HEADS, s, LANES), BF16)] * 2,
        scratch_shapes=[pltpu.VMEM((1, LANES), F32)],
        compiler_params=_cparams(("parallel", "arbitrary")),
        name="fox_prep",
    )(main, f_pre, bf, gq, gk, sh, pq, pk)


def _fox_kernel(bounded_ref, q_ref, k_ref, v_ref, op_ref, o_ref, acc_s, *, t):
    qi = pl.program_id(2)
    lane = lax.broadcasted_iota(jnp.int32, (1, LANES), 1)
    lo = lane < FX_DH
    sels = (lo, jnp.logical_not(lo))
    row = lax.broadcasted_iota(jnp.int32, (t, t), 0)
    col = lax.broadcasted_iota(jnp.int32, (t, t), 1)
    causal = col <= row
    qs = (q_ref[0, 0], q_ref[0, 1])
    bounded = bounded_ref[0] != 0

    def scores(hh, k0, width, masked):
        s = _dot_nt(qs[hh], k_ref[0, hh, pl.ds(k0, width), :])
        return jnp.where(causal, s, NEG) if masked else s

    def vext(hh, k0, width):
        return jnp.where(sels[hh], v_ref[0, pl.ds(k0, width), :], jnp.ones((), BF16))

    def plain_tile(k0, width):
        ss = [scores(hh, k0, width, False) for hh in range(2)]
        ps = [jnp.exp2(s).astype(BF16) for s in ss]
        for hh in range(2):
            acc_s[hh] += _dot(ps[hh], vext(hh, k0, width))

    def plain_diagonal(k0):
        half = t // 2
        parts = [(hh, r0, width) for r0, width in ((0, half), (half, t)) for hh in range(2)]
        ss = [_dot_nt(qs[hh][r0:r0 + half], k_ref[0, hh, pl.ds(k0, width), :]) for hh, r0, width in parts]
        ps = []
        for s, (hh, r0, width) in zip(ss, parts):
            keep = (lax.broadcasted_iota(jnp.int32, (half, width), 1)
                    <= lax.broadcasted_iota(jnp.int32, (half, width), 0) + r0)
            ps.append(jnp.exp2(jnp.where(keep, s, NEG)).astype(BF16))
        for p, (hh, r0, width) in zip(ps, parts):
            acc_s[hh, r0:r0 + half] += _dot(p, vext(hh, k0, width))

    def online_tile(kt, ms, masked):
        k0 = pl.multiple_of(kt * t, t)
        new_ms = []
        for hh in range(2):
            s = scores(hh, k0, t, masked)
            m_new = jnp.maximum(ms[hh], jnp.max(s, axis=-1, keepdims=True))
            alpha = jnp.exp2(ms[hh] - m_new)
            p = jnp.exp2(s - m_new).astype(BF16)
            acc_s[hh] = alpha * acc_s[hh] + _dot(p, vext(hh, k0, t))
            new_ms.append(m_new)
        return tuple(new_ms)

    acc_s[...] = jnp.zeros_like(acc_s)

    @pl.when(bounded)
    def _():
        def step(kt, carry):
            plain_tile(pl.multiple_of(kt * t, t), t)
            return carry

        lax.fori_loop(0, qi, step, 0)
        plain_diagonal(pl.multiple_of(qi * t, t))

    @pl.when(jnp.logical_not(bounded))
    def _():
        m0 = jnp.full((t, 1), NEG, F32)
        ms = lax.fori_loop(0, qi, lambda kt, ms: online_tile(kt, ms, False), (m0, m0))
        online_tile(qi, ms, True)

    acc0 = acc_s[0]
    acc1 = acc_s[1]
    num = jnp.where(lo, acc0, acc1)
    den = pltpu.roll(jnp.where(lo, acc1, acc0), FX_DH, axis=1)
    o_ref[0] = (num / den * _sigmoid(op_ref[0].astype(F32))).astype(o_ref.dtype)


def _fox_logit_bound(q_norm_g, k_norm_g):
    return (FX_DH ** 0.5 * LOG2E) * jnp.max(jnp.abs(q_norm_g)) * jnp.max(jnp.abs(k_norm_g))


def _fox_core(main, qa, ka, bounded, *, t=512):
    b, s, _ = main.shape
    hp = FX_HEADS // 2
    return pl.pallas_call(
        functools.partial(_fox_kernel, t=t),
        grid=(b, hp, s // t),
        in_specs=[
            pl.BlockSpec(memory_space=pltpu.SMEM),
            pl.BlockSpec((1, 2, t, LANES), lambda i, p, j: (i, p, j, 0)),
            pl.BlockSpec((1, 2, s, LANES), lambda i, p, j: (i, p, 0, 0)),
            pl.BlockSpec((1, s, LANES), lambda i, p, j: (i, 0, 2 * hp + p)),
            pl.BlockSpec((1, t, LANES), lambda i, p, j: (i, j, 3 * hp + p)),
        ],
        out_specs=pl.BlockSpec((1, t, LANES), lambda i, p, j: (i, j, p)),
        out_shape=jax.ShapeDtypeStruct((b, s, FX_HEADS * FX_DH), BF16),
        scratch_shapes=[pltpu.VMEM((2, t, LANES), F32)],
        compiler_params=_cparams(("parallel", "parallel", "arbitrary")),
        name="fox_core",
    )(bounded.astype(jnp.int32).reshape(1), qa, ka, main, main)


def _gla_kernel(main_ref, gl_ref, wgu_ref, bg_ref, ng_ref, o_ref, st_s, cum_s, k_s):
    L = CHUNK
    H = GLA_HEADS
    SUB = GLA_SUB
    nk = H * GLA_DK
    nv = H * GLA_DV
    scale = GLA_DK ** -0.5

    @pl.when(pl.program_id(1) == 0)
    def _():
        st_s[...] = jnp.zeros_like(st_s)

    row = lax.broadcasted_iota(jnp.int32, (L, L), 0)
    col = lax.broadcasted_iota(jnp.int32, (L, L), 1)
    tri = (col <= row).astype(BF16)
    eye = (lax.broadcasted_iota(jnp.int32, (GLA_DK, GLA_DK), 0)
           == lax.broadcasted_iota(jnp.int32, (GLA_DK, GLA_DK), 1))

    g_hi, g_lo = _split_bf16(gl_ref[0])
    w_hi, w_lo = _split_bf16(wgu_ref[...])
    logits = _dot(g_hi, w_hi) + _dot(g_hi, w_lo) + _dot(g_lo, w_hi)
    la = _log_sigmoid(logits + bg_ref[...]) * (1.0 / GLA_TAU)
    cum_all = _tri_dot(tri, la)

    heads = range(H)
    nsub = L // SUB
    half = SUB // 2
    qs, v_bfs, cums, decays = [], [], [], []

    inters, updates, a_offs = [], [], []
    for h in heads:
        ks = slice(h * GLA_DK, (h + 1) * GLA_DK)
        cum = cum_all[:, ks]
        tot = cum[L - 1:L, :]
        q = main_ref[0, :, ks].astype(F32) * scale
        k = main_ref[0, :, nk + h * GLA_DK:nk + (h + 1) * GLA_DK].astype(F32)
        v_bf = main_ref[0, :, 2 * nk + h * GLA_DV:2 * nk + (h + 1) * GLA_DV]
        inters.append(_dot((q * jnp.exp(cum)).astype(BF16), st_s[h].astype(BF16)))
        updates.append(_dot_tn((k * jnp.exp(tot - cum)).astype(BF16), v_bf))
        offs = []
        for i in range(1, nsub):
            rs = slice(i * SUB, (i + 1) * SUB)
            ref_pt = cum[i * SUB - 1:i * SUB, :]
            q_rel = (q[rs] * jnp.exp(cum[rs] - ref_pt)).astype(BF16)
            k_rel = (k[:i * SUB] * jnp.exp(ref_pt - cum[:i * SUB])).astype(BF16)
            offs.append(_dot_nt(q_rel, k_rel))
        a_offs.append(offs)
        tot_col = jnp.sum(jnp.where(eye, jnp.broadcast_to(tot, (GLA_DK, GLA_DK)), 0.0),
                          axis=-1, keepdims=True)
        decays.append(jnp.exp(tot_col))
        cum_s[h] = cum
        k_s[h] = k
        qs.append(q)
        v_bfs.append(v_bf)
        cums.append(cum)

    h_offs = [[_dot(a_offs[h][i - 1].astype(BF16), v_bfs[h][:i * SUB]) for i in range(1, nsub)] for h in heads]

    t_half = lax.broadcasted_iota(jnp.int32, (half, 1), 0)
    lane = lax.broadcasted_iota(jnp.int32, (1, LANES), 1)
    for h in heads:
        vs = slice(h * GLA_DV, (h + 1) * GLA_DV)
        for i in range(nsub):
            rs = slice(i * SUB, (i + 1) * SUB)
            q_top, q_bot = qs[h][i * SUB:i * SUB + half], qs[h][i * SUB + half:(i + 1) * SUB]
            c_top, c_bot = cums[h][i * SUB:i * SUB + half], cums[h][i * SUB + half:(i + 1) * SUB]
            a_top = jnp.zeros((half, LANES), F32)
            a_bot = jnp.zeros((half, LANES), F32)
            for j in range(SUB):
                rj = i * SUB + j
                kj = k_s[h, rj:rj + 1, :]
                cj = cum_s[h, rj:rj + 1, :]
                if j < half:
                    e_top = jnp.exp(jnp.where(t_half >= j, c_top - cj, NEG))
                    a_top = jnp.where(lane == j, jnp.sum(q_top * kj * e_top, axis=-1, keepdims=True), a_top)
                    e_bot = jnp.exp(c_bot - cj)
                else:
                    e_bot = jnp.exp(jnp.where(t_half >= j - half, c_bot - cj, NEG))
                a_bot = jnp.where(lane == j, jnp.sum(q_bot * kj * e_bot, axis=-1, keepdims=True), a_bot)
            a_diag = jnp.concatenate([a_top, a_bot], axis=0)[:, :SUB].astype(BF16)
            hb = inters[h][rs] + _dot(a_diag, v_bfs[h][rs])
            if i > 0:
                hb = hb + h_offs[h][i - 1]
            hn = hb * lax.rsqrt(jnp.mean(hb * hb, axis=-1, keepdims=True) + EPS)
            r = main_ref[0, rs, 2 * nk + nv + h * GLA_DV:2 * nk + nv + (h + 1) * GLA_DV].astype(F32)
            out = hn * ng_ref[:, vs] * (r * _sigmoid(r))
            o_ref[0, rs, vs] = out.astype(o_ref.dtype)

    for h in heads:
        st_s[h] = decays[h] * st_s[h] + updates[h]


def _gla_core(main, g_low, w_gate_up, b_gate, norm_g):
    b, s, _ = main.shape
    nc = s // CHUNK
    nk = GLA_HEADS * GLA_DK
    nv = GLA_HEADS * GLA_DV
    wgu = jnp.zeros((LANES, nk), F32).at[:GLA_RANK].set(w_gate_up)
    return pl.pallas_call(
        _gla_kernel,
        grid=(b, nc),
        in_specs=[
            pl.BlockSpec((1, CHUNK, main.shape[2]), lambda i, c: (i, c, 0)),
            pl.BlockSpec((1, CHUNK, LANES), lambda i, c: (i, c, 0)),
            pl.BlockSpec((LANES, nk), lambda i, c: (0, 0)),
            pl.BlockSpec((1, nk), lambda i, c: (0, 0)),
            pl.BlockSpec((1, nv), lambda i, c: (0, 0)),
        ],
        out_specs=pl.BlockSpec((1, CHUNK, nv), lambda i, c: (i, c, 0)),
        out_shape=jax.ShapeDtypeStruct((b, s, nv), BF16),
        scratch_shapes=[
            pltpu.VMEM((GLA_HEADS, GLA_DK, GLA_DV), F32),
            pltpu.VMEM((GLA_HEADS, CHUNK, GLA_DK), F32),
            pltpu.VMEM((GLA_HEADS, CHUNK, GLA_DK), F32),
        ],
        compiler_params=_cparams(("parallel", "arbitrary")),
        name="gla_core",
    )(main, g_low, wgu, b_gate.reshape(1, -1), norm_g.reshape(1, -1))


def _split_w_in(w_in, n_main):
    d = w_in.shape[0]
    n_gate = w_in.shape[1] - n_main
    w_gate = jnp.zeros((d, LANES), F32).at[:, :n_gate].set(w_in[:, n_main:])
    return w_in[:, :n_main].astype(BF16), w_gate.astype(BF16)


def kernel(x, mix_norm_g, ffn_norm_g, ffn_w_gate, ffn_w_up, ffn_w_down, ml_w_in, ml_b_if, ml_norm_g, ml_w_out, sb_w_in, sb_w_out, fx_w_in, fx_b_f, fx_q_norm_g, fx_k_norm_g, fx_w_out, gla_w_in, gla_w_gate_up, gla_b_gate, gla_norm_g, gla_w_out):
    b, s, d = x.shape
    depth = mix_norm_g.shape[0]
    h = x.reshape(b * s, d)
    for layer in range(depth):
        kind = layer % 4
        j = layer // 4
        g_mix = mix_norm_g[layer].reshape(1, d)
        if kind == 0:
            w_main, w_gate = _split_w_in(ml_w_in[j], ml_w_in.shape[2] - 2 * ML_HEADS)
            main, gates = _in_proj(h, g_mix, w_main, w_gate)
            a = _mlstm_core(main.reshape(b, s, -1), gates.reshape(b, s, LANES), ml_b_if[j], ml_norm_g[j])
            w_out = ml_w_out[j]
        elif kind == 1:
            nq = SB_HEADS * SB_DH
            w_sb = jnp.concatenate([sb_w_in[j][:, :nq] * (SB_DH ** -0.5 * LOG2E), sb_w_in[j][:, nq:]], axis=1)
            (main,) = _in_proj(h, g_mix, w_sb.astype(BF16))
            a = _sb_core(main.reshape(b, s, -1))
            w_out = sb_w_out[j]
        elif kind == 2:
            w_main, w_gate = _split_w_in(fx_w_in[j], fx_w_in.shape[2] - FX_HEADS)
            main, gates = _in_proj(h, g_mix, w_main, w_gate)
            main = main.reshape(b, s, -1)
            bound = _fox_logit_bound(fx_q_norm_g[j], fx_k_norm_g[j])
            bounded = bound <= FX_MAX_SHIFT
            qa, ka = _fox_prep(main, gates.reshape(b, s, LANES), fx_b_f[j], fx_q_norm_g[j], fx_k_norm_g[j],
                               jnp.where(bounded, bound, 0.0))
            a = _fox_core(main, qa, ka, bounded)
            w_out = fx_w_out[j]
        else:
            w_main, w_gate = _split_w_in(gla_w_in[j], gla_w_in.shape[2] - GLA_RANK)
            main, gates = _in_proj(h, g_mix, w_main, w_gate)
            a = _gla_core(main.reshape(b, s, -1), gates.reshape(b, s, LANES), gla_w_gate_up[j], gla_b_gate[j], gla_norm_g[j])
            w_out = gla_w_out[j]
        h = _out_ffn(h, a.reshape(b * s, -1), w_out.astype(BF16), ffn_norm_g[layer].reshape(1, d),
                     ffn_w_gate[layer].astype(BF16), ffn_w_up[layer].astype(BF16), ffn_w_down[layer].astype(BF16))
    return h.reshape(b, s, d)
```

```python
import functools

import jax
import jax.numpy as jnp
import numpy as np
from jax import lax
from jax.experimental import pallas as pl
from jax.experimental.pallas import tpu as pltpu

F32 = jnp.float32
BF16 = jnp.bfloat16

D_MODEL = 1024
EPS = 1e-6
CHUNK = 64
ML_HEADS, ML_DQK, ML_DV = 8, 64, 128
SB_HEADS, SB_DH = 16, 64
FX_HEADS, FX_DH = 16, 64
GLA_HEADS, GLA_DK, GLA_DV = 4, 128, 256
GLA_RANK = 16
GLA_TAU = 16.0
GLA_SUB = 16
NEG = -1e30
LOG2E = 1.4426950408889634
FX_MAX_SHIFT = 60.0
SB_LOG2_ZERO = -150.0
SB_Z_CLAMP = 100.0
LANES = 128
VMEM_LIMIT = 56 * 1024 * 1024


def _cparams(sem):
    return pltpu.CompilerParams(dimension_semantics=sem, vmem_limit_bytes=VMEM_LIMIT)


def _resident(shape):
    return pl.BlockSpec(shape, lambda i: (0,) * len(shape), pipeline_mode=pl.Buffered(1))


def _log_sigmoid(x):
    return jnp.minimum(x, 0.0) - jnp.log(1.0 + jnp.exp(-jnp.abs(x)))


def _softplus(x):
    return jnp.maximum(x, 0.0) + jnp.log(1.0 + jnp.exp(-jnp.abs(x)))


def _sigmoid(x):
    return 1.0 / (1.0 + jnp.exp(-x))


def _split_bf16(x):
    hi = x.astype(BF16)
    lo = (x - hi.astype(F32)).astype(BF16)
    return hi, lo


def _dot(a, b):
    return jnp.dot(a, b, preferred_element_type=F32)


def _dot_nt(a, b):
    return lax.dot_general(a, b, (((1,), (1,)), ((), ())), preferred_element_type=F32)


def _dot_tn(a, b):
    return lax.dot_general(a, b, (((0,), (0,)), ((), ())), preferred_element_type=F32)


def _tri_dot(tri, x):
    hi, lo = _split_bf16(x)
    return _dot(tri, hi) + _dot(tri, lo)


def _dot_tri(x, tri):
    hi, lo = _split_bf16(x)
    return _dot(hi, tri) + _dot(lo, tri)


def _in_proj_kernel(x_ref, g_ref, w_ref, *rest, tr, tn):
    wg_ref, main_ref, gate_ref = rest if len(rest) == 3 else (None, rest[0], None)

    def normed(rows):
        x = x_ref[rows, :]
        ms = jnp.mean(x * x, axis=-1, keepdims=True)
        return (x * lax.rsqrt(ms + EPS) * g_ref[...]).astype(BF16)

    blocks = [slice(r * tr, (r + 1) * tr) for r in range(x_ref.shape[0] // tr)]
    u = normed(blocks[0])
    for r, rows in enumerate(blocks):
        u_next = None
        for c in range(main_ref.shape[-1] // tn):
            cs = slice(c * tn, (c + 1) * tn)
            main_ref[rows, cs] = _dot(u, w_ref[:, cs]).astype(main_ref.dtype)
            if c == 0 and r + 1 < len(blocks):
                u_next = normed(blocks[r + 1])
        if gate_ref is not None:
            gate_ref[rows, :] = _dot(u, wg_ref[...])
        u = u_next


def _in_proj(x, g, w_main, w_gate=None, *, tm=1024, tr=256, tn=512):
    n, d = x.shape
    wn = w_main.shape[1]
    in_specs = [pl.BlockSpec((tm, d), lambda i: (i, 0)), _resident((1, d)), _resident((d, wn))]
    out_specs = [pl.BlockSpec((tm, wn), lambda i: (i, 0))]
    out_shape = [jax.ShapeDtypeStruct((n, wn), BF16)]
    operands = [x, g, w_main]
    if w_gate is not None:
        in_specs.append(_resident((d, LANES)))
        out_specs.append(pl.BlockSpec((tm, LANES), lambda i: (i, 0)))
        out_shape.append(jax.ShapeDtypeStruct((n, LANES), F32))
        operands.append(w_gate)
    return pl.pallas_call(
        functools.partial(_in_proj_kernel, tr=tr, tn=tn),
        grid=(n // tm,),
        in_specs=in_specs,
        out_specs=out_specs,
        out_shape=out_shape,
        compiler_params=_cparams(("parallel",)),
        name="in_proj",
    )(*operands)


def _out_ffn_kernel(h_ref, a_ref, wo_ref, g_ref, wg_ref, wu_ref, wd_ref, o_ref, *, tf):
    h1 = h_ref[...] + _dot(a_ref[...], wo_ref[...])
    ms = jnp.mean(h1 * h1, axis=-1, keepdims=True)
    u = (h1 * lax.rsqrt(ms + EPS) * g_ref[...]).astype(BF16)
    acc = h1
    for c in range(wg_ref.shape[1] // tf):
        cs = slice(c * tf, (c + 1) * tf)
        gt = _dot(u, wg_ref[:, cs])
        up = _dot(u, wu_ref[:, cs])
        hid = (gt * _sigmoid(gt) * up).astype(BF16)
        acc = acc + _dot(hid, wd_ref[cs, :])
    o_ref[...] = acc


def _out_ffn(h, a, w_out, g, w_gate, w_up, w_down, layer, *, tm=512, tf=256):
    n, d = h.shape
    dff = w_gate.shape[2]

    def slab(rows, cols):
        return pl.BlockSpec((None, rows, cols), lambda i: (layer, 0, 0), pipeline_mode=pl.Buffered(1))

    return pl.pallas_call(
        functools.partial(_out_ffn_kernel, tf=tf),
        grid=(n // tm,),
        in_specs=[
            pl.BlockSpec((tm, d), lambda i: (i, 0)),
            pl.BlockSpec((tm, d), lambda i: (i, 0)),
            _resident((d, d)),
            _resident((1, d)),
            slab(d, dff),
            slab(d, dff),
            slab(dff, d),
        ],
        out_specs=pl.BlockSpec((tm, d), lambda i: (i, 0)),
        out_shape=jax.ShapeDtypeStruct((n, d), F32),
        compiler_params=_cparams(("parallel",)),
        name="out_ffn",
    )(h, a, w_out, g, w_gate, w_up, w_down)


def _mlstm_kernel(main_ref, gc_ref, gr_ref, bc_ref, br_ref, ng_ref, o_ref, c_s, m_s, *, nb):
    L = CHUNK
    H = ML_HEADS

    @pl.when(pl.program_id(1) == 0)
    def _():
        c_s[...] = jnp.zeros_like(c_s)
        m_s[...] = jnp.zeros_like(m_s)

    staged = [_mlstm_gates(bb, main_ref, gc_ref, gr_ref, bc_ref, br_ref, c_s) for bb in range(nb)]
    _mlstm_heads(staged, main_ref, ng_ref, o_ref, c_s, m_s)


def _mlstm_gates(bb, main_ref, gc_ref, gr_ref, bc_ref, br_ref, c_s):
    L = CHUNK
    H = ML_HEADS
    row = lax.broadcasted_iota(jnp.int32, (L, L), 0)
    col = lax.broadcasted_iota(jnp.int32, (L, L), 1)
    tri = (col <= row).astype(BF16)
    tri_t = (row <= col).astype(BF16)

    gc = gc_ref[bb] + bc_ref[...]
    gr = gr_ref[bb, 0] + br_ref[...]
    cum_c = _tri_dot(tri, _log_sigmoid(gc))
    li_r = gr[0:H]
    cum_r = _dot_tri(_log_sigmoid(gr[H:2 * H]), tri_t)

    def lane_replicate(x, base):
        gate = lax.broadcasted_iota(jnp.int32, (LANES, H * LANES), 0)
        blk = lax.broadcasted_iota(jnp.int32, (LANES, H * LANES), 1) >> 7
        return _dot_tri(x, (gate == blk + base).astype(BF16))

    cum_b_all = lane_replicate(cum_c, H)
    li_b_all = lane_replicate(gc, 0)
    nq = H * ML_DQK
    qs = [main_ref[bb, :, h * ML_DQK:(h + 1) * ML_DQK] for h in range(H)]
    ks = [main_ref[bb, :, nq + h * ML_DQK:nq + (h + 1) * ML_DQK] for h in range(H)]
    s_qks = [_dot_nt(qs[h], ks[h]) for h in range(H)]
    inters = [_dot(qs[h], c_s[bb * H + h].astype(BF16)) for h in range(H)]
    return cum_r, li_r, cum_b_all, li_b_all, ks, s_qks, inters


def _mlstm_heads(staged, main_ref, ng_ref, o_ref, c_s, m_s):
    L = CHUNK
    H = ML_HEADS
    scale = ML_DQK ** -0.5
    nq = H * ML_DQK
    nv = H * ML_DV
    rows = range(len(staged))
    causal = lax.broadcasted_iota(jnp.int32, (L, L), 1) <= lax.broadcasted_iota(jnp.int32, (L, L), 0)
    ones_blk = jnp.ones((L, ML_DV), BF16)

    for h in range(H):
        vexts, m_ts, w_inters, scs, wks, decays, m_news = [], [], [], [], [], [], []
        for bb in rows:
            cum_r, li_r, cum_b_all, li_b_all, ks, s_qks, _ = staged[bb]
            v = main_ref[bb, :, 2 * nq + h * ML_DV:2 * nq + (h + 1) * ML_DV]
            vexts.append(jnp.concatenate([v, ones_blk], axis=1))
            cum_b = cum_b_all[:, h * LANES:(h + 1) * LANES]
            li_b = li_b_all[:, h * LANES:(h + 1) * LANES]
            cum_r1 = cum_r[h:h + 1, :]
            li_r1 = li_r[h:h + 1, :]
            m_st = m_s[bb * H + h][:, 0:1]

            log_d = jnp.where(causal, cum_b[:, :L] - cum_r1 + li_r1, NEG)
            m_t = jnp.maximum(cum_b + m_st, jnp.max(log_d, axis=-1, keepdims=True))
            dmat = jnp.exp(log_d - m_t[:, :L])
            m_ts.append(m_t)
            w_inters.append(jnp.exp(cum_b + m_st - m_t))
            scs.append((s_qks[h] * (dmat * scale)).astype(BF16))

            tot_f = cum_r1[:, L - 1:L]
            log_w_r = tot_f - cum_r1 + li_r1
            m_new = jnp.maximum(tot_f + m_st, jnp.max(log_w_r, axis=-1, keepdims=True))
            m_news.append(m_new)
            decays.append(jnp.exp(tot_f + m_st - m_new))
            w_b = jnp.exp(tot_f - cum_b + li_b - m_new) * scale
            wks.append((ks[h].astype(F32) * w_b[:, :ML_DQK]).astype(BF16))

        intras = [_dot(scs[bb], vexts[bb]) for bb in rows]
        updates = [_dot_tn(wks[bb], vexts[bb]) for bb in rows]

        for bb in rows:
            inter = staged[bb][6][h]
            num = w_inters[bb] * inter[:, :ML_DV] + intras[bb][:, :ML_DV]
            den = w_inters[bb] * inter[:, ML_DV:] + intras[bb][:, ML_DV:]
            hh = num / jnp.maximum(jnp.abs(den), jnp.exp(-m_ts[bb]))
            hn = hh * lax.rsqrt(jnp.mean(hh * hh, axis=-1, keepdims=True) + EPS)
            op = main_ref[bb, :, 2 * nq + nv + h * ML_DV:2 * nq + nv + (h + 1) * ML_DV]
            out = hn * ng_ref[:, h * ML_DV:(h + 1) * ML_DV] * _sigmoid(op.astype(F32))
            o_ref[bb, :, h * ML_DV:(h + 1) * ML_DV] = out.astype(o_ref.dtype)
            c_s[bb * H + h] = decays[bb] * c_s[bb * H + h] + updates[bb]
            m_s[bb * H + h] = jnp.broadcast_to(m_news[bb], (1, LANES))


def _mlstm_core(main, gates, b_if, norm_g, *, nb=2):
    b, s, _ = main.shape
    nc = s // CHUNK
    h = ML_HEADS
    gr = gates[:, :, :2 * h].reshape(b, nc, CHUNK, 2 * h).transpose(0, 1, 3, 2)
    bc = jnp.zeros((1, LANES), F32).at[0, :2 * h].set(b_if)
    br = b_if.reshape(2 * h, 1)
    return pl.pallas_call(
        functools.partial(_mlstm_kernel, nb=nb),
        grid=(b // nb, nc),
        in_specs=[
            pl.BlockSpec((nb, CHUNK, main.shape[2]), lambda i, c: (i, c, 0)),
            pl.BlockSpec((nb, CHUNK, LANES), lambda i, c: (i, c, 0)),
            pl.BlockSpec((nb, 1, 2 * h, CHUNK), lambda i, c: (i, c, 0, 0)),
            pl.BlockSpec((1, LANES), lambda i, c: (0, 0)),
            pl.BlockSpec((2 * h, 1), lambda i, c: (0, 0)),
            pl.BlockSpec((1, h * ML_DV), lambda i, c: (0, 0)),
        ],
        out_specs=pl.BlockSpec((nb, CHUNK, h * ML_DV), lambda i, c: (i, c, 0)),
        out_shape=jax.ShapeDtypeStruct((b, s, h * ML_DV), BF16),
        scratch_shapes=[
            pltpu.VMEM((nb * h, ML_DQK, 2 * ML_DV), F32),
            pltpu.VMEM((nb * h, 1, LANES), F32),
        ],
        compiler_params=_cparams(("parallel", "arbitrary")),
        name="mlstm_core",
    )(main, gates, gr, bc, br, norm_g.reshape(1, -1))


def _sb_kernel(q_ref, k_ref, v_ref, o_ref, acc_s, carry_s, *, t, heads):
    qi = pl.program_id(2)
    lane = lax.broadcasted_iota(jnp.int32, (1, LANES), 1)
    lo = lane < SB_DH
    sels = (lo, jnp.logical_not(lo))
    row = lax.broadcasted_iota(jnp.int32, (t, t), 0)
    col = lax.broadcasted_iota(jnp.int32, (t, t), 1)
    valid = col < row
    lower = (row > col).astype(BF16)
    groups = [slice(g * LANES, (g + 1) * LANES) for g in range(heads // 2)]
    zero = jnp.zeros((), BF16)
    qms = [jnp.where(sels[h % 2], q_ref[0, :, groups[h // 2]], zero) for h in range(heads)]

    def tiles(kts, carries, from_diagonal):
        hs = range(heads)
        k0s = [pl.multiple_of(kt * t, t) for kt in kts]
        masks = [from_diagonal and i == 0 for i in range(len(kts))]
        zs = [[_dot_nt(qms[h], k_ref[0, pl.ds(k0, t), groups[h // 2]]) for h in hs] for k0 in k0s]
        nlps = []
        for zt, masked in zip(zs, masks):
            row_nlps = []
            for h in hs:
                nlp = jnp.maximum(jnp.log2(1.0 + jnp.exp2(jnp.minimum(zt[h], SB_Z_CLAMP))), zt[h])
                row_nlps.append(jnp.where(valid, nlp, 0.0) if masked else nlp)
            nlps.append(row_nlps)
        laters = [[_dot(nt[h].astype(BF16), lower) for h in hs] for nt in nlps]
        for zt, nt, lt, masked, k0 in zip(zs, nlps, laters, masks, k0s):
            weights = []
            for h in hs:
                a = jnp.exp2(zt[h] - nt[h] - lt[h] + carries[h])
                weights.append((jnp.where(valid, a, 0.0) if masked else a).astype(BF16))
            for g, lanes in enumerate(groups):
                vv = v_ref[0, pl.ds(k0, t), lanes]
                contrib = (_dot(weights[2 * g], jnp.where(sels[0], vv, zero))
                           + _dot(weights[2 * g + 1], jnp.where(sels[1], vv, zero)))
                if masked:
                    acc_s[:, lanes] = contrib
                else:
                    acc_s[:, lanes] += contrib
            carries = tuple(carries[h] - jnp.sum(nt[h], axis=-1, keepdims=True) for h in hs)
        return carries

    zero_carries = (jnp.zeros((t, 1), F32),) * heads

    @pl.when(qi > 0)
    def _():
        for h, c in enumerate(tiles([qi, qi - 1], zero_carries, True)):
            carry_s[h] = c

    @pl.when(qi == 0)
    def _():
        for h, c in enumerate(tiles([qi], zero_carries, True)):
            carry_s[h] = c

    carries = tuple(carry_s[h] for h in range(heads))

    def live(carries):
        return functools.reduce(jnp.maximum, [jnp.max(c) for c in carries])

    def cond(st):
        return jnp.logical_and(st[0] >= 0, st[1] > SB_LOG2_ZERO)

    def body(st):
        carries = tiles([st[0]], st[2], False)
        return st[0] - 1, live(carries), carries

    lax.while_loop(cond, body, (qi - 2, live(carries), carries))
    o_ref[0] = acc_s[...].astype(o_ref.dtype)


def _sb_core(main, *, t=256, heads=4):
    b, s, _ = main.shape
    ng = SB_HEADS // heads
    w = heads * SB_DH
    return pl.pallas_call(
        functools.partial(_sb_kernel, t=t, heads=heads),
        grid=(b, ng, s // t),
        in_specs=[
            pl.BlockSpec((1, t, w), lambda i, p, j: (i, j, p)),
            pl.BlockSpec((1, s, w), lambda i, p, j: (i, 0, ng + p)),
            pl.BlockSpec((1, s, w), lambda i, p, j: (i, 0, 2 * ng + p)),
        ],
        out_specs=pl.BlockSpec((1, t, w), lambda i, p, j: (i, j, p)),
        out_shape=jax.ShapeDtypeStruct((b, s, SB_HEADS * SB_DH), BF16),
        scratch_shapes=[pltpu.VMEM((t, w), F32), pltpu.VMEM((heads, t, 1), F32)],
        compiler_params=_cparams(("parallel", "parallel", "arbitrary")),
        name="sb_core",
    )(main, main, main)


def _fox_scatter_matrices():
    pq = np.zeros((FX_HEADS // 2, LANES, 2 * LANES), np.float32)
    pk = np.zeros((FX_HEADS // 2, LANES, 2 * LANES), np.float32)
    for p in range(FX_HEADS // 2):
        for hh, base in ((0, FX_DH), (1, LANES)):
            head = 2 * p + hh
            for i in range(3):
                pq[p, 16 * i + head, base + i] = 1.0
                pq[p, 48, base + 3 + i] = 1.0
                pk[p, 48, base + i] = 1.0
                pk[p, 16 * i + head, base + 3 + i] = -1.0
    return jnp.asarray(pq, BF16), jnp.asarray(pk, BF16)


def _fox_prep_kernel(main_ref, fp_ref, bf_ref, gq_ref, gk_ref, sh_ref, pq_ref, pk_ref, qa_ref, ka_ref,
                     carry_s, *, ts):
    @pl.when(pl.program_id(1) == 0)
    def _():
        carry_s[...] = jnp.zeros_like(carry_s)

    row = lax.broadcasted_iota(jnp.int32, (ts, ts), 0)
    col = lax.broadcasted_iota(jnp.int32, (ts, ts), 1)
    tri = (col <= row).astype(BF16)
    lf = _log_sigmoid(fp_ref[0] + bf_ref[...])
    cum = _tri_dot(tri, lf) + carry_s[...]
    carry_s[...] = cum[ts - 1:ts, :]

    lane = lax.broadcasted_iota(jnp.int32, (1, LANES), 1)
    lo = lane < FX_DH
    head_lane = lane < FX_HEADS

    def packed(c):
        c = jnp.where(head_lane, c, 0.0)
        hi = c.astype(BF16).astype(F32)
        mid = (c - hi).astype(BF16).astype(F32)
        low = (c - hi - mid).astype(BF16).astype(F32)
        out = hi + pltpu.roll(mid, 16, axis=1) + pltpu.roll(low, 32, axis=1)
        return jnp.where(lane == 48, 1.0, out).astype(BF16)

    pack_q = packed(cum * LOG2E - sh_ref[...])
    pack_k = packed(cum * LOG2E)

    n_q = FX_HEADS * FX_DH // LANES
    for c in range(2 * n_q):
        is_q = c < n_q
        p = c % n_q
        xc = main_ref[0, :, c * LANES:(c + 1) * LANES].astype(F32)
        x2 = xc * xc
        s_lo = jnp.sum(jnp.where(lo, x2, 0.0), axis=-1, keepdims=True)
        s_hi = jnp.sum(jnp.where(lo, 0.0, x2), axis=-1, keepdims=True)
        r = jnp.where(lo, lax.rsqrt(s_lo / FX_DH + EPS), lax.rsqrt(s_hi / FX_DH + EPS))
        xn = xc * r * (gq_ref[...] if is_q else gk_ref[...])
        aug = _dot(pack_q, pq_ref[p]) if is_q else _dot(pack_k, pk_ref[p])
        out_ref = qa_ref if is_q else ka_ref
        out_ref[0, 2 * p] = jnp.where(lo, xn, aug[:, :LANES]).astype(out_ref.dtype)
        out_ref[0, 2 * p + 1] = jnp.where(lo, aug[:, LANES:], xn).astype(out_ref.dtype)


def _fox_prep(main, f_pre, b_f, q_norm_g, k_norm_g, shift, *, ts=256):
    b, s, _ = main.shape
    w = 2 * FX_HEADS * FX_DH
    bf = jnp.zeros((1, LANES), F32).at[0, :FX_HEADS].set(b_f)
    sh = jnp.full((1, LANES), shift, F32)
    gq = jnp.tile(q_norm_g, 2).reshape(1, LANES) * (FX_DH ** -0.5 * LOG2E)
    gk = jnp.tile(k_norm_g, 2).reshape(1, LANES)
    pq, pk = _fox_scatter_matrices()
    head_rows = pl.BlockSpec((1, FX_HEADS, ts, LANES), lambda i, j: (i, 0, j, 0))
    const_row = pl.BlockSpec((1, LANES), lambda i, j: (0, 0))
    scatter = pl.BlockSpec(pq.shape, lambda i, j: (0, 0, 0))
    return pl.pallas_call(
        functools.partial(_fox_prep_kernel, ts=ts),
        grid=(b, s // ts),
        in_specs=[
            pl.BlockSpec((1, ts, w), lambda i, j: (i, j, 0)),
            pl.BlockSpec((1, ts, LANES), lambda i, j: (i, j, 0)),
            const_row, const_row, const_row, const_row, scatter, scatter,
        ],
        out_specs=[head_rows, head_rows],
        out_shape=[jax.ShapeDtypeStruct((b, FX_HEADS, s, LANES), BF16)] * 2,
        scratch_shapes=[pltpu.VMEM((1, LANES), F32)],
        compiler_params=_cparams(("parallel", "arbitrary")),
        name="fox_prep",
    )(main, f_pre, bf, gq, gk, sh, pq, pk)


def _fox_kernel(bounded_ref, q_ref, k_ref, v_ref, op_ref, o_ref, acc_s, *, t):
    qi = pl.program_id(2)
    lane = lax.broadcasted_iota(jnp.int32, (1, LANES), 1)
    lo = lane < FX_DH
    sels = (lo, jnp.logical_not(lo))
    row = lax.broadcasted_iota(jnp.int32, (t, t), 0)
    col = lax.broadcasted_iota(jnp.int32, (t, t), 1)
    causal = col <= row
    qs = (q_ref[0, 0], q_ref[0, 1])
    bounded = bounded_ref[0] != 0

    def scores(hh, k0, width, masked):
        s = _dot_nt(qs[hh], k_ref[0, hh, pl.ds(k0, width), :])
        return jnp.where(causal, s, NEG) if masked else s

    def vext(hh, k0, width):
        return jnp.where(sels[hh], v_ref[0, pl.ds(k0, width), :], jnp.ones((), BF16))

    def plain_tile(k0, width):
        ss = [scores(hh, k0, width, False) for hh in range(2)]
        ps = [jnp.exp2(s).astype(BF16) for s in ss]
        for hh in range(2):
            acc_s[hh] += _dot(ps[hh], vext(hh, k0, width))

    def plain_diagonal(k0):
        half = t // 2
        parts = [(hh, r0, width) for r0, width in ((0, half), (half, t)) for hh in range(2)]
        ss = [_dot_nt(qs[hh][r0:r0 + half], k_ref[0, hh, pl.ds(k0, width), :]) for hh, r0, width in parts]
        ps = []
        for s, (hh, r0, width) in zip(ss, parts):
            keep = (lax.broadcasted_iota(jnp.int32, (half, width), 1)
                    <= lax.broadcasted_iota(jnp.int32, (half, width), 0) + r0)
            ps.append(jnp.exp2(jnp.where(keep, s, NEG)).astype(BF16))
        for p, (hh, r0, width) in zip(ps, parts):
            acc_s[hh, r0:r0 + half] = _dot(p, vext(hh, k0, width))

    def online_tile(kt, ms, masked):
        k0 = pl.multiple_of(kt * t, t)
        new_ms = []
        for hh in range(2):
            s = scores(hh, k0, t, masked)
            m_new = jnp.maximum(ms[hh], jnp.max(s, axis=-1, keepdims=True))
            alpha = jnp.exp2(ms[hh] - m_new)
            p = jnp.exp2(s - m_new).astype(BF16)
            acc_s[hh] = alpha * acc_s[hh] + _dot(p, vext(hh, k0, t))
            new_ms.append(m_new)
        return tuple(new_ms)

    @pl.when(bounded)
    def _():
        plain_diagonal(pl.multiple_of(qi * t, t))

        def step(kt, carry):
            plain_tile(pl.multiple_of(kt * t, t), t)
            return carry

        lax.fori_loop(0, qi, step, 0)

    @pl.when(jnp.logical_not(bounded))
    def _():
        acc_s[...] = jnp.zeros_like(acc_s)
        m0 = jnp.full((t, 1), NEG, F32)
        ms = lax.fori_loop(0, qi, lambda kt, ms: online_tile(kt, ms, False), (m0, m0))
        online_tile(qi, ms, True)

    acc0 = acc_s[0]
    acc1 = acc_s[1]
    num = jnp.where(lo, acc0, acc1)
    den = pltpu.roll(jnp.where(lo, acc1, acc0), FX_DH, axis=1)
    o_ref[0] = (num / den * _sigmoid(op_ref[0].astype(F32))).astype(o_ref.dtype)


def _fox_logit_bound(q_norm_g, k_norm_g):
    return (FX_DH ** 0.5 * LOG2E) * jnp.max(jnp.abs(q_norm_g)) * jnp.max(jnp.abs(k_norm_g))


def _fox_core(main, qa, ka, bounded, *, t=512):
    b, s, _ = main.shape
    hp = FX_HEADS // 2
    return pl.pallas_call(
        functools.partial(_fox_kernel, t=t),
        grid=(b, hp, s // t),
        in_specs=[
            pl.BlockSpec(memory_space=pltpu.SMEM),
            pl.BlockSpec((1, 2, t, LANES), lambda i, p, j: (i, p, j, 0)),
            pl.BlockSpec((1, 2, s, LANES), lambda i, p, j: (i, p, 0, 0)),
            pl.BlockSpec((1, s, LANES), lambda i, p, j: (i, 0, 2 * hp + p)),
            pl.BlockSpec((1, t, LANES), lambda i, p, j: (i, j, 3 * hp + p)),
        ],
        out_specs=pl.BlockSpec((1, t, LANES), lambda i, p, j: (i, j, p)),
        out_shape=jax.ShapeDtypeStruct((b, s, FX_HEADS * FX_DH), BF16),
        scratch_shapes=[pltpu.VMEM((2, t, LANES), F32)],
        compiler_params=_cparams(("parallel", "parallel", "arbitrary")),
        name="fox_core",
    )(bounded.astype(jnp.int32).reshape(1), qa, ka, main, main)


def _gla_kernel(main_ref, gl_ref, wgu_ref, bg_ref, ng_ref, o_ref, st_s, cum_s, k_s, *, nb):
    L = CHUNK
    H = GLA_HEADS
    SUB = GLA_SUB
    nk = H * GLA_DK
    nv = H * GLA_DV
    scale = GLA_DK ** -0.5

    @pl.when(pl.program_id(1) == 0)
    def _():
        st_s[...] = jnp.zeros_like(st_s)

    row = lax.broadcasted_iota(jnp.int32, (L, L), 0)
    col = lax.broadcasted_iota(jnp.int32, (L, L), 1)
    tri = (col <= row).astype(BF16)
    eye = (lax.broadcasted_iota(jnp.int32, (GLA_DK, GLA_DK), 0)
           == lax.broadcasted_iota(jnp.int32, (GLA_DK, GLA_DK), 1))

    w_hi, w_lo = _split_bf16(wgu_ref[...])
    cum_alls = []
    for bb in range(nb):
        g_hi, g_lo = _split_bf16(gl_ref[bb])
        logits = _dot(g_hi, w_hi) + _dot(g_hi, w_lo) + _dot(g_lo, w_hi)
        la = _log_sigmoid(logits + bg_ref[...]) * (1.0 / GLA_TAU)
        cum_alls.append(_tri_dot(tri, la))

    units = [(bb, h) for bb in range(nb) for h in range(H)]
    nsub = L // SUB
    half = SUB // 2
    qs, v_bfs, cums, decays = [], [], [], []

    inters, updates, a_offs = [], [], []
    for u, (bb, h) in enumerate(units):
        ks = slice(h * GLA_DK, (h + 1) * GLA_DK)
        cum = cum_alls[bb][:, ks]
        tot = cum[L - 1:L, :]
        q = main_ref[bb, :, ks].astype(F32) * scale
        k = main_ref[bb, :, nk + h * GLA_DK:nk + (h + 1) * GLA_DK].astype(F32)
        v_bf = main_ref[bb, :, 2 * nk + h * GLA_DV:2 * nk + (h + 1) * GLA_DV]
        inters.append(_dot((q * jnp.exp(cum)).astype(BF16), st_s[u].astype(BF16)))
        updates.append(_dot_tn((k * jnp.exp(tot - cum)).astype(BF16), v_bf))
        offs = []
        for i in range(1, nsub):
            rs = slice(i * SUB, (i + 1) * SUB)
            ref_pt = cum[i * SUB - 1:i * SUB, :]
            q_rel = (q[rs] * jnp.exp(cum[rs] - ref_pt)).astype(BF16)
            k_rel = (k[:i * SUB] * jnp.exp(ref_pt - cum[:i * SUB])).astype(BF16)
            offs.append(_dot_nt(q_rel, k_rel))
        a_offs.append(offs)
        tot_col = jnp.sum(jnp.where(eye, jnp.broadcast_to(tot, (GLA_DK, GLA_DK)), 0.0),
                          axis=-1, keepdims=True)
        decays.append(jnp.exp(tot_col))
        cum_s[u] = cum
        k_s[u] = k
        qs.append(q)
        v_bfs.append(v_bf)
        cums.append(cum)

    h_offs = [[_dot(a_offs[u][i - 1].astype(BF16), v_bfs[u][:i * SUB]) for i in range(1, nsub)]
              for u in range(len(units))]

    t_half = lax.broadcasted_iota(jnp.int32, (half, 1), 0)
    lane = lax.broadcasted_iota(jnp.int32, (1, LANES), 1)
    for u, (bb, h) in enumerate(units):
        vs = slice(h * GLA_DV, (h + 1) * GLA_DV)
        for i in range(nsub):
            rs = slice(i * SUB, (i + 1) * SUB)
            q_top, q_bot = qs[u][i * SUB:i * SUB + half], qs[u][i * SUB + half:(i + 1) * SUB]
            c_top, c_bot = cums[u][i * SUB:i * SUB + half], cums[u][i * SUB + half:(i + 1) * SUB]
            a_top = jnp.zeros((half, LANES), F32)
            a_bot = jnp.zeros((half, LANES), F32)
            for j in range(SUB):
                rj = i * SUB + j
                kj = k_s[u, rj:rj + 1, :]
                cj = cum_s[u, rj:rj + 1, :]
                if j < half:
                    e_top = jnp.exp(jnp.where(t_half >= j, c_top - cj, NEG))
                    a_top = jnp.where(lane == j, jnp.sum(q_top * kj * e_top, axis=-1, keepdims=True), a_top)
                    e_bot = jnp.exp(c_bot - cj)
                else:
                    e_bot = jnp.exp(jnp.where(t_half >= j - half, c_bot - cj, NEG))
                a_bot = jnp.where(lane == j, jnp.sum(q_bot * kj * e_bot, axis=-1, keepdims=True), a_bot)
            a_diag = jnp.concatenate([a_top, a_bot], axis=0)[:, :SUB].astype(BF16)
            hb = inters[u][rs] + _dot(a_diag, v_bfs[u][rs])
            if i > 0:
                hb = hb + h_offs[u][i - 1]
            hn = hb * lax.rsqrt(jnp.mean(hb * hb, axis=-1, keepdims=True) + EPS)
            r = main_ref[bb, rs, 2 * nk + nv + h * GLA_DV:2 * nk + nv + (h + 1) * GLA_DV].astype(F32)
            out = hn * ng_ref[:, vs] * (r * _sigmoid(r))
            o_ref[bb, rs, vs] = out.astype(o_ref.dtype)

    for u in range(len(units)):
        st_s[u] = decays[u] * st_s[u] + updates[u]


def _gla_core(main, g_low, w_gate_up, b_gate, norm_g, *, nb=2):
    b, s, _ = main.shape
    nc = s // CHUNK
    nk = GLA_HEADS * GLA_DK
    nv = GLA_HEADS * GLA_DV
    wgu = jnp.zeros((LANES, nk), F32).at[:GLA_RANK].set(w_gate_up)
    return pl.pallas_call(
        functools.partial(_gla_kernel, nb=nb),
        grid=(b // nb, nc),
        in_specs=[
            pl.BlockSpec((nb, CHUNK, main.shape[2]), lambda i, c: (i, c, 0)),
            pl.BlockSpec((nb, CHUNK, LANES), lambda i, c: (i, c, 0)),
            pl.BlockSpec((LANES, nk), lambda i, c: (0, 0)),
            pl.BlockSpec((1, nk), lambda i, c: (0, 0)),
            pl.BlockSpec((1, nv), lambda i, c: (0, 0)),
        ],
        out_specs=pl.BlockSpec((nb, CHUNK, nv), lambda i, c: (i, c, 0)),
        out_shape=jax.ShapeDtypeStruct((b, s, nv), BF16),
        scratch_shapes=[
            pltpu.VMEM((nb * GLA_HEADS, GLA_DK, GLA_DV), F32),
            pltpu.VMEM((nb * GLA_HEADS, CHUNK, GLA_DK), F32),
            pltpu.VMEM((nb * GLA_HEADS, CHUNK, GLA_DK), F32),
        ],
        compiler_params=_cparams(("parallel", "arbitrary")),
        name="gla_core",
    )(main, g_low, wgu, b_gate.reshape(1, -1), norm_g.reshape(1, -1))


def _split_w_in(w_in, n_main):
    n_gate = w_in.shape[1] - n_main
    w_bf = w_in.astype(BF16)
    return w_bf[:, :n_main], jnp.pad(w_bf[:, n_main:], ((0, 0), (0, LANES - n_gate)))


def kernel(x, mix_norm_g, ffn_norm_g, ffn_w_gate, ffn_w_up, ffn_w_down, ml_w_in, ml_b_if, ml_norm_g, ml_w_out, sb_w_in, sb_w_out, fx_w_in, fx_b_f, fx_q_norm_g, fx_k_norm_g, fx_w_out, gla_w_in, gla_w_gate_up, gla_b_gate, gla_norm_g, gla_w_out):
    b, s, d = x.shape
    depth = mix_norm_g.shape[0]
    h = x.reshape(b * s, d)
    ffn_gate, ffn_up, ffn_down = (w.astype(BF16) for w in (ffn_w_gate, ffn_w_up, ffn_w_down))
    for layer in range(depth):
        kind = layer % 4
        j = layer // 4
        g_mix = mix_norm_g[layer].reshape(1, d)
        if kind == 0:
            w_main, w_gate = _split_w_in(ml_w_in[j], ml_w_in.shape[2] - 2 * ML_HEADS)
            main, gates = _in_proj(h, g_mix, w_main, w_gate)
            a = _mlstm_core(main.reshape(b, s, -1), gates.reshape(b, s, LANES), ml_b_if[j], ml_norm_g[j])
            w_out = ml_w_out[j]
        elif kind == 1:
            nq = SB_HEADS * SB_DH
            w_sb = jnp.concatenate([sb_w_in[j][:, :nq] * (SB_DH ** -0.5 * LOG2E), sb_w_in[j][:, nq:]], axis=1)
            (main,) = _in_proj(h, g_mix, w_sb.astype(BF16))
            a = _sb_core(main.reshape(b, s, -1))
            w_out = sb_w_out[j]
        elif kind == 2:
            w_main, w_gate = _split_w_in(fx_w_in[j], fx_w_in.shape[2] - FX_HEADS)
            main, gates = _in_proj(h, g_mix, w_main, w_gate)
            main = main.reshape(b, s, -1)
            bound = _fox_logit_bound(fx_q_norm_g[j], fx_k_norm_g[j])
            bounded = bound <= FX_MAX_SHIFT
            qa, ka = _fox_prep(main, gates.reshape(b, s, LANES), fx_b_f[j], fx_q_norm_g[j], fx_k_norm_g[j],
                               jnp.where(bounded, bound, 0.0))
            a = _fox_core(main, qa, ka, bounded)
            w_out = fx_w_out[j]
        else:
            w_main, w_gate = _split_w_in(gla_w_in[j], gla_w_in.shape[2] - GLA_RANK)
            main, gates = _in_proj(h, g_mix, w_main, w_gate)
            a = _gla_core(main.reshape(b, s, -1), gates.reshape(b, s, LANES), gla_w_gate_up[j], gla_b_gate[j], gla_norm_g[j])
            w_out = gla_w_out[j]
        h = _out_ffn(h, a.reshape(b * s, -1), w_out.astype(BF16), ffn_norm_g[layer].reshape(1, d),
                     ffn_gate, ffn_up, ffn_down, layer)
    return h.reshape(b, s, d)
```

```python
import functools

import jax
import jax.numpy as jnp
import numpy as np
from jax import lax
from jax.experimental import pallas as pl
from jax.experimental.pallas import tpu as pltpu

F32 = jnp.float32
BF16 = jnp.bfloat16

D_MODEL = 1024
EPS = 1e-6
CHUNK = 64
ML_HEADS, ML_DQK, ML_DV = 8, 64, 128
SB_HEADS, SB_DH = 16, 64
FX_HEADS, FX_DH = 16, 64
GLA_HEADS, GLA_DK, GLA_DV = 4, 128, 256
GLA_RANK = 16
GLA_TAU = 16.0
GLA_SUB = 16
NEG = -1e30
LOG2E = 1.4426950408889634
FX_MAX_SHIFT = 60.0
SB_LOG2_ZERO = -150.0
SB_Z_CLAMP = 100.0
LANES = 128
VMEM_LIMIT = 56 * 1024 * 1024


def _cparams(sem):
    return pltpu.CompilerParams(dimension_semantics=sem, vmem_limit_bytes=VMEM_LIMIT)


def _resident(shape):
    return pl.BlockSpec(shape, lambda i: (0,) * len(shape), pipeline_mode=pl.Buffered(1))


def _log_sigmoid(x):
    return jnp.minimum(x, 0.0) - jnp.log(1.0 + jnp.exp(-jnp.abs(x)))


def _softplus(x):
    return jnp.maximum(x, 0.0) + jnp.log(1.0 + jnp.exp(-jnp.abs(x)))


def _sigmoid(x):
    return 1.0 / (1.0 + jnp.exp(-x))


def _split_bf16(x):
    hi = x.astype(BF16)
    lo = (x - hi.astype(F32)).astype(BF16)
    return hi, lo


def _dot(a, b):
    return jnp.dot(a, b, preferred_element_type=F32)


def _dot_nt(a, b):
    return lax.dot_general(a, b, (((1,), (1,)), ((), ())), preferred_element_type=F32)


def _dot_tn(a, b):
    return lax.dot_general(a, b, (((0,), (0,)), ((), ())), preferred_element_type=F32)


def _tri_dot(tri, x):
    hi, lo = _split_bf16(x)
    return _dot(tri, hi) + _dot(tri, lo)


def _dot_tri(x, tri):
    hi, lo = _split_bf16(x)
    return _dot(hi, tri) + _dot(lo, tri)


def _in_proj_kernel(x_ref, g_ref, w_ref, *rest, tr, tn):
    wg_ref, main_ref, gate_ref = rest if len(rest) == 3 else (None, rest[0], None)

    def normed(rows):
        x = x_ref[rows, :]
        ms = jnp.mean(x * x, axis=-1, keepdims=True)
        return (x * lax.rsqrt(ms + EPS) * g_ref[...]).astype(BF16)

    blocks = [slice(r * tr, (r + 1) * tr) for r in range(x_ref.shape[0] // tr)]
    u = normed(blocks[0])
    for r, rows in enumerate(blocks):
        u_next = None
        for c in range(main_ref.shape[-1] // tn):
            cs = slice(c * tn, (c + 1) * tn)
            main_ref[rows, cs] = _dot(u, w_ref[:, cs]).astype(main_ref.dtype)
            if c == 0 and r + 1 < len(blocks):
                u_next = normed(blocks[r + 1])
        if gate_ref is not None:
            gate_ref[rows, :] = _dot(u, wg_ref[...])
        u = u_next


def _in_proj(x, g, w_in, wn, w_gate=None, *, tm=1024, tr=256, tn=512):
    n, d = x.shape
    in_specs = [pl.BlockSpec((tm, d), lambda i: (i, 0)), _resident((1, d)), _resident(w_in.shape)]
    out_specs = [pl.BlockSpec((tm, wn), lambda i: (i, 0))]
    out_shape = [jax.ShapeDtypeStruct((n, wn), BF16)]
    operands = [x, g, w_in]
    if w_gate is not None:
        in_specs.append(_resident((d, LANES)))
        out_specs.append(pl.BlockSpec((tm, LANES), lambda i: (i, 0)))
        out_shape.append(jax.ShapeDtypeStruct((n, LANES), F32))
        operands.append(w_gate)
    return pl.pallas_call(
        functools.partial(_in_proj_kernel, tr=tr, tn=tn),
        grid=(n // tm,),
        in_specs=in_specs,
        out_specs=out_specs,
        out_shape=out_shape,
        compiler_params=_cparams(("parallel",)),
        name="in_proj",
    )(*operands)


def _out_ffn_kernel(h_ref, a_ref, wo_ref, g_ref, wg_ref, wu_ref, wd_ref, o_ref, *, tf):
    h1 = h_ref[...] + _dot(a_ref[...], wo_ref[...])
    ms = jnp.mean(h1 * h1, axis=-1, keepdims=True)
    u = (h1 * lax.rsqrt(ms + EPS) * g_ref[...]).astype(BF16)
    acc = h1
    for c in range(wg_ref.shape[1] // tf):
        cs = slice(c * tf, (c + 1) * tf)
        gt = _dot(u, wg_ref[:, cs])
        up = _dot(u, wu_ref[:, cs])
        hid = (gt * _sigmoid(gt) * up).astype(BF16)
        acc = acc + _dot(hid, wd_ref[cs, :])
    o_ref[...] = acc


def _out_ffn(h, a, w_out, g, w_gate, w_up, w_down, layer, *, tm=512, tf=256):
    n, d = h.shape
    dff = w_gate.shape[2]

    def slab(rows, cols):
        return pl.BlockSpec((None, rows, cols), lambda i: (layer, 0, 0), pipeline_mode=pl.Buffered(1))

    return pl.pallas_call(
        functools.partial(_out_ffn_kernel, tf=tf),
        grid=(n // tm,),
        in_specs=[
            pl.BlockSpec((tm, d), lambda i: (i, 0)),
            pl.BlockSpec((tm, d), lambda i: (i, 0)),
            _resident((d, d)),
            _resident((1, d)),
            slab(d, dff),
            slab(d, dff),
            slab(dff, d),
        ],
        out_specs=pl.BlockSpec((tm, d), lambda i: (i, 0)),
        out_shape=jax.ShapeDtypeStruct((n, d), F32),
        compiler_params=_cparams(("parallel",)),
        name="out_ffn",
    )(h, a, w_out, g, w_gate, w_up, w_down)


def _mlstm_kernel(main_ref, gc_ref, gr_ref, bc_ref, br_ref, ng_ref, o_ref, c_s, m_s, *, nb):
    L = CHUNK
    H = ML_HEADS

    @pl.when(pl.program_id(1) == 0)
    def _():
        c_s[...] = jnp.zeros_like(c_s)
        m_s[...] = jnp.zeros_like(m_s)

    staged = [_mlstm_gates(bb, main_ref, gc_ref, gr_ref, bc_ref, br_ref, c_s) for bb in range(nb)]
    for bb in range(nb):
        _mlstm_heads(bb, staged[bb], main_ref, ng_ref, o_ref, c_s, m_s)


def _mlstm_gates(bb, main_ref, gc_ref, gr_ref, bc_ref, br_ref, c_s):
    L = CHUNK
    H = ML_HEADS
    row = lax.broadcasted_iota(jnp.int32, (L, L), 0)
    col = lax.broadcasted_iota(jnp.int32, (L, L), 1)
    tri = (col <= row).astype(BF16)
    tri_t = (row <= col).astype(BF16)

    gc = gc_ref[bb] + bc_ref[...]
    gr = gr_ref[bb, 0] + br_ref[...]
    cum_c = _tri_dot(tri, _log_sigmoid(gc))
    li_r = gr[0:H]
    cum_r = _dot_tri(_log_sigmoid(gr[H:2 * H]), tri_t)

    def lane_replicate(x, base):
        gate = lax.broadcasted_iota(jnp.int32, (LANES, H * LANES), 0)
        blk = lax.broadcasted_iota(jnp.int32, (LANES, H * LANES), 1) >> 7
        return _dot_tri(x, (gate == blk + base).astype(BF16))

    cum_b_all = lane_replicate(cum_c, H)
    li_b_all = lane_replicate(gc, 0)
    nq = H * ML_DQK
    qs = [main_ref[bb, :, h * ML_DQK:(h + 1) * ML_DQK] for h in range(H)]
    ks = [main_ref[bb, :, nq + h * ML_DQK:nq + (h + 1) * ML_DQK] for h in range(H)]
    s_qks = [_dot_nt(qs[h], ks[h]) for h in range(H)]
    inters = [_dot(qs[h], c_s[bb * H + h].astype(BF16)) for h in range(H)]
    return cum_r, li_r, cum_b_all, li_b_all, ks, s_qks, inters


def _mlstm_heads(bb, staged, main_ref, ng_ref, o_ref, c_s, m_s):
    L = CHUNK
    H = ML_HEADS
    scale = ML_DQK ** -0.5
    nq = H * ML_DQK
    nv = H * ML_DV
    cum_r, li_r, cum_b_all, li_b_all, ks, s_qks, inters = staged
    causal = lax.broadcasted_iota(jnp.int32, (L, L), 1) <= lax.broadcasted_iota(jnp.int32, (L, L), 0)
    ones_blk = jnp.ones((L, ML_DV), BF16)

    for h in range(H):
        v = main_ref[bb, :, 2 * nq + h * ML_DV:2 * nq + (h + 1) * ML_DV]
        op = main_ref[bb, :, 2 * nq + nv + h * ML_DV:2 * nq + nv + (h + 1) * ML_DV]
        vext = jnp.concatenate([v, ones_blk], axis=1)
        cum_b = cum_b_all[:, h * LANES:(h + 1) * LANES]
        li_b = li_b_all[:, h * LANES:(h + 1) * LANES]
        cum_r1 = cum_r[h:h + 1, :]
        li_r1 = li_r[h:h + 1, :]
        m_st = m_s[bb * H + h][:, 0:1]
        cext = c_s[bb * H + h]

        log_d = jnp.where(causal, cum_b[:, :L] - cum_r1 + li_r1, NEG)
        m_t = jnp.maximum(cum_b + m_st, jnp.max(log_d, axis=-1, keepdims=True))
        dmat = jnp.exp(log_d - m_t[:, :L])
        w_inter = jnp.exp(cum_b + m_st - m_t)
        sc = (s_qks[h] * (dmat * scale)).astype(BF16)
        inter = inters[h]
        intra = _dot(sc, vext)
        num = w_inter * inter[:, :ML_DV] + intra[:, :ML_DV]
        den = w_inter * inter[:, ML_DV:] + intra[:, ML_DV:]
        hh = num / jnp.maximum(jnp.abs(den), jnp.exp(-m_t))
        hn = hh * lax.rsqrt(jnp.mean(hh * hh, axis=-1, keepdims=True) + EPS)
        out = hn * ng_ref[:, h * ML_DV:(h + 1) * ML_DV] * _sigmoid(op.astype(F32))
        o_ref[bb, :, h * ML_DV:(h + 1) * ML_DV] = out.astype(o_ref.dtype)

        tot_f = cum_r1[:, L - 1:L]
        log_w_r = tot_f - cum_r1 + li_r1
        m_new = jnp.maximum(tot_f + m_st, jnp.max(log_w_r, axis=-1, keepdims=True))
        decay = jnp.exp(tot_f + m_st - m_new)
        w_b = jnp.exp(tot_f - cum_b + li_b - m_new) * scale
        wk = (ks[h].astype(F32) * w_b[:, :ML_DQK]).astype(BF16)
        c_s[bb * H + h] = decay * cext + _dot_tn(wk, vext)
        m_s[bb * H + h] = jnp.broadcast_to(m_new, (1, LANES))


def _mlstm_core(main, gates, b_if, norm_g, *, nb=2):
    b, s, _ = main.shape
    nc = s // CHUNK
    h = ML_HEADS
    gr = gates[:, :, :2 * h].reshape(b, nc, CHUNK, 2 * h).transpose(0, 1, 3, 2)
    bc = jnp.zeros((1, LANES), F32).at[0, :2 * h].set(b_if)
    br = b_if.reshape(2 * h, 1)
    return pl.pallas_call(
        functools.partial(_mlstm_kernel, nb=nb),
        grid=(b // nb, nc),
        in_specs=[
            pl.BlockSpec((nb, CHUNK, main.shape[2]), lambda i, c: (i, c, 0)),
            pl.BlockSpec((nb, CHUNK, LANES), lambda i, c: (i, c, 0)),
            pl.BlockSpec((nb, 1, 2 * h, CHUNK), lambda i, c: (i, c, 0, 0)),
            pl.BlockSpec((1, LANES), lambda i, c: (0, 0)),
            pl.BlockSpec((2 * h, 1), lambda i, c: (0, 0)),
            pl.BlockSpec((1, h * ML_DV), lambda i, c: (0, 0)),
        ],
        out_specs=pl.BlockSpec((nb, CHUNK, h * ML_DV), lambda i, c: (i, c, 0)),
        out_shape=jax.ShapeDtypeStruct((b, s, h * ML_DV), BF16),
        scratch_shapes=[
            pltpu.VMEM((nb * h, ML_DQK, 2 * ML_DV), F32),
            pltpu.VMEM((nb * h, 1, LANES), F32),
        ],
        compiler_params=_cparams(("parallel", "arbitrary")),
        name="mlstm_core",
    )(main, gates, gr, bc, br, norm_g.reshape(1, -1))


def _sb_kernel(q_ref, k_ref, v_ref, o_ref, acc_s, carry_s, *, t, heads):
    qi = pl.program_id(2)
    lane = lax.broadcasted_iota(jnp.int32, (1, LANES), 1)
    lo = lane < SB_DH
    sels = (lo, jnp.logical_not(lo))
    row = lax.broadcasted_iota(jnp.int32, (t, t), 0)
    col = lax.broadcasted_iota(jnp.int32, (t, t), 1)
    valid = col < row
    lower = (row > col).astype(BF16)
    groups = [slice(g * LANES, (g + 1) * LANES) for g in range(heads // 2)]
    zero = jnp.zeros((), BF16)
    qms = [jnp.where(sels[h % 2], q_ref[0, :, groups[h // 2]], zero) for h in range(heads)]

    def tiles(kts, carries, from_diagonal):
        hs = range(heads)
        k0s = [pl.multiple_of(kt * t, t) for kt in kts]
        masks = [from_diagonal and i == 0 for i in range(len(kts))]
        zs = [[_dot_nt(qms[h], k_ref[0, pl.ds(k0, t), groups[h // 2]]) for h in hs] for k0 in k0s]
        nlps = []
        for zt, masked in zip(zs, masks):
            row_nlps = []
            for h in hs:
                nlp = jnp.maximum(jnp.log2(1.0 + jnp.exp2(jnp.minimum(zt[h], SB_Z_CLAMP))), zt[h])
                row_nlps.append(jnp.where(valid, nlp, 0.0) if masked else nlp)
            nlps.append(row_nlps)
        laters = [[_dot(nt[h].astype(BF16), lower) for h in hs] for nt in nlps]
        for zt, nt, lt, masked, k0 in zip(zs, nlps, laters, masks, k0s):
            weights = []
            for h in hs:
                a = jnp.exp2(zt[h] - nt[h] - lt[h] + carries[h])
                weights.append((jnp.where(valid, a, 0.0) if masked else a).astype(BF16))
            for g, lanes in enumerate(groups):
                vv = v_ref[0, pl.ds(k0, t), lanes]
                contrib = (_dot(weights[2 * g], jnp.where(sels[0], vv, zero))
                           + _dot(weights[2 * g + 1], jnp.where(sels[1], vv, zero)))
                if masked:
                    acc_s[:, lanes] = contrib
                else:
                    acc_s[:, lanes] += contrib
            carries = tuple(carries[h] - jnp.sum(nt[h], axis=-1, keepdims=True) for h in hs)
        return carries

    zero_carries = (jnp.zeros((t, 1), F32),) * heads

    @pl.when(qi > 0)
    def _():
        for h, c in enumerate(tiles([qi, qi - 1], zero_carries, True)):
            carry_s[h] = c

    @pl.when(qi == 0)
    def _():
        for h, c in enumerate(tiles([qi], zero_carries, True)):
            carry_s[h] = c

    carries = tuple(carry_s[h] for h in range(heads))

    def live(carries):
        return functools.reduce(jnp.maximum, [jnp.max(c) for c in carries])

    def cond(st):
        return jnp.logical_and(st[0] >= 0, st[1] > SB_LOG2_ZERO)

    def body(st):
        carries = tiles([st[0]], st[2], False)
        return st[0] - 1, live(carries), carries

    lax.while_loop(cond, body, (qi - 2, live(carries), carries))
    o_ref[0] = acc_s[...].astype(o_ref.dtype)


def _sb_core(main, *, t=256, heads=4):
    b, s, _ = main.shape
    ng = SB_HEADS // heads
    w = heads * SB_DH
    return pl.pallas_call(
        functools.partial(_sb_kernel, t=t, heads=heads),
        grid=(b, ng, s // t),
        in_specs=[
            pl.BlockSpec((1, t, w), lambda i, p, j: (i, j, p)),
            pl.BlockSpec((1, s, w), lambda i, p, j: (i, 0, ng + p)),
            pl.BlockSpec((1, s, w), lambda i, p, j: (i, 0, 2 * ng + p)),
        ],
        out_specs=pl.BlockSpec((1, t, w), lambda i, p, j: (i, j, p)),
        out_shape=jax.ShapeDtypeStruct((b, s, SB_HEADS * SB_DH), BF16),
        scratch_shapes=[pltpu.VMEM((t, w), F32), pltpu.VMEM((heads, t, 1), F32)],
        compiler_params=_cparams(("parallel", "parallel", "arbitrary")),
        name="sb_core",
    )(main, main, main)


def _fox_scatter_matrices():
    pq = np.zeros((FX_HEADS // 2, LANES, 2 * LANES), np.float32)
    pk = np.zeros((FX_HEADS // 2, LANES, 2 * LANES), np.float32)
    for p in range(FX_HEADS // 2):
        for hh, base in ((0, FX_DH), (1, LANES)):
            head = 2 * p + hh
            for i in range(3):
                pq[p, 16 * i + head, base + i] = 1.0
                pq[p, 48, base + 3 + i] = 1.0
                pk[p, 48, base + i] = 1.0
                pk[p, 16 * i + head, base + 3 + i] = -1.0
    return jnp.asarray(pq, BF16), jnp.asarray(pk, BF16)


def _fox_prep_kernel(main_ref, fp_ref, bf_ref, gq_ref, gk_ref, sh_ref, pq_ref, pk_ref, qa_ref, ka_ref,
                     carry_s, *, ts):
    @pl.when(pl.program_id(1) == 0)
    def _():
        carry_s[...] = jnp.zeros_like(carry_s)

    row = lax.broadcasted_iota(jnp.int32, (ts, ts), 0)
    col = lax.broadcasted_iota(jnp.int32, (ts, ts), 1)
    tri = (col <= row).astype(BF16)
    lf = _log_sigmoid(fp_ref[0] + bf_ref[...])
    cum = _tri_dot(tri, lf) + carry_s[...]
    carry_s[...] = cum[ts - 1:ts, :]

    lane = lax.broadcasted_iota(jnp.int32, (1, LANES), 1)
    lo = lane < FX_DH
    head_lane = lane < FX_HEADS

    def packed(c):
        c = jnp.where(head_lane, c, 0.0)
        hi = c.astype(BF16).astype(F32)
        mid = (c - hi).astype(BF16).astype(F32)
        low = (c - hi - mid).astype(BF16).astype(F32)
        out = hi + pltpu.roll(mid, 16, axis=1) + pltpu.roll(low, 32, axis=1)
        return jnp.where(lane == 48, 1.0, out).astype(BF16)

    pack_q = packed(cum * LOG2E - sh_ref[...])
    pack_k = packed(cum * LOG2E)

    n_q = FX_HEADS * FX_DH // LANES
    for c in range(2 * n_q):
        is_q = c < n_q
        p = c % n_q
        xc = main_ref[0, :, c * LANES:(c + 1) * LANES].astype(F32)
        x2 = xc * xc
        s_lo = jnp.sum(jnp.where(lo, x2, 0.0), axis=-1, keepdims=True)
        s_hi = jnp.sum(jnp.where(lo, 0.0, x2), axis=-1, keepdims=True)
        r = jnp.where(lo, lax.rsqrt(s_lo / FX_DH + EPS), lax.rsqrt(s_hi / FX_DH + EPS))
        xn = xc * r * (gq_ref[...] if is_q else gk_ref[...])
        aug = _dot(pack_q, pq_ref[p]) if is_q else _dot(pack_k, pk_ref[p])
        out_ref = qa_ref if is_q else ka_ref
        out_ref[0, 2 * p] = jnp.where(lo, xn, aug[:, :LANES]).astype(out_ref.dtype)
        out_ref[0, 2 * p + 1] = jnp.where(lo, aug[:, LANES:], xn).astype(out_ref.dtype)


def _fox_prep(main, f_pre, b_f, q_norm_g, k_norm_g, shift, *, ts=256):
    b, s, _ = main.shape
    w = 2 * FX_HEADS * FX_DH
    bf = jnp.zeros((1, LANES), F32).at[0, :FX_HEADS].set(b_f)
    sh = jnp.full((1, LANES), shift, F32)
    gq = jnp.tile(q_norm_g, 2).reshape(1, LANES) * (FX_DH ** -0.5 * LOG2E)
    gk = jnp.tile(k_norm_g, 2).reshape(1, LANES)
    pq, pk = _fox_scatter_matrices()
    head_rows = pl.BlockSpec((1, FX_HEADS, ts, LANES), lambda i, j: (i, 0, j, 0))
    const_row = pl.BlockSpec((1, LANES), lambda i, j: (0, 0))
    scatter = pl.BlockSpec(pq.shape, lambda i, j: (0, 0, 0))
    return pl.pallas_call(
        functools.partial(_fox_prep_kernel, ts=ts),
        grid=(b, s // ts),
        in_specs=[
            pl.BlockSpec((1, ts, w), lambda i, j: (i, j, 0)),
            pl.BlockSpec((1, ts, LANES), lambda i, j: (i, j, 0)),
            const_row, const_row, const_row, const_row, scatter, scatter,
        ],
        out_specs=[head_rows, head_rows],
        out_shape=[jax.ShapeDtypeStruct((b, FX_HEADS, s, LANES), BF16)] * 2,
        scratch_shapes=[pltpu.VMEM((1, LANES), F32)],
        compiler_params=_cparams(("parallel", "arbitrary")),
        name="fox_prep",
    )(main, f_pre, bf, gq, gk, sh, pq, pk)


def _fox_kernel(bounded_ref, q_ref, k_ref, v_ref, op_ref, o_ref, acc_s, *, t):
    qi = pl.program_id(2)
    lane = lax.broadcasted_iota(jnp.int32, (1, LANES), 1)
    lo = lane < FX_DH
    sels = (lo, jnp.logical_not(lo))
    row = lax.broadcasted_iota(jnp.int32, (t, t), 0)
    col = lax.broadcasted_iota(jnp.int32, (t, t), 1)
    causal = col <= row
    qs = (q_ref[0, 0], q_ref[0, 1])
    bounded = bounded_ref[0] != 0

    def scores(hh, k0, width, masked):
        s = _dot_nt(qs[hh], k_ref[0, hh, pl.ds(k0, width), :])
        return jnp.where(causal, s, NEG) if masked else s

    def vext(hh, k0, width):
        return jnp.where(sels[hh], v_ref[0, pl.ds(k0, width), :], jnp.ones((), BF16))

    def plain_tile(k0, width):
        ss = [scores(hh, k0, width, False) for hh in range(2)]
        ps = [jnp.exp2(s).astype(BF16) for s in ss]
        for hh in range(2):
            acc_s[hh] += _dot(ps[hh], vext(hh, k0, width))

    def plain_diagonal(k0, k0_prev=None):
        half = t // 2
        parts = [(hh, r0, width) for r0, width in ((0, half), (half, t)) for hh in range(2)]
        ss = [_dot_nt(qs[hh][r0:r0 + half], k_ref[0, hh, pl.ds(k0, width), :]) for hh, r0, width in parts]
        ss_prev = [] if k0_prev is None else [scores(hh, k0_prev, t, False) for hh in range(2)]
        ps = []
        for s, (hh, r0, width) in zip(ss, parts):
            keep = (lax.broadcasted_iota(jnp.int32, (half, width), 1)
                    <= lax.broadcasted_iota(jnp.int32, (half, width), 0) + r0)
            ps.append(jnp.exp2(jnp.where(keep, s, NEG)).astype(BF16))
        ps_prev = [jnp.exp2(s).astype(BF16) for s in ss_prev]
        for p, (hh, r0, width) in zip(ps, parts):
            acc_s[hh, r0:r0 + half] = _dot(p, vext(hh, k0, width))
        for hh, p in enumerate(ps_prev):
            acc_s[hh] += _dot(p, vext(hh, k0_prev, t))

    def online_tile(kt, ms, masked):
        k0 = pl.multiple_of(kt * t, t)
        new_ms = []
        for hh in range(2):
            s = scores(hh, k0, t, masked)
            m_new = jnp.maximum(ms[hh], jnp.max(s, axis=-1, keepdims=True))
            alpha = jnp.exp2(ms[hh] - m_new)
            p = jnp.exp2(s - m_new).astype(BF16)
            acc_s[hh] = alpha * acc_s[hh] + _dot(p, vext(hh, k0, t))
            new_ms.append(m_new)
        return tuple(new_ms)

    @pl.when(bounded)
    def _():
        @pl.when(qi > 0)
        def _():
            plain_diagonal(pl.multiple_of(qi * t, t), pl.multiple_of((qi - 1) * t, t))

        @pl.when(qi == 0)
        def _():
            plain_diagonal(pl.multiple_of(qi * t, t))

        def step(kt, carry):
            plain_tile(pl.multiple_of(kt * t, t), t)
            return carry

        lax.fori_loop(0, qi - 1, step, 0)

    @pl.when(jnp.logical_not(bounded))
    def _():
        acc_s[...] = jnp.zeros_like(acc_s)
        m0 = jnp.full((t, 1), NEG, F32)
        ms = lax.fori_loop(0, qi, lambda kt, ms: online_tile(kt, ms, False), (m0, m0))
        online_tile(qi, ms, True)

    acc0 = acc_s[0]
    acc1 = acc_s[1]
    num = jnp.where(lo, acc0, acc1)
    den = pltpu.roll(jnp.where(lo, acc1, acc0), FX_DH, axis=1)
    o_ref[0] = (num / den * _sigmoid(op_ref[0].astype(F32))).astype(o_ref.dtype)


def _fox_logit_bound(q_norm_g, k_norm_g):
    return (FX_DH ** 0.5 * LOG2E) * jnp.max(jnp.abs(q_norm_g)) * jnp.max(jnp.abs(k_norm_g))


def _fox_core(main, qa, ka, bounded, *, t=512):
    b, s, _ = main.shape
    hp = FX_HEADS // 2
    return pl.pallas_call(
        functools.partial(_fox_kernel, t=t),
        grid=(b, hp, s // t),
        in_specs=[
            pl.BlockSpec(memory_space=pltpu.SMEM),
            pl.BlockSpec((1, 2, t, LANES), lambda i, p, j: (i, p, j, 0)),
            pl.BlockSpec((1, 2, s, LANES), lambda i, p, j: (i, p, 0, 0)),
            pl.BlockSpec((1, s, LANES), lambda i, p, j: (i, 0, 2 * hp + p)),
            pl.BlockSpec((1, t, LANES), lambda i, p, j: (i, j, 3 * hp + p)),
        ],
        out_specs=pl.BlockSpec((1, t, LANES), lambda i, p, j: (i, j, p)),
        out_shape=jax.ShapeDtypeStruct((b, s, FX_HEADS * FX_DH), BF16),
        scratch_shapes=[pltpu.VMEM((2, t, LANES), F32)],
        compiler_params=_cparams(("parallel", "parallel", "arbitrary")),
        name="fox_core",
    )(bounded.astype(jnp.int32).reshape(1), qa, ka, main, main)


def _gla_kernel(main_ref, gl_ref, wgu_ref, bg_ref, ng_ref, o_ref, st_s, cum_s, k_s, *, nb):
    L = CHUNK
    H = GLA_HEADS
    SUB = GLA_SUB
    nk = H * GLA_DK
    nv = H * GLA_DV
    scale = GLA_DK ** -0.5

    @pl.when(pl.program_id(1) == 0)
    def _():
        st_s[...] = jnp.zeros_like(st_s)

    row = lax.broadcasted_iota(jnp.int32, (L, L), 0)
    col = lax.broadcasted_iota(jnp.int32, (L, L), 1)
    tri = (col <= row).astype(BF16)
    eye = (lax.broadcasted_iota(jnp.int32, (GLA_DK, GLA_DK), 0)
           == lax.broadcasted_iota(jnp.int32, (GLA_DK, GLA_DK), 1))

    w_hi, w_lo = _split_bf16(wgu_ref[...])
    cum_alls = []
    for bb in range(nb):
        g_hi, g_lo = _split_bf16(gl_ref[bb])
        logits = _dot(g_hi, w_hi) + _dot(g_hi, w_lo) + _dot(g_lo, w_hi)
        la = _log_sigmoid(logits + bg_ref[...]) * (1.0 / GLA_TAU)
        cum_alls.append(_tri_dot(tri, la))

    units = [(bb, h) for bb in range(nb) for h in range(H)]
    nsub = L // SUB
    half = SUB // 2
    qs, v_bfs, cums, decays = [], [], [], []

    inters, updates, a_offs = [], [], []
    for u, (bb, h) in enumerate(units):
        ks = slice(h * GLA_DK, (h + 1) * GLA_DK)
        cum = cum_alls[bb][:, ks]
        tot = cum[L - 1:L, :]
        q = main_ref[bb, :, ks].astype(F32) * scale
        k = main_ref[bb, :, nk + h * GLA_DK:nk + (h + 1) * GLA_DK].astype(F32)
        v_bf = main_ref[bb, :, 2 * nk + h * GLA_DV:2 * nk + (h + 1) * GLA_DV]
        inters.append(_dot((q * jnp.exp(cum)).astype(BF16), st_s[u].astype(BF16)))
        updates.append(_dot_tn((k * jnp.exp(tot - cum)).astype(BF16), v_bf))
        offs = []
        for i in range(1, nsub):
            rs = slice(i * SUB, (i + 1) * SUB)
            ref_pt = cum[i * SUB - 1:i * SUB, :]
            q_rel = (q[rs] * jnp.exp(cum[rs] - ref_pt)).astype(BF16)
            k_rel = (k[:i * SUB] * jnp.exp(ref_pt - cum[:i * SUB])).astype(BF16)
            offs.append(_dot_nt(q_rel, k_rel))
        a_offs.append(offs)
        tot_col = jnp.sum(jnp.where(eye, jnp.broadcast_to(tot, (GLA_DK, GLA_DK)), 0.0),
                          axis=-1, keepdims=True)
        decays.append(jnp.exp(tot_col))
        cum_s[u] = cum
        k_s[u] = k
        qs.append(q)
        v_bfs.append(v_bf)
        cums.append(cum)

    h_offs = [[_dot(a_offs[u][i - 1].astype(BF16), v_bfs[u][:i * SUB]) for i in range(1, nsub)]
              for u in range(len(units))]

    t_half = lax.broadcasted_iota(jnp.int32, (half, 1), 0)
    lane = lax.broadcasted_iota(jnp.int32, (1, LANES), 1)
    for u, (bb, h) in enumerate(units):
        vs = slice(h * GLA_DV, (h + 1) * GLA_DV)
        for i in range(nsub):
            rs = slice(i * SUB, (i + 1) * SUB)
            q_top, q_bot = qs[u][i * SUB:i * SUB + half], qs[u][i * SUB + half:(i + 1) * SUB]
            c_top, c_bot = cums[u][i * SUB:i * SUB + half], cums[u][i * SUB + half:(i + 1) * SUB]
            a_top = jnp.zeros((half, LANES), F32)
            a_bot = jnp.zeros((half, LANES), F32)
            for j in range(SUB):
                rj = i * SUB + j
                kj = k_s[u, rj:rj + 1, :]
                cj = cum_s[u, rj:rj + 1, :]
                if j < half:
                    e_top = jnp.exp(jnp.where(t_half >= j, c_top - cj, NEG))
                    a_top = jnp.where(lane == j, jnp.sum(q_top * kj * e_top, axis=-1, keepdims=True), a_top)
                    e_bot = jnp.exp(c_bot - cj)
                else:
                    e_bot = jnp.exp(jnp.where(t_half >= j - half, c_bot - cj, NEG))
                a_bot = jnp.where(lane == j, jnp.sum(q_bot * kj * e_bot, axis=-1, keepdims=True), a_bot)
            a_diag = jnp.concatenate([a_top, a_bot], axis=0)[:, :SUB].astype(BF16)
            hb = inters[u][rs] + _dot(a_diag, v_bfs[u][rs])
            if i > 0:
                hb = hb + h_offs[u][i - 1]
            hn = hb * lax.rsqrt(jnp.mean(hb * hb, axis=-1, keepdims=True) + EPS)
            r = main_ref[bb, rs, 2 * nk + nv + h * GLA_DV:2 * nk + nv + (h + 1) * GLA_DV].astype(F32)
            out = hn * ng_ref[:, vs] * (r * _sigmoid(r))
            o_ref[bb, rs, vs] = out.astype(o_ref.dtype)

    for u in range(len(units)):
        st_s[u] = decays[u] * st_s[u] + updates[u]


def _gla_core(main, g_low, w_gate_up, b_gate, norm_g, *, nb=2):
    b, s, _ = main.shape
    nc = s // CHUNK
    nk = GLA_HEADS * GLA_DK
    nv = GLA_HEADS * GLA_DV
    wgu = jnp.zeros((LANES, nk), F32).at[:GLA_RANK].set(w_gate_up)
    return pl.pallas_call(
        functools.partial(_gla_kernel, nb=nb),
        grid=(b // nb, nc),
        in_specs=[
            pl.BlockSpec((nb, CHUNK, main.shape[2]), lambda i, c: (i, c, 0)),
            pl.BlockSpec((nb, CHUNK, LANES), lambda i, c: (i, c, 0)),
            pl.BlockSpec((LANES, nk), lambda i, c: (0, 0)),
            pl.BlockSpec((1, nk), lambda i, c: (0, 0)),
            pl.BlockSpec((1, nv), lambda i, c: (0, 0)),
        ],
        out_specs=pl.BlockSpec((nb, CHUNK, nv), lambda i, c: (i, c, 0)),
        out_shape=jax.ShapeDtypeStruct((b, s, nv), BF16),
        scratch_shapes=[
            pltpu.VMEM((nb * GLA_HEADS, GLA_DK, GLA_DV), F32),
            pltpu.VMEM((nb * GLA_HEADS, CHUNK, GLA_DK), F32),
            pltpu.VMEM((nb * GLA_HEADS, CHUNK, GLA_DK), F32),
        ],
        compiler_params=_cparams(("parallel", "arbitrary")),
        name="gla_core",
    )(main, g_low, wgu, b_gate.reshape(1, -1), norm_g.reshape(1, -1))


def _gated_in_proj(x, g, w_in, n_gate):
    n_main = w_in.shape[1] - n_gate
    w_bf = w_in.astype(BF16)
    w_gate = jnp.pad(w_bf[:, n_main:], ((0, 0), (0, LANES - n_gate)))
    return _in_proj(x, g, w_bf, n_main, w_gate)


def kernel(x, mix_norm_g, ffn_norm_g, ffn_w_gate, ffn_w_up, ffn_w_down, ml_w_in, ml_b_if, ml_norm_g, ml_w_out, sb_w_in, sb_w_out, fx_w_in, fx_b_f, fx_q_norm_g, fx_k_norm_g, fx_w_out, gla_w_in, gla_w_gate_up, gla_b_gate, gla_norm_g, gla_w_out):
    b, s, d = x.shape
    depth = mix_norm_g.shape[0]
    h = x.reshape(b * s, d)
    ffn_gate, ffn_up, ffn_down = (w.astype(BF16) for w in (ffn_w_gate, ffn_w_up, ffn_w_down))
    for layer in range(depth):
        kind = layer % 4
        j = layer // 4
        g_mix = mix_norm_g[layer].reshape(1, d)
        if kind == 0:
            main, gates = _gated_in_proj(h, g_mix, ml_w_in[j], 2 * ML_HEADS)
            a = _mlstm_core(main.reshape(b, s, -1), gates.reshape(b, s, LANES), ml_b_if[j], ml_norm_g[j])
            w_out = ml_w_out[j]
        elif kind == 1:
            nq = SB_HEADS * SB_DH
            w_sb = jnp.concatenate([sb_w_in[j][:, :nq] * (SB_DH ** -0.5 * LOG2E), sb_w_in[j][:, nq:]], axis=1)
            (main,) = _in_proj(h, g_mix, w_sb.astype(BF16), w_sb.shape[1])
            a = _sb_core(main.reshape(b, s, -1))
            w_out = sb_w_out[j]
        elif kind == 2:
            main, gates = _gated_in_proj(h, g_mix, fx_w_in[j], FX_HEADS)
            main = main.reshape(b, s, -1)
            bound = _fox_logit_bound(fx_q_norm_g[j], fx_k_norm_g[j])
            bounded = bound <= FX_MAX_SHIFT
            qa, ka = _fox_prep(main, gates.reshape(b, s, LANES), fx_b_f[j], fx_q_norm_g[j], fx_k_norm_g[j],
                               jnp.where(bounded, bound, 0.0))
            a = _fox_core(main, qa, ka, bounded)
            w_out = fx_w_out[j]
        else:
            main, gates = _gated_in_proj(h, g_mix, gla_w_in[j], GLA_RANK)
            a = _gla_core(main.reshape(b, s, -1), gates.reshape(b, s, LANES), gla_w_gate_up[j], gla_b_gate[j], gla_norm_g[j])
            w_out = gla_w_out[j]
        h = _out_ffn(h, a.reshape(b * s, -1), w_out.astype(BF16), ffn_norm_g[layer].reshape(1, d),
                     ffn_gate, ffn_up, ffn_down, layer)
    return h.reshape(b, s, d)
```

```python
import functools

import jax
import jax.numpy as jnp
import numpy as np
from jax import lax
from jax.experimental import pallas as pl
from jax.experimental.pallas import tpu as pltpu

F32 = jnp.float32
BF16 = jnp.bfloat16

EPS = 1e-6
CHUNK = 64
ML_HEADS, ML_DQK, ML_DV = 8, 64, 128
SB_HEADS, SB_DH = 16, 64
FX_HEADS, FX_DH = 16, 64
GLA_HEADS, GLA_DK, GLA_DV = 4, 128, 256
GLA_RANK = 16
GLA_TAU = 16.0
GLA_SUB = 16
NEG = -1e30
LOG2E = 1.4426950408889634
FX_MAX_SHIFT = 60.0
SB_LOG2_ZERO = -150.0
SB_Z_CLAMP = 100.0
LANES = 128
VMEM_LIMIT = 56 * 1024 * 1024


def _cparams(sem):
    return pltpu.CompilerParams(dimension_semantics=sem, vmem_limit_bytes=VMEM_LIMIT)


def _resident(shape):
    return pl.BlockSpec(shape, lambda i: (0,) * len(shape), pipeline_mode=pl.Buffered(1))


def _log_sigmoid(x):
    return jnp.minimum(x, 0.0) - jnp.log(1.0 + jnp.exp(-jnp.abs(x)))


def _sigmoid(x):
    return 1.0 / (1.0 + jnp.exp(-x))


def _split_bf16(x):
    hi = x.astype(BF16)
    lo = (x - hi.astype(F32)).astype(BF16)
    return hi, lo


def _dot(a, b):
    return jnp.dot(a, b, preferred_element_type=F32)


def _dot_nt(a, b):
    return lax.dot_general(a, b, (((1,), (1,)), ((), ())), preferred_element_type=F32)


def _dot_tn(a, b):
    return lax.dot_general(a, b, (((0,), (0,)), ((), ())), preferred_element_type=F32)


def _tri_dot(tri, x):
    hi, lo = _split_bf16(x)
    return _dot(tri, hi) + _dot(tri, lo)


def _dot_tri(x, tri):
    hi, lo = _split_bf16(x)
    return _dot(hi, tri) + _dot(lo, tri)


def _in_proj_kernel(x_ref, g_ref, w_ref, *rest, tr, tn):
    wg_ref, main_ref, gate_ref = rest if len(rest) == 3 else (None, rest[0], None)

    def normed(rows):
        x = x_ref[rows, :]
        ms = jnp.mean(x * x, axis=-1, keepdims=True)
        return (x * lax.rsqrt(ms + EPS) * g_ref[...]).astype(BF16)

    blocks = [slice(r * tr, (r + 1) * tr) for r in range(x_ref.shape[0] // tr)]
    u = normed(blocks[0])
    for r, rows in enumerate(blocks):
        u_next = None
        for c in range(main_ref.shape[-1] // tn):
            cs = slice(c * tn, (c + 1) * tn)
            main_ref[rows, cs] = _dot(u, w_ref[:, cs]).astype(main_ref.dtype)
            if c == 0 and r + 1 < len(blocks):
                u_next = normed(blocks[r + 1])
        if gate_ref is not None:
            gate_ref[rows, :] = _dot(u, wg_ref[...])
        u = u_next


def _in_proj(x, g, w_in, wn, w_gate=None, *, tm=1024, tr=256, tn=512):
    n, d = x.shape
    in_specs = [pl.BlockSpec((tm, d), lambda i: (i, 0)), _resident((1, d)), _resident(w_in.shape)]
    out_specs = [pl.BlockSpec((tm, wn), lambda i: (i, 0))]
    out_shape = [jax.ShapeDtypeStruct((n, wn), BF16)]
    operands = [x, g, w_in]
    if w_gate is not None:
        in_specs.append(_resident((d, LANES)))
        out_specs.append(pl.BlockSpec((tm, LANES), lambda i: (i, 0)))
        out_shape.append(jax.ShapeDtypeStruct((n, LANES), F32))
        operands.append(w_gate)
    return pl.pallas_call(
        functools.partial(_in_proj_kernel, tr=tr, tn=tn),
        grid=(n // tm,),
        in_specs=in_specs,
        out_specs=out_specs,
        out_shape=out_shape,
        compiler_params=_cparams(("parallel",)),
        name="in_proj",
    )(*operands)


def _out_ffn_kernel(h_ref, a_ref, wo_ref, g_ref, wg_ref, wu_ref, wd_ref, o_ref, *, tf):
    h1 = h_ref[...] + _dot(a_ref[...], wo_ref[...])
    ms = jnp.mean(h1 * h1, axis=-1, keepdims=True)
    u = (h1 * lax.rsqrt(ms + EPS) * g_ref[...]).astype(BF16)
    acc = h1
    for c in range(wg_ref.shape[1] // tf):
        cs = slice(c * tf, (c + 1) * tf)
        gt = _dot(u, wg_ref[:, cs])
        up = _dot(u, wu_ref[:, cs])
        hid = (gt * _sigmoid(gt) * up).astype(BF16)
        acc = acc + _dot(hid, wd_ref[cs, :])
    o_ref[...] = acc


def _out_ffn(h, a, w_out, g, w_gate, w_up, w_down, layer, *, tm=512, tf=256):
    n, d = h.shape
    dff = w_gate.shape[2]

    def slab(rows, cols):
        return pl.BlockSpec((None, rows, cols), lambda i: (layer, 0, 0), pipeline_mode=pl.Buffered(1))

    return pl.pallas_call(
        functools.partial(_out_ffn_kernel, tf=tf),
        grid=(n // tm,),
        in_specs=[
            pl.BlockSpec((tm, d), lambda i: (i, 0)),
            pl.BlockSpec((tm, d), lambda i: (i, 0)),
            _resident((d, d)),
            _resident((1, d)),
            slab(d, dff),
            slab(d, dff),
            slab(dff, d),
        ],
        out_specs=pl.BlockSpec((tm, d), lambda i: (i, 0)),
        out_shape=jax.ShapeDtypeStruct((n, d), F32),
        compiler_params=_cparams(("parallel",)),
        name="out_ffn",
    )(h, a, w_out, g, w_gate, w_up, w_down)


def _mlstm_kernel(main_ref, gc_ref, gr_ref, bc_ref, br_ref, ng_ref, o_ref, c_s, m_s, *, nb):
    @pl.when(pl.program_id(1) == 0)
    def _():
        c_s[...] = jnp.zeros_like(c_s)
        m_s[...] = jnp.zeros_like(m_s)

    staged = [_mlstm_gates(bb, main_ref, gc_ref, gr_ref, bc_ref, br_ref, c_s) for bb in range(nb)]
    for bb in range(nb):
        _mlstm_heads(bb, staged[bb], main_ref, ng_ref, o_ref, c_s, m_s)


def _mlstm_gates(bb, main_ref, gc_ref, gr_ref, bc_ref, br_ref, c_s):
    L = CHUNK
    H = ML_HEADS
    row = lax.broadcasted_iota(jnp.int32, (L, L), 0)
    col = lax.broadcasted_iota(jnp.int32, (L, L), 1)
    tri = (col <= row).astype(BF16)
    tri_t = (row <= col).astype(BF16)

    gc = gc_ref[bb] + bc_ref[...]
    gr = gr_ref[bb, 0] + br_ref[...]
    cum_c = _tri_dot(tri, _log_sigmoid(gc))
    li_r = gr[0:H]
    cum_r = _dot_tri(_log_sigmoid(gr[H:2 * H]), tri_t)

    def lane_replicate(x, base):
        gate = lax.broadcasted_iota(jnp.int32, (LANES, H * LANES), 0)
        blk = lax.broadcasted_iota(jnp.int32, (LANES, H * LANES), 1) >> 7
        return _dot_tri(x, (gate == blk + base).astype(BF16))

    cum_b_all = lane_replicate(cum_c, H)
    li_b_all = lane_replicate(gc, 0)
    nq = H * ML_DQK
    qs = [main_ref[bb, :, h * ML_DQK:(h + 1) * ML_DQK] for h in range(H)]
    ks = [main_ref[bb, :, nq + h * ML_DQK:nq + (h + 1) * ML_DQK] for h in range(H)]
    s_qks = [_dot_nt(qs[h], ks[h]) for h in range(H)]
    inters = [_dot(qs[h], c_s[bb * H + h].astype(BF16)) for h in range(H)]
    return cum_r, li_r, cum_b_all, li_b_all, ks, s_qks, inters


def _mlstm_heads(bb, staged, main_ref, ng_ref, o_ref, c_s, m_s):
    L = CHUNK
    H = ML_HEADS
    scale = ML_DQK ** -0.5
    nq = H * ML_DQK
    nv = H * ML_DV
    cum_r, li_r, cum_b_all, li_b_all, ks, s_qks, inters = staged
    causal = lax.broadcasted_iota(jnp.int32, (L, L), 1) <= lax.broadcasted_iota(jnp.int32, (L, L), 0)
    ones_blk = jnp.ones((L, ML_DV), BF16)

    for h in range(H):
        v = main_ref[bb, :, 2 * nq + h * ML_DV:2 * nq + (h + 1) * ML_DV]
        op = main_ref[bb, :, 2 * nq + nv + h * ML_DV:2 * nq + nv + (h + 1) * ML_DV]
        vext = jnp.concatenate([v, ones_blk], axis=1)
        cum_b = cum_b_all[:, h * LANES:(h + 1) * LANES]
        li_b = li_b_all[:, h * LANES:(h + 1) * LANES]
        cum_r1 = cum_r[h:h + 1, :]
        li_r1 = li_r[h:h + 1, :]
        m_st = m_s[bb * H + h][:, 0:1]
        cext = c_s[bb * H + h]

        log_d = jnp.where(causal, cum_b[:, :L] - cum_r1 + li_r1, NEG)
        m_t = jnp.maximum(cum_b + m_st, jnp.max(log_d, axis=-1, keepdims=True))
        dmat = jnp.exp(log_d - m_t[:, :L])
        w_inter = jnp.exp(cum_b + m_st - m_t)
        sc = (s_qks[h] * (dmat * scale)).astype(BF16)
        inter = inters[h]
        intra = _dot(sc, vext)
        num = w_inter * inter[:, :ML_DV] + intra[:, :ML_DV]
        den = w_inter * inter[:, ML_DV:] + intra[:, ML_DV:]
        hh = num / jnp.maximum(jnp.abs(den), jnp.exp(-m_t))
        hn = hh * lax.rsqrt(jnp.mean(hh * hh, axis=-1, keepdims=True) + EPS)
        out = hn * ng_ref[:, h * ML_DV:(h + 1) * ML_DV] * _sigmoid(op.astype(F32))
        o_ref[bb, :, h * ML_DV:(h + 1) * ML_DV] = out.astype(o_ref.dtype)

        tot_f = cum_r1[:, L - 1:L]
        log_w_r = tot_f - cum_r1 + li_r1
        m_new = jnp.maximum(tot_f + m_st, jnp.max(log_w_r, axis=-1, keepdims=True))
        decay = jnp.exp(tot_f + m_st - m_new)
        w_b = jnp.exp(tot_f - cum_b + li_b - m_new) * scale
        wk = (ks[h].astype(F32) * w_b[:, :ML_DQK]).astype(BF16)
        c_s[bb * H + h] = decay * cext + _dot_tn(wk, vext)
        m_s[bb * H + h] = jnp.broadcast_to(m_new, (1, LANES))


def _mlstm_core(main, gates, b_if, norm_g, *, nb=2):
    b, s, _ = main.shape
    nc = s // CHUNK
    h = ML_HEADS
    gr = gates[:, :, :2 * h].reshape(b, nc, CHUNK, 2 * h).transpose(0, 1, 3, 2)
    bc = jnp.zeros((1, LANES), F32).at[0, :2 * h].set(b_if)
    br = b_if.reshape(2 * h, 1)
    return pl.pallas_call(
        functools.partial(_mlstm_kernel, nb=nb),
        grid=(b // nb, nc),
        in_specs=[
            pl.BlockSpec((nb, CHUNK, main.shape[2]), lambda i, c: (i, c, 0)),
            pl.BlockSpec((nb, CHUNK, LANES), lambda i, c: (i, c, 0)),
            pl.BlockSpec((nb, 1, 2 * h, CHUNK), lambda i, c: (i, c, 0, 0)),
            pl.BlockSpec((1, LANES), lambda i, c: (0, 0)),
            pl.BlockSpec((2 * h, 1), lambda i, c: (0, 0)),
            pl.BlockSpec((1, h * ML_DV), lambda i, c: (0, 0)),
        ],
        out_specs=pl.BlockSpec((nb, CHUNK, h * ML_DV), lambda i, c: (i, c, 0)),
        out_shape=jax.ShapeDtypeStruct((b, s, h * ML_DV), BF16),
        scratch_shapes=[
            pltpu.VMEM((nb * h, ML_DQK, 2 * ML_DV), F32),
            pltpu.VMEM((nb * h, 1, LANES), F32),
        ],
        compiler_params=_cparams(("parallel", "arbitrary")),
        name="mlstm_core",
    )(main, gates, gr, bc, br, norm_g.reshape(1, -1))


def _sb_kernel(q_ref, k_ref, v_ref, o_ref, acc_s, carry_s, *, t, heads):
    qi = pl.program_id(2)
    lane = lax.broadcasted_iota(jnp.int32, (1, LANES), 1)
    lo = lane < SB_DH
    sels = (lo, jnp.logical_not(lo))
    row = lax.broadcasted_iota(jnp.int32, (t, t), 0)
    col = lax.broadcasted_iota(jnp.int32, (t, t), 1)
    valid = col < row
    lower = (row > col).astype(BF16)
    groups = [slice(g * LANES, (g + 1) * LANES) for g in range(heads // 2)]
    zero = jnp.zeros((), BF16)
    qms = [jnp.where(sels[h % 2], q_ref[0, :, groups[h // 2]], zero) for h in range(heads)]

    def tiles(kts, carries, from_diagonal):
        hs = range(heads)
        k0s = [pl.multiple_of(kt * t, t) for kt in kts]
        masks = [from_diagonal and i == 0 for i in range(len(kts))]
        zs = [[_dot_nt(qms[h], k_ref[0, pl.ds(k0, t), groups[h // 2]]) for h in hs] for k0 in k0s]
        nlps = []
        for zt, masked in zip(zs, masks):
            row_nlps = []
            for h in hs:
                nlp = jnp.maximum(jnp.log2(1.0 + jnp.exp2(jnp.minimum(zt[h], SB_Z_CLAMP))), zt[h])
                row_nlps.append(jnp.where(valid, nlp, 0.0) if masked else nlp)
            nlps.append(row_nlps)
        laters = [[_dot(nt[h].astype(BF16), lower) for h in hs] for nt in nlps]
        for zt, nt, lt, masked, k0 in zip(zs, nlps, laters, masks, k0s):
            weights = []
            for h in hs:
                a = jnp.exp2(zt[h] - nt[h] - lt[h] + carries[h])
                weights.append((jnp.where(valid, a, 0.0) if masked else a).astype(BF16))
            for g, lanes in enumerate(groups):
                vv = v_ref[0, pl.ds(k0, t), lanes]
                contrib = (_dot(weights[2 * g], jnp.where(sels[0], vv, zero))
                           + _dot(weights[2 * g + 1], jnp.where(sels[1], vv, zero)))
                if masked:
                    acc_s[:, lanes] = contrib
                else:
                    acc_s[:, lanes] += contrib
            carries = tuple(carries[h] - jnp.sum(nt[h], axis=-1, keepdims=True) for h in hs)
        return carries

    zero_carries = (jnp.zeros((t, 1), F32),) * heads

    @pl.when(qi > 0)
    def _():
        for h, c in enumerate(tiles([qi, qi - 1], zero_carries, True)):
            carry_s[h] = c

    @pl.when(qi == 0)
    def _():
        for h, c in enumerate(tiles([qi], zero_carries, True)):
            carry_s[h] = c

    carries = tuple(carry_s[h] for h in range(heads))

    def live(carries):
        return functools.reduce(jnp.maximum, [jnp.max(c) for c in carries])

    def cond(st):
        return jnp.logical_and(st[0] >= 0, st[1] > SB_LOG2_ZERO)

    def body(st):
        carries = tiles([st[0]], st[2], False)
        return st[0] - 1, live(carries), carries

    lax.while_loop(cond, body, (qi - 2, live(carries), carries))
    o_ref[0] = acc_s[...].astype(o_ref.dtype)


def _sb_core(main, *, t=256, heads=8):
    b, s, _ = main.shape
    ng = SB_HEADS // heads
    w = heads * SB_DH
    return pl.pallas_call(
        functools.partial(_sb_kernel, t=t, heads=heads),
        grid=(b, ng, s // t),
        in_specs=[
            pl.BlockSpec((1, t, w), lambda i, p, j: (i, j, p)),
            pl.BlockSpec((1, s, w), lambda i, p, j: (i, 0, ng + p)),
            pl.BlockSpec((1, s, w), lambda i, p, j: (i, 0, 2 * ng + p)),
        ],
        out_specs=pl.BlockSpec((1, t, w), lambda i, p, j: (i, j, p)),
        out_shape=jax.ShapeDtypeStruct((b, s, SB_HEADS * SB_DH), BF16),
        scratch_shapes=[pltpu.VMEM((t, w), F32), pltpu.VMEM((heads, t, 1), F32)],
        compiler_params=_cparams(("parallel", "parallel", "arbitrary")),
        name="sb_core",
    )(main, main, main)


def _fox_scatter_matrices():
    pq = np.zeros((FX_HEADS // 2, LANES, 2 * LANES), np.float32)
    pk = np.zeros((FX_HEADS // 2, LANES, 2 * LANES), np.float32)
    for p in range(FX_HEADS // 2):
        for hh, base in ((0, FX_DH), (1, LANES)):
            head = 2 * p + hh
            for i in range(3):
                pq[p, 16 * i + head, base + i] = 1.0
                pq[p, 48, base + 3 + i] = 1.0
                pk[p, 48, base + i] = 1.0
                pk[p, 16 * i + head, base + 3 + i] = -1.0
    return jnp.asarray(pq, BF16), jnp.asarray(pk, BF16)


def _fox_prep_kernel(main_ref, fp_ref, bf_ref, gq_ref, gk_ref, sh_ref, pq_ref, pk_ref, qa_ref, ka_ref,
                     carry_s, *, ts):
    @pl.when(pl.program_id(1) == 0)
    def _():
        carry_s[...] = jnp.zeros_like(carry_s)

    row = lax.broadcasted_iota(jnp.int32, (ts, ts), 0)
    col = lax.broadcasted_iota(jnp.int32, (ts, ts), 1)
    tri = (col <= row).astype(BF16)
    lf = _log_sigmoid(fp_ref[0] + bf_ref[...])
    cum = _tri_dot(tri, lf) + carry_s[...]
    carry_s[...] = cum[ts - 1:ts, :]

    lane = lax.broadcasted_iota(jnp.int32, (1, LANES), 1)
    lo = lane < FX_DH
    head_lane = lane < FX_HEADS

    def packed(c):
        c = jnp.where(head_lane, c, 0.0)
        hi = c.astype(BF16).astype(F32)
        mid = (c - hi).astype(BF16).astype(F32)
        low = (c - hi - mid).astype(BF16).astype(F32)
        out = hi + pltpu.roll(mid, 16, axis=1) + pltpu.roll(low, 32, axis=1)
        return jnp.where(lane == 48, 1.0, out).astype(BF16)

    pack_q = packed(cum * LOG2E - sh_ref[...])
    pack_k = packed(cum * LOG2E)

    n_q = FX_HEADS * FX_DH // LANES
    for c in range(2 * n_q):
        is_q = c < n_q
        p = c % n_q
        xc = main_ref[0, :, c * LANES:(c + 1) * LANES].astype(F32)
        x2 = xc * xc
        s_lo = jnp.sum(jnp.where(lo, x2, 0.0), axis=-1, keepdims=True)
        s_hi = jnp.sum(jnp.where(lo, 0.0, x2), axis=-1, keepdims=True)
        r = jnp.where(lo, lax.rsqrt(s_lo / FX_DH + EPS), lax.rsqrt(s_hi / FX_DH + EPS))
        xn = xc * r * (gq_ref[...] if is_q else gk_ref[...])
        aug = _dot(pack_q, pq_ref[p]) if is_q else _dot(pack_k, pk_ref[p])
        out_ref = qa_ref if is_q else ka_ref
        out_ref[0, 2 * p] = jnp.where(lo, xn, aug[:, :LANES]).astype(out_ref.dtype)
        out_ref[0, 2 * p + 1] = jnp.where(lo, aug[:, LANES:], xn).astype(out_ref.dtype)


def _fox_prep(main, f_pre, b_f, q_norm_g, k_norm_g, shift, *, ts=256):
    b, s, _ = main.shape
    w = 2 * FX_HEADS * FX_DH
    bf = jnp.zeros((1, LANES), F32).at[0, :FX_HEADS].set(b_f)
    sh = jnp.full((1, LANES), shift, F32)
    gq = jnp.tile(q_norm_g, 2).reshape(1, LANES) * (FX_DH ** -0.5 * LOG2E)
    gk = jnp.tile(k_norm_g, 2).reshape(1, LANES)
    pq, pk = _fox_scatter_matrices()
    head_rows = pl.BlockSpec((1, FX_HEADS, ts, LANES), lambda i, j: (i, 0, j, 0))
    const_row = pl.BlockSpec((1, LANES), lambda i, j: (0, 0))
    scatter = pl.BlockSpec(pq.shape, lambda i, j: (0, 0, 0))
    return pl.pallas_call(
        functools.partial(_fox_prep_kernel, ts=ts),
        grid=(b, s // ts),
        in_specs=[
            pl.BlockSpec((1, ts, w), lambda i, j: (i, j, 0)),
            pl.BlockSpec((1, ts, LANES), lambda i, j: (i, j, 0)),
            const_row, const_row, const_row, const_row, scatter, scatter,
        ],
        out_specs=[head_rows, head_rows],
        out_shape=[jax.ShapeDtypeStruct((b, FX_HEADS, s, LANES), BF16)] * 2,
        scratch_shapes=[pltpu.VMEM((1, LANES), F32)],
        compiler_params=_cparams(("parallel", "arbitrary")),
        name="fox_prep",
    )(main, f_pre, bf, gq, gk, sh, pq, pk)


def _fox_kernel(bounded_ref, q_ref, k_ref, v_ref, op_ref, o_ref, acc_s, *, t):
    qi = pl.program_id(2)
    lane = lax.broadcasted_iota(jnp.int32, (1, LANES), 1)
    lo = lane < FX_DH
    sels = (lo, jnp.logical_not(lo))
    row = lax.broadcasted_iota(jnp.int32, (t, t), 0)
    col = lax.broadcasted_iota(jnp.int32, (t, t), 1)
    causal = col <= row
    qs = (q_ref[0, 0], q_ref[0, 1])
    bounded = bounded_ref[0] != 0

    def scores(hh, k0, width, masked):
        s = _dot_nt(qs[hh], k_ref[0, hh, pl.ds(k0, width), :])
        return jnp.where(causal, s, NEG) if masked else s

    def vext(hh, k0, width):
        return jnp.where(sels[hh], v_ref[0, pl.ds(k0, width), :], jnp.ones((), BF16))

    def plain_tile(k0, width):
        ss = [scores(hh, k0, width, False) for hh in range(2)]
        ps = [jnp.exp2(s).astype(BF16) for s in ss]
        for hh in range(2):
            acc_s[hh] += _dot(ps[hh], vext(hh, k0, width))

    def plain_diagonal(k0, k0_prev=None):
        half = t // 2
        parts = [(hh, r0, width) for r0, width in ((0, half), (half, t)) for hh in range(2)]
        ss = [_dot_nt(qs[hh][r0:r0 + half], k_ref[0, hh, pl.ds(k0, width), :]) for hh, r0, width in parts]
        ss_prev = [] if k0_prev is None else [scores(hh, k0_prev, t, False) for hh in range(2)]
        ps = []
        for s, (hh, r0, width) in zip(ss, parts):
            keep = (lax.broadcasted_iota(jnp.int32, (half, width), 1)
                    <= lax.broadcasted_iota(jnp.int32, (half, width), 0) + r0)
            ps.append(jnp.exp2(jnp.where(keep, s, NEG)).astype(BF16))
        ps_prev = [jnp.exp2(s).astype(BF16) for s in ss_prev]
        for p, (hh, r0, width) in zip(ps, parts):
            acc_s[hh, r0:r0 + half] = _dot(p, vext(hh, k0, width))
        for hh, p in enumerate(ps_prev):
            acc_s[hh] += _dot(p, vext(hh, k0_prev, t))

    def online_tile(kt, ms, masked):
        k0 = pl.multiple_of(kt * t, t)
        new_ms = []
        for hh in range(2):
            s = scores(hh, k0, t, masked)
            m_new = jnp.maximum(ms[hh], jnp.max(s, axis=-1, keepdims=True))
            alpha = jnp.exp2(ms[hh] - m_new)
            p = jnp.exp2(s - m_new).astype(BF16)
            acc_s[hh] = alpha * acc_s[hh] + _dot(p, vext(hh, k0, t))
            new_ms.append(m_new)
        return tuple(new_ms)

    @pl.when(bounded)
    def _():
        @pl.when(qi > 0)
        def _():
            plain_diagonal(pl.multiple_of(qi * t, t), pl.multiple_of((qi - 1) * t, t))

        @pl.when(qi == 0)
        def _():
            plain_diagonal(pl.multiple_of(qi * t, t))

        def step(kt, carry):
            plain_tile(pl.multiple_of(kt * t, t), t)
            return carry

        lax.fori_loop(0, qi - 1, step, 0)

    @pl.when(jnp.logical_not(bounded))
    def _():
        acc_s[...] = jnp.zeros_like(acc_s)
        m0 = jnp.full((t, 1), NEG, F32)
        ms = lax.fori_loop(0, qi, lambda kt, ms: online_tile(kt, ms, False), (m0, m0))
        online_tile(qi, ms, True)

    acc0 = acc_s[0]
    acc1 = acc_s[1]
    num = jnp.where(lo, acc0, acc1)
    den = pltpu.roll(jnp.where(lo, acc1, acc0), FX_DH, axis=1)
    o_ref[0] = (num / den * _sigmoid(op_ref[0].astype(F32))).astype(o_ref.dtype)


def _fox_logit_bound(q_norm_g, k_norm_g):
    return (FX_DH ** 0.5 * LOG2E) * jnp.max(jnp.abs(q_norm_g)) * jnp.max(jnp.abs(k_norm_g))


def _fox_core(main, qa, ka, bounded, *, t=512):
    b, s, _ = main.shape
    hp = FX_HEADS // 2
    return pl.pallas_call(
        functools.partial(_fox_kernel, t=t),
        grid=(b, hp, s // t),
        in_specs=[
            pl.BlockSpec(memory_space=pltpu.SMEM),
            pl.BlockSpec((1, 2, t, LANES), lambda i, p, j: (i, p, j, 0)),
            pl.BlockSpec((1, 2, s, LANES), lambda i, p, j: (i, p, 0, 0)),
            pl.BlockSpec((1, s, LANES), lambda i, p, j: (i, 0, 2 * hp + p)),
            pl.BlockSpec((1, t, LANES), lambda i, p, j: (i, j, 3 * hp + p)),
        ],
        out_specs=pl.BlockSpec((1, t, LANES), lambda i, p, j: (i, j, p)),
        out_shape=jax.ShapeDtypeStruct((b, s, FX_HEADS * FX_DH), BF16),
        scratch_shapes=[pltpu.VMEM((2, t, LANES), F32)],
        compiler_params=_cparams(("parallel", "parallel", "arbitrary")),
        name="fox_core",
    )(bounded.astype(jnp.int32).reshape(1), qa, ka, main, main)


def _gla_kernel(main_ref, gl_ref, wgu_ref, bg_ref, ng_ref, o_ref, st_s, cum_s, k_s, *, nb):
    L = CHUNK
    H = GLA_HEADS
    SUB = GLA_SUB
    nk = H * GLA_DK
    nv = H * GLA_DV
    scale = GLA_DK ** -0.5

    @pl.when(pl.program_id(1) == 0)
    def _():
        st_s[...] = jnp.zeros_like(st_s)

    row = lax.broadcasted_iota(jnp.int32, (L, L), 0)
    col = lax.broadcasted_iota(jnp.int32, (L, L), 1)
    tri = (col <= row).astype(BF16)
    eye = (lax.broadcasted_iota(jnp.int32, (GLA_DK, GLA_DK), 0)
           == lax.broadcasted_iota(jnp.int32, (GLA_DK, GLA_DK), 1))

    w_hi, w_lo = _split_bf16(wgu_ref[...])
    cum_alls = []
    for bb in range(nb):
        g_hi, g_lo = _split_bf16(gl_ref[bb])
        logits = _dot(g_hi, w_hi) + _dot(g_hi, w_lo) + _dot(g_lo, w_hi)
        la = _log_sigmoid(logits + bg_ref[...]) * (1.0 / GLA_TAU)
        cum_alls.append(_tri_dot(tri, la))

    units = [(bb, h) for bb in range(nb) for h in range(H)]
    nsub = L // SUB
    half = SUB // 2
    qs, v_bfs, cums, decays = [], [], [], []

    inters, updates, a_offs = [], [], []
    for u, (bb, h) in enumerate(units):
        ks = slice(h * GLA_DK, (h + 1) * GLA_DK)
        cum = cum_alls[bb][:, ks]
        tot = cum[L - 1:L, :]
        q = main_ref[bb, :, ks].astype(F32) * scale
        k = main_ref[bb, :, nk + h * GLA_DK:nk + (h + 1) * GLA_DK].astype(F32)
        v_bf = main_ref[bb, :, 2 * nk + h * GLA_DV:2 * nk + (h + 1) * GLA_DV]
        inters.append(_dot((q * jnp.exp(cum)).astype(BF16), st_s[u].astype(BF16)))
        updates.append(_dot_tn((k * jnp.exp(tot - cum)).astype(BF16), v_bf))
        offs = []
        for i in range(1, nsub):
            rs = slice(i * SUB, (i + 1) * SUB)
            ref_pt = cum[i * SUB - 1:i * SUB, :]
            q_rel = (q[rs] * jnp.exp(cum[rs] - ref_pt)).astype(BF16)
            k_rel = (k[:i * SUB] * jnp.exp(ref_pt - cum[:i * SUB])).astype(BF16)
            offs.append(_dot_nt(q_rel, k_rel))
        a_offs.append(offs)
        tot_col = jnp.sum(jnp.where(eye, jnp.broadcast_to(tot, (GLA_DK, GLA_DK)), 0.0),
                          axis=-1, keepdims=True)
        decays.append(jnp.exp(tot_col))
        cum_s[u] = cum
        k_s[u] = k
        qs.append(q)
        v_bfs.append(v_bf)
        cums.append(cum)

    h_offs = [[_dot(a_offs[u][i - 1].astype(BF16), v_bfs[u][:i * SUB]) for i in range(1, nsub)]
              for u in range(len(units))]

    t_half = lax.broadcasted_iota(jnp.int32, (half, 1), 0)
    lane = lax.broadcasted_iota(jnp.int32, (1, LANES), 1)
    for u, (bb, h) in enumerate(units):
        vs = slice(h * GLA_DV, (h + 1) * GLA_DV)
        for i in range(nsub):
            rs = slice(i * SUB, (i + 1) * SUB)
            q_top, q_bot = qs[u][i * SUB:i * SUB + half], qs[u][i * SUB + half:(i + 1) * SUB]
            c_top, c_bot = cums[u][i * SUB:i * SUB + half], cums[u][i * SUB + half:(i + 1) * SUB]
            a_top = jnp.zeros((half, LANES), F32)
            a_bot = jnp.zeros((half, LANES), F32)
            for j in range(SUB):
                rj = i * SUB + j
                kj = k_s[u, rj:rj + 1, :]
                cj = cum_s[u, rj:rj + 1, :]
                if j < half:
                    e_top = jnp.exp(jnp.where(t_half >= j, c_top - cj, NEG))
                    a_top = jnp.where(lane == j, jnp.sum(q_top * kj * e_top, axis=-1, keepdims=True), a_top)
                    e_bot = jnp.exp(c_bot - cj)
                else:
                    e_bot = jnp.exp(jnp.where(t_half >= j - half, c_bot - cj, NEG))
                a_bot = jnp.where(lane == j, jnp.sum(q_bot * kj * e_bot, axis=-1, keepdims=True), a_bot)
            a_diag = jnp.concatenate([a_top, a_bot], axis=0)[:, :SUB].astype(BF16)
            hb = inters[u][rs] + _dot(a_diag, v_bfs[u][rs])
            if i > 0:
                hb = hb + h_offs[u][i - 1]
            hn = hb * lax.rsqrt(jnp.mean(hb * hb, axis=-1, keepdims=True) + EPS)
            r = main_ref[bb, rs, 2 * nk + nv + h * GLA_DV:2 * nk + nv + (h + 1) * GLA_DV].astype(F32)
            out = hn * ng_ref[:, vs] * (r * _sigmoid(r))
            o_ref[bb, rs, vs] = out.astype(o_ref.dtype)

    for u in range(len(units)):
        st_s[u] = decays[u] * st_s[u] + updates[u]


def _gla_core(main, g_low, w_gate_up, b_gate, norm_g, *, nb=2):
    b, s, _ = main.shape
    nc = s // CHUNK
    nk = GLA_HEADS * GLA_DK
    nv = GLA_HEADS * GLA_DV
    wgu = jnp.zeros((LANES, nk), F32).at[:GLA_RANK].set(w_gate_up)
    return pl.pallas_call(
        functools.partial(_gla_kernel, nb=nb),
        grid=(b // nb, nc),
        in_specs=[
            pl.BlockSpec((nb, CHUNK, main.shape[2]), lambda i, c: (i, c, 0)),
            pl.BlockSpec((nb, CHUNK, LANES), lambda i, c: (i, c, 0)),
            pl.BlockSpec((LANES, nk), lambda i, c: (0, 0)),
            pl.BlockSpec((1, nk), lambda i, c: (0, 0)),
            pl.BlockSpec((1, nv), lambda i, c: (0, 0)),
        ],
        out_specs=pl.BlockSpec((nb, CHUNK, nv), lambda i, c: (i, c, 0)),
        out_shape=jax.ShapeDtypeStruct((b, s, nv), BF16),
        scratch_shapes=[
            pltpu.VMEM((nb * GLA_HEADS, GLA_DK, GLA_DV), F32),
            pltpu.VMEM((nb * GLA_HEADS, CHUNK, GLA_DK), F32),
            pltpu.VMEM((nb * GLA_HEADS, CHUNK, GLA_DK), F32),
        ],
        compiler_params=_cparams(("parallel", "arbitrary")),
        name="gla_core",
    )(main, g_low, wgu, b_gate.reshape(1, -1), norm_g.reshape(1, -1))


def _gated_in_proj(x, g, w_in, n_gate):
    n_main = w_in.shape[1] - n_gate
    w_bf = w_in.astype(BF16)
    w_gate = jnp.pad(w_bf[:, n_main:], ((0, 0), (0, LANES - n_gate)))
    return _in_proj(x, g, w_bf, n_main, w_gate)


def kernel(x, mix_norm_g, ffn_norm_g, ffn_w_gate, ffn_w_up, ffn_w_down, ml_w_in, ml_b_if, ml_norm_g, ml_w_out, sb_w_in, sb_w_out, fx_w_in, fx_b_f, fx_q_norm_g, fx_k_norm_g, fx_w_out, gla_w_in, gla_w_gate_up, gla_b_gate, gla_norm_g, gla_w_out):
    b, s, d = x.shape
    depth = mix_norm_g.shape[0]
    h = x.reshape(b * s, d)
    ffn_gate, ffn_up, ffn_down = (w.astype(BF16) for w in (ffn_w_gate, ffn_w_up, ffn_w_down))
    for layer in range(depth):
        kind = layer % 4
        j = layer // 4
        g_mix = mix_norm_g[layer].reshape(1, d)
        if kind == 0:
            main, gates = _gated_in_proj(h, g_mix, ml_w_in[j], 2 * ML_HEADS)
            a = _mlstm_core(main.reshape(b, s, -1), gates.reshape(b, s, LANES), ml_b_if[j], ml_norm_g[j])
            w_out = ml_w_out[j]
        elif kind == 1:
            nq = SB_HEADS * SB_DH
            w_sb = jnp.concatenate([sb_w_in[j][:, :nq] * (SB_DH ** -0.5 * LOG2E), sb_w_in[j][:, nq:]], axis=1)
            (main,) = _in_proj(h, g_mix, w_sb.astype(BF16), w_sb.shape[1])
            a = _sb_core(main.reshape(b, s, -1))
            w_out = sb_w_out[j]
        elif kind == 2:
            main, gates = _gated_in_proj(h, g_mix, fx_w_in[j], FX_HEADS)
            main = main.reshape(b, s, -1)
            bound = _fox_logit_bound(fx_q_norm_g[j], fx_k_norm_g[j])
            bounded = bound <= FX_MAX_SHIFT
            qa, ka = _fox_prep(main, gates.reshape(b, s, LANES), fx_b_f[j], fx_q_norm_g[j], fx_k_norm_g[j],
                               jnp.where(bounded, bound, 0.0))
            a = _fox_core(main, qa, ka, bounded)
            w_out = fx_w_out[j]
        else:
            main, gates = _gated_in_proj(h, g_mix, gla_w_in[j], GLA_RANK)
            a = _gla_core(main.reshape(b, s, -1), gates.reshape(b, s, LANES), gla_w_gate_up[j], gla_b_gate[j], gla_norm_g[j])
            w_out = gla_w_out[j]
        h = _out_ffn(h, a.reshape(b * s, -1), w_out.astype(BF16), ffn_norm_g[layer].reshape(1, d),
                     ffn_gate, ffn_up, ffn_down, layer)
    return h.reshape(b, s, d)
```

```python
import functools

import jax
import jax.numpy as jnp
import numpy as np
from jax import lax
from jax.experimental import pallas as pl
from jax.experimental.pallas import tpu as pltpu

F32 = jnp.float32
BF16 = jnp.bfloat16

EPS = 1e-6
CHUNK = 64
ML_HEADS, ML_DQK, ML_DV = 8, 64, 128
SB_HEADS, SB_DH = 16, 64
FX_HEADS, FX_DH = 16, 64
GLA_HEADS, GLA_DK, GLA_DV = 4, 128, 256
GLA_RANK = 16
GLA_TAU = 16.0
GLA_SUB = 16
NEG = -1e30
LOG2E = 1.4426950408889634
FX_MAX_SHIFT = 60.0
SB_LOG2_ZERO = -150.0
SB_Z_CLAMP = 100.0
LANES = 128
VMEM_LIMIT = 56 * 1024 * 1024


def _cparams(sem):
    return pltpu.CompilerParams(dimension_semantics=sem, vmem_limit_bytes=VMEM_LIMIT)


def _resident(shape):
    return pl.BlockSpec(shape, lambda i: (0,) * len(shape), pipeline_mode=pl.Buffered(1))


def _log_sigmoid(x):
    return jnp.minimum(x, 0.0) - jnp.log(1.0 + jnp.exp(-jnp.abs(x)))


def _sigmoid(x):
    return 1.0 / (1.0 + jnp.exp(-x))


def _split_bf16(x):
    hi = x.astype(BF16)
    lo = (x - hi.astype(F32)).astype(BF16)
    return hi, lo


def _dot(a, b):
    return jnp.dot(a, b, preferred_element_type=F32)


def _dot_nt(a, b):
    return lax.dot_general(a, b, (((1,), (1,)), ((), ())), preferred_element_type=F32)


def _dot_tn(a, b):
    return lax.dot_general(a, b, (((0,), (0,)), ((), ())), preferred_element_type=F32)


def _tri_dot(tri, x):
    hi, lo = _split_bf16(x)
    return _dot(tri, hi) + _dot(tri, lo)


def _dot_tri(x, tri):
    hi, lo = _split_bf16(x)
    return _dot(hi, tri) + _dot(lo, tri)


def _in_proj_kernel(x_ref, g_ref, w_ref, *rest, tr, tn):
    wg_ref, main_ref, gate_ref = rest if len(rest) == 3 else (None, rest[0], None)

    def normed(rows):
        x = x_ref[rows, :]
        ms = jnp.mean(x * x, axis=-1, keepdims=True)
        return (x * lax.rsqrt(ms + EPS) * g_ref[...]).astype(BF16)

    blocks = [slice(r * tr, (r + 1) * tr) for r in range(x_ref.shape[0] // tr)]
    u = normed(blocks[0])
    for r, rows in enumerate(blocks):
        u_next = None
        for c in range(main_ref.shape[-1] // tn):
            cs = slice(c * tn, (c + 1) * tn)
            main_ref[rows, cs] = _dot(u, w_ref[:, cs]).astype(main_ref.dtype)
            if c == 0 and r + 1 < len(blocks):
                u_next = normed(blocks[r + 1])
        if gate_ref is not None:
            gate_ref[rows, :] = _dot(u, wg_ref[...])
        u = u_next


def _in_proj(x, g, w_in, wn, w_gate=None, *, tm=1024, tr=256, tn=512):
    n, d = x.shape
    in_specs = [pl.BlockSpec((tm, d), lambda i: (i, 0)), _resident((1, d)), _resident(w_in.shape)]
    out_specs = [pl.BlockSpec((tm, wn), lambda i: (i, 0))]
    out_shape = [jax.ShapeDtypeStruct((n, wn), BF16)]
    operands = [x, g, w_in]
    if w_gate is not None:
        in_specs.append(_resident((d, LANES)))
        out_specs.append(pl.BlockSpec((tm, LANES), lambda i: (i, 0)))
        out_shape.append(jax.ShapeDtypeStruct((n, LANES), F32))
        operands.append(w_gate)
    return pl.pallas_call(
        functools.partial(_in_proj_kernel, tr=tr, tn=tn),
        grid=(n // tm,),
        in_specs=in_specs,
        out_specs=out_specs,
        out_shape=out_shape,
        compiler_params=_cparams(("parallel",)),
        name="in_proj",
    )(*operands)


def _out_ffn_kernel(h_ref, a_ref, wo_ref, g_ref, wg_ref, wu_ref, wd_ref, o_ref, *, tf):
    h1 = h_ref[...] + _dot(a_ref[...], wo_ref[...])
    ms = jnp.mean(h1 * h1, axis=-1, keepdims=True)
    u = (h1 * lax.rsqrt(ms + EPS) * g_ref[...]).astype(BF16)
    acc = h1
    for c in range(wg_ref.shape[1] // tf):
        cs = slice(c * tf, (c + 1) * tf)
        gt = _dot(u, wg_ref[:, cs])
        up = _dot(u, wu_ref[:, cs])
        hid = (gt * _sigmoid(gt) * up).astype(BF16)
        acc = acc + _dot(hid, wd_ref[cs, :])
    o_ref[...] = acc


def _out_ffn(h, a, w_out, g, w_gate, w_up, w_down, layer, *, tm=512, tf=256):
    n, d = h.shape
    dff = w_gate.shape[2]

    def slab(rows, cols):
        return pl.BlockSpec((None, rows, cols), lambda i: (layer, 0, 0), pipeline_mode=pl.Buffered(1))

    return pl.pallas_call(
        functools.partial(_out_ffn_kernel, tf=tf),
        grid=(n // tm,),
        in_specs=[
            pl.BlockSpec((tm, d), lambda i: (i, 0)),
            pl.BlockSpec((tm, d), lambda i: (i, 0)),
            _resident((d, d)),
            _resident((1, d)),
            slab(d, dff),
            slab(d, dff),
            slab(dff, d),
        ],
        out_specs=pl.BlockSpec((tm, d), lambda i: (i, 0)),
        out_shape=jax.ShapeDtypeStruct((n, d), F32),
        compiler_params=_cparams(("parallel",)),
        name="out_ffn",
    )(h, a, w_out, g, w_gate, w_up, w_down)


def _mlstm_kernel(main_ref, gc_ref, gr_ref, bc_ref, br_ref, ng_ref, o_ref, c_s, m_s, *, nb):
    @pl.when(pl.program_id(1) == 0)
    def _():
        c_s[...] = jnp.zeros_like(c_s)
        m_s[...] = jnp.zeros_like(m_s)

    staged = [_mlstm_gates(bb, main_ref, gc_ref, gr_ref, bc_ref, br_ref, c_s) for bb in range(nb)]
    for bb in range(nb):
        _mlstm_heads(bb, staged[bb], main_ref, ng_ref, o_ref, c_s, m_s)


def _mlstm_gates(bb, main_ref, gc_ref, gr_ref, bc_ref, br_ref, c_s):
    L = CHUNK
    H = ML_HEADS
    row = lax.broadcasted_iota(jnp.int32, (L, L), 0)
    col = lax.broadcasted_iota(jnp.int32, (L, L), 1)
    tri = (col <= row).astype(BF16)
    tri_t = (row <= col).astype(BF16)

    gc = gc_ref[bb] + bc_ref[...]
    gr = gr_ref[bb, 0] + br_ref[...]
    cum_c = _tri_dot(tri, _log_sigmoid(gc))
    li_r = gr[0:H]
    cum_r = _dot_tri(_log_sigmoid(gr[H:2 * H]), tri_t)

    def lane_replicate(x, base):
        gate = lax.broadcasted_iota(jnp.int32, (LANES, H * LANES), 0)
        blk = lax.broadcasted_iota(jnp.int32, (LANES, H * LANES), 1) >> 7
        return _dot_tri(x, (gate == blk + base).astype(BF16))

    cum_b_all = lane_replicate(cum_c, H)
    li_b_all = lane_replicate(gc, 0)
    nq = H * ML_DQK
    qs = [main_ref[bb, :, h * ML_DQK:(h + 1) * ML_DQK] for h in range(H)]
    ks = [main_ref[bb, :, nq + h * ML_DQK:nq + (h + 1) * ML_DQK] for h in range(H)]
    s_qks = [_dot_nt(qs[h], ks[h]) for h in range(H)]
    inters = [_dot(qs[h], c_s[bb * H + h].astype(BF16)) for h in range(H)]
    return cum_r, li_r, cum_b_all, li_b_all, ks, s_qks, inters


def _mlstm_heads(bb, staged, main_ref, ng_ref, o_ref, c_s, m_s):
    L = CHUNK
    H = ML_HEADS
    scale = ML_DQK ** -0.5
    nq = H * ML_DQK
    nv = H * ML_DV
    cum_r, li_r, cum_b_all, li_b_all, ks, s_qks, inters = staged
    causal = lax.broadcasted_iota(jnp.int32, (L, L), 1) <= lax.broadcasted_iota(jnp.int32, (L, L), 0)
    ones_blk = jnp.ones((L, ML_DV), BF16)

    for h in range(H):
        v = main_ref[bb, :, 2 * nq + h * ML_DV:2 * nq + (h + 1) * ML_DV]
        op = main_ref[bb, :, 2 * nq + nv + h * ML_DV:2 * nq + nv + (h + 1) * ML_DV]
        vext = jnp.concatenate([v, ones_blk], axis=1)
        cum_b = cum_b_all[:, h * LANES:(h + 1) * LANES]
        li_b = li_b_all[:, h * LANES:(h + 1) * LANES]
        cum_r1 = cum_r[h:h + 1, :]
        li_r1 = li_r[h:h + 1, :]
        m_st = m_s[bb * H + h][:, 0:1]
        cext = c_s[bb * H + h]

        log_d = jnp.where(causal, cum_b[:, :L] - cum_r1 + li_r1, NEG)
        m_t = jnp.maximum(cum_b + m_st, jnp.max(log_d, axis=-1, keepdims=True))
        dmat = jnp.exp(log_d - m_t[:, :L])
        w_inter = jnp.exp(cum_b + m_st - m_t)
        sc = (s_qks[h] * (dmat * scale)).astype(BF16)
        inter = inters[h]
        intra = _dot(sc, vext)
        num = w_inter * inter[:, :ML_DV] + intra[:, :ML_DV]
        den = w_inter * inter[:, ML_DV:] + intra[:, ML_DV:]
        hh = num / jnp.maximum(jnp.abs(den), jnp.exp(-m_t))
        hn = hh * lax.rsqrt(jnp.mean(hh * hh, axis=-1, keepdims=True) + EPS)
        out = hn * ng_ref[:, h * ML_DV:(h + 1) * ML_DV] * _sigmoid(op.astype(F32))
        o_ref[bb, :, h * ML_DV:(h + 1) * ML_DV] = out.astype(o_ref.dtype)

        tot_f = cum_r1[:, L - 1:L]
        log_w_r = tot_f - cum_r1 + li_r1
        m_new = jnp.maximum(tot_f + m_st, jnp.max(log_w_r, axis=-1, keepdims=True))
        decay = jnp.exp(tot_f + m_st - m_new)
        w_b = jnp.exp(tot_f - cum_b + li_b - m_new) * scale
        wk = (ks[h].astype(F32) * w_b[:, :ML_DQK]).astype(BF16)
        c_s[bb * H + h] = decay * cext + _dot_tn(wk, vext)
        m_s[bb * H + h] = jnp.broadcast_to(m_new, (1, LANES))


def _mlstm_core(main, gates, b_if, norm_g, *, nb=2):
    b, s, _ = main.shape
    nc = s // CHUNK
    h = ML_HEADS
    gr = gates[:, :, :2 * h].reshape(b, nc, CHUNK, 2 * h).transpose(0, 1, 3, 2)
    bc = jnp.zeros((1, LANES), F32).at[0, :2 * h].set(b_if)
    br = b_if.reshape(2 * h, 1)
    return pl.pallas_call(
        functools.partial(_mlstm_kernel, nb=nb),
        grid=(b // nb, nc),
        in_specs=[
            pl.BlockSpec((nb, CHUNK, main.shape[2]), lambda i, c: (i, c, 0)),
            pl.BlockSpec((nb, CHUNK, LANES), lambda i, c: (i, c, 0)),
            pl.BlockSpec((nb, 1, 2 * h, CHUNK), lambda i, c: (i, c, 0, 0)),
            pl.BlockSpec((1, LANES), lambda i, c: (0, 0)),
            pl.BlockSpec((2 * h, 1), lambda i, c: (0, 0)),
            pl.BlockSpec((1, h * ML_DV), lambda i, c: (0, 0)),
        ],
        out_specs=pl.BlockSpec((nb, CHUNK, h * ML_DV), lambda i, c: (i, c, 0)),
        out_shape=jax.ShapeDtypeStruct((b, s, h * ML_DV), BF16),
        scratch_shapes=[
            pltpu.VMEM((nb * h, ML_DQK, 2 * ML_DV), F32),
            pltpu.VMEM((nb * h, 1, LANES), F32),
        ],
        compiler_params=_cparams(("parallel", "arbitrary")),
        name="mlstm_core",
    )(main, gates, gr, bc, br, norm_g.reshape(1, -1))


def _sb_kernel(q_ref, k_ref, v_ref, o_ref, acc_s, carry_s, *, t, heads):
    qi = pl.program_id(2)
    lane = lax.broadcasted_iota(jnp.int32, (1, LANES), 1)
    lo = lane < SB_DH
    sels = (lo, jnp.logical_not(lo))
    row = lax.broadcasted_iota(jnp.int32, (t, t), 0)
    col = lax.broadcasted_iota(jnp.int32, (t, t), 1)
    valid = col < row
    lower = (row > col).astype(BF16)
    groups = [slice(g * LANES, (g + 1) * LANES) for g in range(heads // 2)]
    zero = jnp.zeros((), BF16)
    qms = [jnp.where(sels[h % 2], q_ref[0, :, groups[h // 2]], zero) for h in range(heads)]

    def tiles(kts, carries, from_diagonal):
        hs = range(heads)
        k0s = [pl.multiple_of(kt * t, t) for kt in kts]
        masks = [from_diagonal and i == 0 for i in range(len(kts))]
        zs = [[_dot_nt(qms[h], k_ref[0, pl.ds(k0, t), groups[h // 2]]) for h in hs] for k0 in k0s]
        nlps = []
        for zt, masked in zip(zs, masks):
            row_nlps = []
            for h in hs:
                nlp = jnp.maximum(jnp.log2(1.0 + jnp.exp2(jnp.minimum(zt[h], SB_Z_CLAMP))), zt[h])
                row_nlps.append(jnp.where(valid, nlp, 0.0) if masked else nlp)
            nlps.append(row_nlps)
        laters = [[_dot(nt[h].astype(BF16), lower) for h in hs] for nt in nlps]
        for zt, nt, lt, masked, k0 in zip(zs, nlps, laters, masks, k0s):
            weights = []
            for h in hs:
                a = jnp.exp2(zt[h] - nt[h] - lt[h] + carries[h])
                weights.append((jnp.where(valid, a, 0.0) if masked else a).astype(BF16))
            for g, lanes in enumerate(groups):
                vv = v_ref[0, pl.ds(k0, t), lanes]
                contrib = (_dot(weights[2 * g], jnp.where(sels[0], vv, zero))
                           + _dot(weights[2 * g + 1], jnp.where(sels[1], vv, zero)))
                if masked:
                    acc_s[:, lanes] = contrib
                else:
                    acc_s[:, lanes] += contrib
            carries = tuple(carries[h] - jnp.sum(nt[h], axis=-1, keepdims=True) for h in hs)
        return carries

    zero_carries = (jnp.zeros((t, 1), F32),) * heads

    @pl.when(qi > 0)
    def _():
        for h, c in enumerate(tiles([qi, qi - 1], zero_carries, True)):
            carry_s[h] = c

    @pl.when(qi == 0)
    def _():
        for h, c in enumerate(tiles([qi], zero_carries, True)):
            carry_s[h] = c

    carries = tuple(carry_s[h] for h in range(heads))

    def live(carries):
        return functools.reduce(jnp.maximum, [jnp.max(c) for c in carries])

    def cond(st):
        return jnp.logical_and(st[0] >= 0, st[1] > SB_LOG2_ZERO)

    def body(st):
        carries = tiles([st[0]], st[2], False)
        return st[0] - 1, live(carries), carries

    lax.while_loop(cond, body, (qi - 2, live(carries), carries))
    o_ref[0] = acc_s[...].astype(o_ref.dtype)


def _sb_core(main, *, t=256, heads=8):
    b, s, _ = main.shape
    ng = SB_HEADS // heads
    w = heads * SB_DH
    return pl.pallas_call(
        functools.partial(_sb_kernel, t=t, heads=heads),
        grid=(b, ng, s // t),
        in_specs=[
            pl.BlockSpec((1, t, w), lambda i, p, j: (i, j, p)),
            pl.BlockSpec((1, s, w), lambda i, p, j: (i, 0, ng + p)),
            pl.BlockSpec((1, s, w), lambda i, p, j: (i, 0, 2 * ng + p)),
        ],
        out_specs=pl.BlockSpec((1, t, w), lambda i, p, j: (i, j, p)),
        out_shape=jax.ShapeDtypeStruct((b, s, SB_HEADS * SB_DH), BF16),
        scratch_shapes=[pltpu.VMEM((t, w), F32), pltpu.VMEM((heads, t, 1), F32)],
        compiler_params=_cparams(("parallel", "parallel", "arbitrary")),
        name="sb_core",
    )(main, main, main)


def _fox_scatter_matrices():
    pq = np.zeros((FX_HEADS // 2, LANES, 2 * LANES), np.float32)
    pk = np.zeros((FX_HEADS // 2, LANES, 2 * LANES), np.float32)
    for p in range(FX_HEADS // 2):
        for hh, base in ((0, FX_DH), (1, LANES)):
            head = 2 * p + hh
            for i in range(3):
                pq[p, 16 * i + head, base + i] = 1.0
                pq[p, 48, base + 3 + i] = 1.0
                pk[p, 48, base + i] = 1.0
                pk[p, 16 * i + head, base + 3 + i] = -1.0
    return jnp.asarray(pq, BF16), jnp.asarray(pk, BF16)


def _fox_prep_kernel(main_ref, fp_ref, bf_ref, gq_ref, gk_ref, sh_ref, pq_ref, pk_ref, qa_ref, ka_ref,
                     carry_s, *, ts):
    @pl.when(pl.program_id(1) == 0)
    def _():
        carry_s[...] = jnp.zeros_like(carry_s)

    row = lax.broadcasted_iota(jnp.int32, (ts, ts), 0)
    col = lax.broadcasted_iota(jnp.int32, (ts, ts), 1)
    tri = (col <= row).astype(BF16)
    lf = _log_sigmoid(fp_ref[0] + bf_ref[...])
    cum = _tri_dot(tri, lf) + carry_s[...]
    carry_s[...] = cum[ts - 1:ts, :]

    lane = lax.broadcasted_iota(jnp.int32, (1, LANES), 1)
    lo = lane < FX_DH
    head_lane = lane < FX_HEADS

    def packed(c):
        c = jnp.where(head_lane, c, 0.0)
        hi = c.astype(BF16).astype(F32)
        mid = (c - hi).astype(BF16).astype(F32)
        low = (c - hi - mid).astype(BF16).astype(F32)
        out = hi + pltpu.roll(mid, 16, axis=1) + pltpu.roll(low, 32, axis=1)
        return jnp.where(lane == 48, 1.0, out).astype(BF16)

    pack_q = packed(cum * LOG2E - sh_ref[...])
    pack_k = packed(cum * LOG2E)

    n_q = FX_HEADS * FX_DH // LANES
    for c in range(2 * n_q):
        is_q = c < n_q
        p = c % n_q
        xc = main_ref[0, :, c * LANES:(c + 1) * LANES].astype(F32)
        x2 = xc * xc
        s_lo = jnp.sum(jnp.where(lo, x2, 0.0), axis=-1, keepdims=True)
        s_hi = jnp.sum(jnp.where(lo, 0.0, x2), axis=-1, keepdims=True)
        r = jnp.where(lo, lax.rsqrt(s_lo / FX_DH + EPS), lax.rsqrt(s_hi / FX_DH + EPS))
        xn = xc * r * (gq_ref[...] if is_q else gk_ref[...])
        aug = _dot(pack_q, pq_ref[p]) if is_q else _dot(pack_k, pk_ref[p])
        out_ref = qa_ref if is_q else ka_ref
        out_ref[0, 2 * p] = jnp.where(lo, xn, aug[:, :LANES]).astype(out_ref.dtype)
        out_ref[0, 2 * p + 1] = jnp.where(lo, aug[:, LANES:], xn).astype(out_ref.dtype)


def _fox_prep(main, f_pre, b_f, q_norm_g, k_norm_g, shift, *, ts=256):
    b, s, _ = main.shape
    w = 2 * FX_HEADS * FX_DH
    bf = jnp.zeros((1, LANES), F32).at[0, :FX_HEADS].set(b_f)
    sh = jnp.full((1, LANES), shift, F32)
    gq = jnp.tile(q_norm_g, 2).reshape(1, LANES) * (FX_DH ** -0.5 * LOG2E)
    gk = jnp.tile(k_norm_g, 2).reshape(1, LANES)
    pq, pk = _fox_scatter_matrices()
    head_rows = pl.BlockSpec((1, FX_HEADS, ts, LANES), lambda i, j: (i, 0, j, 0))
    const_row = pl.BlockSpec((1, LANES), lambda i, j: (0, 0))
    scatter = pl.BlockSpec(pq.shape, lambda i, j: (0, 0, 0))
    return pl.pallas_call(
        functools.partial(_fox_prep_kernel, ts=ts),
        grid=(b, s // ts),
        in_specs=[
            pl.BlockSpec((1, ts, w), lambda i, j: (i, j, 0)),
            pl.BlockSpec((1, ts, LANES), lambda i, j: (i, j, 0)),
            const_row, const_row, const_row, const_row, scatter, scatter,
        ],
        out_specs=[head_rows, head_rows],
        out_shape=[jax.ShapeDtypeStruct((b, FX_HEADS, s, LANES), BF16)] * 2,
        scratch_shapes=[pltpu.VMEM((1, LANES), F32)],
        compiler_params=_cparams(("parallel", "arbitrary")),
        name="fox_prep",
    )(main, f_pre, bf, gq, gk, sh, pq, pk)


def _fox_kernel(bounded_ref, q_ref, k_ref, v_ref, op_ref, o_ref, acc_s, *, t, heads):
    qi = pl.program_id(2)
    hs = range(heads)
    lane = lax.broadcasted_iota(jnp.int32, (1, LANES), 1)
    lo = lane < FX_DH
    sels = (lo, jnp.logical_not(lo))
    row = lax.broadcasted_iota(jnp.int32, (t, t), 0)
    col = lax.broadcasted_iota(jnp.int32, (t, t), 1)
    causal = col <= row
    qs = [q_ref[0, hh] for hh in hs]
    bounded = bounded_ref[0] != 0

    def scores(hh, k0, width, masked):
        s = _dot_nt(qs[hh], k_ref[0, hh, pl.ds(k0, width), :])
        return jnp.where(causal, s, NEG) if masked else s

    def vext(hh, k0, width):
        v2 = v_ref[0, pl.ds(k0, width), (hh // 2) * LANES:(hh // 2 + 1) * LANES]
        return jnp.where(sels[hh % 2], v2, jnp.ones((), BF16))

    def plain_tile(k0, width):
        ss = [scores(hh, k0, width, False) for hh in hs]
        ps = [jnp.exp2(s).astype(BF16) for s in ss]
        for hh in hs:
            acc_s[hh] += _dot(ps[hh], vext(hh, k0, width))

    def plain_diagonal(k0, k0_prev=None):
        half = t // 2
        parts = [(hh, r0, width) for r0, width in ((0, half), (half, t)) for hh in hs]
        ss = [_dot_nt(qs[hh][r0:r0 + half], k_ref[0, hh, pl.ds(k0, width), :]) for hh, r0, width in parts]
        ss_prev = [] if k0_prev is None else [scores(hh, k0_prev, t, False) for hh in hs]
        ps = []
        for s, (hh, r0, width) in zip(ss, parts):
            keep = (lax.broadcasted_iota(jnp.int32, (half, width), 1)
                    <= lax.broadcasted_iota(jnp.int32, (half, width), 0) + r0)
            ps.append(jnp.exp2(jnp.where(keep, s, NEG)).astype(BF16))
        ps_prev = [jnp.exp2(s).astype(BF16) for s in ss_prev]
        for p, (hh, r0, width) in zip(ps, parts):
            acc_s[hh, r0:r0 + half] = _dot(p, vext(hh, k0, width))
        for hh, p in enumerate(ps_prev):
            acc_s[hh] += _dot(p, vext(hh, k0_prev, t))

    def online_tile(kt, ms, masked):
        k0 = pl.multiple_of(kt * t, t)
        new_ms = []
        for hh in hs:
            s = scores(hh, k0, t, masked)
            m_new = jnp.maximum(ms[hh], jnp.max(s, axis=-1, keepdims=True))
            alpha = jnp.exp2(ms[hh] - m_new)
            p = jnp.exp2(s - m_new).astype(BF16)
            acc_s[hh] = alpha * acc_s[hh] + _dot(p, vext(hh, k0, t))
            new_ms.append(m_new)
        return tuple(new_ms)

    @pl.when(bounded)
    def _():
        @pl.when(qi > 0)
        def _():
            plain_diagonal(pl.multiple_of(qi * t, t), pl.multiple_of((qi - 1) * t, t))

        @pl.when(qi == 0)
        def _():
            plain_diagonal(pl.multiple_of(qi * t, t))

        def step(kt, carry):
            plain_tile(pl.multiple_of(kt * t, t), t)
            return carry

        lax.fori_loop(0, qi - 1, step, 0)

    @pl.when(jnp.logical_not(bounded))
    def _():
        acc_s[...] = jnp.zeros_like(acc_s)
        m0 = jnp.full((t, 1), NEG, F32)
        ms = lax.fori_loop(0, qi, lambda kt, ms: online_tile(kt, ms, False), (m0,) * heads)
        online_tile(qi, ms, True)

    for g in range(heads // 2):
        lanes = slice(g * LANES, (g + 1) * LANES)
        acc0 = acc_s[2 * g]
        acc1 = acc_s[2 * g + 1]
        num = jnp.where(lo, acc0, acc1)
        den = pltpu.roll(jnp.where(lo, acc1, acc0), FX_DH, axis=1)
        o_ref[0, :, lanes] = (num / den * _sigmoid(op_ref[0, :, lanes].astype(F32))).astype(o_ref.dtype)


def _fox_logit_bound(q_norm_g, k_norm_g):
    return (FX_DH ** 0.5 * LOG2E) * jnp.max(jnp.abs(q_norm_g)) * jnp.max(jnp.abs(k_norm_g))


def _fox_core(main, qa, ka, bounded, *, t=512, heads=8):
    b, s, _ = main.shape
    ng = FX_HEADS // heads
    w = heads * FX_DH
    return pl.pallas_call(
        functools.partial(_fox_kernel, t=t, heads=heads),
        grid=(b, ng, s // t),
        in_specs=[
            pl.BlockSpec(memory_space=pltpu.SMEM),
            pl.BlockSpec((1, heads, t, LANES), lambda i, p, j: (i, p, j, 0)),
            pl.BlockSpec((1, heads, s, LANES), lambda i, p, j: (i, p, 0, 0)),
            pl.BlockSpec((1, s, w), lambda i, p, j: (i, 0, 2 * ng + p)),
            pl.BlockSpec((1, t, w), lambda i, p, j: (i, j, 3 * ng + p)),
        ],
        out_specs=pl.BlockSpec((1, t, w), lambda i, p, j: (i, j, p)),
        out_shape=jax.ShapeDtypeStruct((b, s, FX_HEADS * FX_DH), BF16),
        scratch_shapes=[pltpu.VMEM((heads, t, LANES), F32)],
        compiler_params=_cparams(("parallel", "parallel", "arbitrary")),
        name="fox_core",
    )(bounded.astype(jnp.int32).reshape(1), qa, ka, main, main)


def _gla_kernel(main_ref, gl_ref, wgu_ref, bg_ref, ng_ref, o_ref, st_s, cum_s, k_s, *, nb):
    L = CHUNK
    H = GLA_HEADS
    SUB = GLA_SUB
    nk = H * GLA_DK
    nv = H * GLA_DV
    scale = GLA_DK ** -0.5

    @pl.when(pl.program_id(1) == 0)
    def _():
        st_s[...] = jnp.zeros_like(st_s)

    row = lax.broadcasted_iota(jnp.int32, (L, L), 0)
    col = lax.broadcasted_iota(jnp.int32, (L, L), 1)
    tri = (col <= row).astype(BF16)
    eye = (lax.broadcasted_iota(jnp.int32, (GLA_DK, GLA_DK), 0)
           == lax.broadcasted_iota(jnp.int32, (GLA_DK, GLA_DK), 1))

    w_hi, w_lo = _split_bf16(wgu_ref[...])
    cum_alls = []
    for bb in range(nb):
        g_hi, g_lo = _split_bf16(gl_ref[bb])
        logits = _dot(g_hi, w_hi) + _dot(g_hi, w_lo) + _dot(g_lo, w_hi)
        la = _log_sigmoid(logits + bg_ref[...]) * (1.0 / GLA_TAU)
        cum_alls.append(_tri_dot(tri, la))

    units = [(bb, h) for bb in range(nb) for h in range(H)]
    nsub = L // SUB
    half = SUB // 2
    qs, v_bfs, cums, decays = [], [], [], []

    inters, updates, a_offs = [], [], []
    for u, (bb, h) in enumerate(units):
        ks = slice(h * GLA_DK, (h + 1) * GLA_DK)
        cum = cum_alls[bb][:, ks]
        tot = cum[L - 1:L, :]
        q = main_ref[bb, :, ks].astype(F32) * scale
        k = main_ref[bb, :, nk + h * GLA_DK:nk + (h + 1) * GLA_DK].astype(F32)
        v_bf = main_ref[bb, :, 2 * nk + h * GLA_DV:2 * nk + (h + 1) * GLA_DV]
        inters.append(_dot((q * jnp.exp(cum)).astype(BF16), st_s[u].astype(BF16)))
        updates.append(_dot_tn((k * jnp.exp(tot - cum)).astype(BF16), v_bf))
        offs = []
        for i in range(1, nsub):
            rs = slice(i * SUB, (i + 1) * SUB)
            ref_pt = cum[i * SUB - 1:i * SUB, :]
            q_rel = (q[rs] * jnp.exp(cum[rs] - ref_pt)).astype(BF16)
            k_rel = (k[:i * SUB] * jnp.exp(ref_pt - cum[:i * SUB])).astype(BF16)
            offs.append(_dot_nt(q_rel, k_rel))
        a_offs.append(offs)
        tot_col = jnp.sum(jnp.where(eye, jnp.broadcast_to(tot, (GLA_DK, GLA_DK)), 0.0),
                          axis=-1, keepdims=True)
        decays.append(jnp.exp(tot_col))
        cum_s[u] = cum
        k_s[u] = k
        qs.append(q)
        v_bfs.append(v_bf)
        cums.append(cum)

    h_offs = [[_dot(a_offs[u][i - 1].astype(BF16), v_bfs[u][:i * SUB]) for i in range(1, nsub)]
              for u in range(len(units))]

    t_half = lax.broadcasted_iota(jnp.int32, (half, 1), 0)
    lane = lax.broadcasted_iota(jnp.int32, (1, LANES), 1)
    for u, (bb, h) in enumerate(units):
        vs = slice(h * GLA_DV, (h + 1) * GLA_DV)
        for i in range(nsub):
            rs = slice(i * SUB, (i + 1) * SUB)
            q_top, q_bot = qs[u][i * SUB:i * SUB + half], qs[u][i * SUB + half:(i + 1) * SUB]
            c_top, c_bot = cums[u][i * SUB:i * SUB + half], cums[u][i * SUB + half:(i + 1) * SUB]
            a_top = jnp.zeros((half, LANES), F32)
            a_bot = jnp.zeros((half, LANES), F32)
            for j in range(SUB):
                rj = i * SUB + j
                kj = k_s[u, rj:rj + 1, :]
                cj = cum_s[u, rj:rj + 1, :]
                if j < half:
                    e_top = jnp.exp(jnp.where(t_half >= j, c_top - cj, NEG))
                    a_top = jnp.where(lane == j, jnp.sum(q_top * kj * e_top, axis=-1, keepdims=True), a_top)
                    e_bot = jnp.exp(c_bot - cj)
                else:
                    e_bot = jnp.exp(jnp.where(t_half >= j - half, c_bot - cj, NEG))
                a_bot = jnp.where(lane == j, jnp.sum(q_bot * kj * e_bot, axis=-1, keepdims=True), a_bot)
            a_diag = jnp.concatenate([a_top, a_bot], axis=0)[:, :SUB].astype(BF16)
            hb = inters[u][rs] + _dot(a_diag, v_bfs[u][rs])
            if i > 0:
                hb = hb + h_offs[u][i - 1]
            hn = hb * lax.rsqrt(jnp.mean(hb * hb, axis=-1, keepdims=True) + EPS)
            r = main_ref[bb, rs, 2 * nk + nv + h * GLA_DV:2 * nk + nv + (h + 1) * GLA_DV].astype(F32)
            out = hn * ng_ref[:, vs] * (r * _sigmoid(r))
            o_ref[bb, rs, vs] = out.astype(o_ref.dtype)

    for u in range(len(units)):
        st_s[u] = decays[u] * st_s[u] + updates[u]


def _gla_core(main, g_low, w_gate_up, b_gate, norm_g, *, nb=2):
    b, s, _ = main.shape
    nc = s // CHUNK
    nk = GLA_HEADS * GLA_DK
    nv = GLA_HEADS * GLA_DV
    wgu = jnp.zeros((LANES, nk), F32).at[:GLA_RANK].set(w_gate_up)
    return pl.pallas_call(
        functools.partial(_gla_kernel, nb=nb),
        grid=(b // nb, nc),
        in_specs=[
            pl.BlockSpec((nb, CHUNK, main.shape[2]), lambda i, c: (i, c, 0)),
            pl.BlockSpec((nb, CHUNK, LANES), lambda i, c: (i, c, 0)),
            pl.BlockSpec((LANES, nk), lambda i, c: (0, 0)),
            pl.BlockSpec((1, nk), lambda i, c: (0, 0)),
            pl.BlockSpec((1, nv), lambda i, c: (0, 0)),
        ],
        out_specs=pl.BlockSpec((nb, CHUNK, nv), lambda i, c: (i, c, 0)),
        out_shape=jax.ShapeDtypeStruct((b, s, nv), BF16),
        scratch_shapes=[
            pltpu.VMEM((nb * GLA_HEADS, GLA_DK, GLA_DV), F32),
            pltpu.VMEM((nb * GLA_HEADS, CHUNK, GLA_DK), F32),
            pltpu.VMEM((nb * GLA_HEADS, CHUNK, GLA_DK), F32),
        ],
        compiler_params=_cparams(("parallel", "arbitrary")),
        name="gla_core",
    )(main, g_low, wgu, b_gate.reshape(1, -1), norm_g.reshape(1, -1))


def _gated_in_proj(x, g, w_in, n_gate):
    n_main = w_in.shape[1] - n_gate
    w_bf = w_in.astype(BF16)
    w_gate = jnp.pad(w_bf[:, n_main:], ((0, 0), (0, LANES - n_gate)))
    return _in_proj(x, g, w_bf, n_main, w_gate)


def kernel(x, mix_norm_g, ffn_norm_g, ffn_w_gate, ffn_w_up, ffn_w_down, ml_w_in, ml_b_if, ml_norm_g, ml_w_out, sb_w_in, sb_w_out, fx_w_in, fx_b_f, fx_q_norm_g, fx_k_norm_g, fx_w_out, gla_w_in, gla_w_gate_up, gla_b_gate, gla_norm_g, gla_w_out):
    b, s, d = x.shape
    depth = mix_norm_g.shape[0]
    h = x.reshape(b * s, d)
    ffn_gate, ffn_up, ffn_down = (w.astype(BF16) for w in (ffn_w_gate, ffn_w_up, ffn_w_down))
    for layer in range(depth):
        kind = layer % 4
        j = layer // 4
        g_mix = mix_norm_g[layer].reshape(1, d)
        if kind == 0:
            main, gates = _gated_in_proj(h, g_mix, ml_w_in[j], 2 * ML_HEADS)
            a = _mlstm_core(main.reshape(b, s, -1), gates.reshape(b, s, LANES), ml_b_if[j], ml_norm_g[j])
            w_out = ml_w_out[j]
        elif kind == 1:
            nq = SB_HEADS * SB_DH
            w_sb = jnp.concatenate([sb_w_in[j][:, :nq] * (SB_DH ** -0.5 * LOG2E), sb_w_in[j][:, nq:]], axis=1)
            (main,) = _in_proj(h, g_mix, w_sb.astype(BF16), w_sb.shape[1])
            a = _sb_core(main.reshape(b, s, -1))
            w_out = sb_w_out[j]
        elif kind == 2:
            main, gates = _gated_in_proj(h, g_mix, fx_w_in[j], FX_HEADS)
            main = main.reshape(b, s, -1)
            bound = _fox_logit_bound(fx_q_norm_g[j], fx_k_norm_g[j])
            bounded = bound <= FX_MAX_SHIFT
            qa, ka = _fox_prep(main, gates.reshape(b, s, LANES), fx_b_f[j], fx_q_norm_g[j], fx_k_norm_g[j],
                               jnp.where(bounded, bound, 0.0))
            a = _fox_core(main, qa, ka, bounded)
            w_out = fx_w_out[j]
        else:
            main, gates = _gated_in_proj(h, g_mix, gla_w_in[j], GLA_RANK)
            a = _gla_core(main.reshape(b, s, -1), gates.reshape(b, s, LANES), gla_w_gate_up[j], gla_b_gate[j], gla_norm_g[j])
            w_out = gla_w_out[j]
        h = _out_ffn(h, a.reshape(b * s, -1), w_out.astype(BF16), ffn_norm_g[layer].reshape(1, d),
                     ffn_gate, ffn_up, ffn_down, layer)
    return h.reshape(b, s, d)
```

```python
import functools

import jax
import jax.numpy as jnp
import numpy as np
from jax import lax
from jax.experimental import pallas as pl
from jax.experimental.pallas import tpu as pltpu

F32 = jnp.float32
BF16 = jnp.bfloat16

EPS = 1e-6
CHUNK = 64
ML_HEADS, ML_DQK, ML_DV = 8, 64, 128
SB_HEADS, SB_DH = 16, 64
FX_HEADS, FX_DH = 16, 64
GLA_HEADS, GLA_DK, GLA_DV = 4, 128, 256
GLA_RANK = 16
GLA_TAU = 16.0
GLA_SUB = 16
NEG = -1e30
LOG2E = 1.4426950408889634
FX_MAX_SHIFT = 60.0
FX_TERMS = 3
FX_ONE_LANE = FX_TERMS * FX_HEADS
SB_LOG2_ZERO = -150.0
SB_Z_CLAMP = 100.0
LANES = 128
VMEM_LIMIT = 56 * 1024 * 1024


def _cparams(sem):
    return pltpu.CompilerParams(dimension_semantics=sem, vmem_limit_bytes=VMEM_LIMIT)


def _resident(shape):
    return pl.BlockSpec(shape, lambda i: (0,) * len(shape), pipeline_mode=pl.Buffered(1))


def _log_sigmoid(x):
    return jnp.minimum(x, 0.0) - jnp.log(1.0 + jnp.exp(-jnp.abs(x)))


def _sigmoid(x):
    return 1.0 / (1.0 + jnp.exp(-x))


def _split_bf16(x):
    hi = x.astype(BF16)
    lo = (x - hi.astype(F32)).astype(BF16)
    return hi, lo


def _dot(a, b):
    return jnp.dot(a, b, preferred_element_type=F32)


def _dot_nt(a, b):
    return lax.dot_general(a, b, (((1,), (1,)), ((), ())), preferred_element_type=F32)


def _dot_tn(a, b):
    return lax.dot_general(a, b, (((0,), (0,)), ((), ())), preferred_element_type=F32)


def _tri_dot(tri, x):
    hi, lo = _split_bf16(x)
    return _dot(tri, hi) + _dot(tri, lo)


def _dot_tri(x, tri):
    hi, lo = _split_bf16(x)
    return _dot(hi, tri) + _dot(lo, tri)


def _in_proj_kernel(x_ref, g_ref, w_ref, *rest, tr, tn):
    wg_ref, main_ref, gate_ref = rest if len(rest) == 3 else (None, rest[0], None)

    def normed(rows):
        x = x_ref[rows, :]
        ms = jnp.mean(x * x, axis=-1, keepdims=True)
        return (x * lax.rsqrt(ms + EPS) * g_ref[...]).astype(BF16)

    blocks = [slice(r * tr, (r + 1) * tr) for r in range(x_ref.shape[0] // tr)]
    u = normed(blocks[0])
    for r, rows in enumerate(blocks):
        u_next = None
        for c in range(main_ref.shape[-1] // tn):
            cs = slice(c * tn, (c + 1) * tn)
            main_ref[rows, cs] = _dot(u, w_ref[:, cs]).astype(main_ref.dtype)
            if c == 0 and r + 1 < len(blocks):
                u_next = normed(blocks[r + 1])
        if gate_ref is not None:
            gate_ref[rows, :] = _dot(u, wg_ref[...])
        u = u_next


def _in_proj(x, g, w_in, wn, w_gate=None, *, tm=1024, tr=256, tn=512):
    n, d = x.shape
    in_specs = [pl.BlockSpec((tm, d), lambda i: (i, 0)), _resident((1, d)), _resident(w_in.shape)]
    out_specs = [pl.BlockSpec((tm, wn), lambda i: (i, 0))]
    out_shape = [jax.ShapeDtypeStruct((n, wn), BF16)]
    operands = [x, g, w_in]
    if w_gate is not None:
        in_specs.append(_resident((d, LANES)))
        out_specs.append(pl.BlockSpec((tm, LANES), lambda i: (i, 0)))
        out_shape.append(jax.ShapeDtypeStruct((n, LANES), F32))
        operands.append(w_gate)
    return pl.pallas_call(
        functools.partial(_in_proj_kernel, tr=tr, tn=tn),
        grid=(n // tm,),
        in_specs=in_specs,
        out_specs=out_specs,
        out_shape=out_shape,
        compiler_params=_cparams(("parallel",)),
        name="in_proj",
    )(*operands)


def _out_ffn_kernel(h_ref, a_ref, wo_ref, g_ref, wg_ref, wu_ref, wd_ref, o_ref, *, tf):
    h1 = h_ref[...] + _dot(a_ref[...], wo_ref[...])
    ms = jnp.mean(h1 * h1, axis=-1, keepdims=True)
    u = (h1 * lax.rsqrt(ms + EPS) * g_ref[...]).astype(BF16)
    acc = h1
    for c in range(wg_ref.shape[1] // tf):
        cs = slice(c * tf, (c + 1) * tf)
        gt = _dot(u, wg_ref[:, cs])
        up = _dot(u, wu_ref[:, cs])
        hid = (gt * _sigmoid(gt) * up).astype(BF16)
        acc = acc + _dot(hid, wd_ref[cs, :])
    o_ref[...] = acc


def _out_ffn(h, a, w_out, g, w_gate, w_up, w_down, layer, *, tm=1024, tf=256):
    n, d = h.shape
    dff = w_gate.shape[2]

    def slab(rows, cols):
        return pl.BlockSpec((None, rows, cols), lambda i: (layer, 0, 0), pipeline_mode=pl.Buffered(1))

    return pl.pallas_call(
        functools.partial(_out_ffn_kernel, tf=tf),
        grid=(n // tm,),
        in_specs=[
            pl.BlockSpec((tm, d), lambda i: (i, 0)),
            pl.BlockSpec((tm, d), lambda i: (i, 0)),
            _resident((d, d)),
            _resident((1, d)),
            slab(d, dff),
            slab(d, dff),
            slab(dff, d),
        ],
        out_specs=pl.BlockSpec((tm, d), lambda i: (i, 0)),
        out_shape=jax.ShapeDtypeStruct((n, d), F32),
        compiler_params=_cparams(("parallel",)),
        name="out_ffn",
    )(h, a, w_out, g, w_gate, w_up, w_down)


def _mlstm_select_matrices():
    sel = np.zeros((2, LANES, ML_HEADS * LANES), np.float32)
    for h in range(ML_HEADS):
        sel[0, h, h * LANES:(h + 1) * LANES] = 1.0
        sel[1, ML_HEADS + h, h * LANES:(h + 1) * LANES] = 1.0
    return jnp.asarray(sel, BF16)


def _mlstm_kernel(main_ref, gc_ref, gr_ref, bc_ref, br_ref, ng_ref, sel_ref, o_ref, c_s, m_s, *, nb):
    @pl.when(pl.program_id(1) == 0)
    def _():
        c_s[...] = jnp.zeros_like(c_s)
        m_s[...] = jnp.zeros_like(m_s)

    staged = [_mlstm_gates(bb, main_ref, gc_ref, gr_ref, bc_ref, br_ref, sel_ref, c_s) for bb in range(nb)]
    for bb in range(nb):
        _mlstm_heads(bb, staged[bb], main_ref, ng_ref, o_ref, c_s, m_s)


def _mlstm_gates(bb, main_ref, gc_ref, gr_ref, bc_ref, br_ref, sel_ref, c_s):
    L = CHUNK
    H = ML_HEADS
    row = lax.broadcasted_iota(jnp.int32, (L, L), 0)
    col = lax.broadcasted_iota(jnp.int32, (L, L), 1)
    tri = (col <= row).astype(BF16)
    tri_t = (row <= col).astype(BF16)

    gc = gc_ref[bb] + bc_ref[...]
    gr = gr_ref[bb, 0] + br_ref[...]
    cum_c = _tri_dot(tri, _log_sigmoid(gc))
    li_r = gr[0:H]
    cum_r = _dot_tri(_log_sigmoid(gr[H:2 * H]), tri_t)

    li_b_all = _dot_tri(gc, sel_ref[0])
    cum_b_all = _dot_tri(cum_c, sel_ref[1])
    nq = H * ML_DQK
    qs = [main_ref[bb, :, h * ML_DQK:(h + 1) * ML_DQK] for h in range(H)]
    ks = [main_ref[bb, :, nq + h * ML_DQK:nq + (h + 1) * ML_DQK] for h in range(H)]
    s_qks = [_dot_nt(qs[h], ks[h]) for h in range(H)]
    inters = [_dot(qs[h], c_s[bb * H + h].astype(BF16)) for h in range(H)]
    return cum_r, li_r, cum_b_all, li_b_all, ks, s_qks, inters


def _mlstm_heads(bb, staged, main_ref, ng_ref, o_ref, c_s, m_s):
    L = CHUNK
    H = ML_HEADS
    scale = ML_DQK ** -0.5
    nq = H * ML_DQK
    nv = H * ML_DV
    cum_r, li_r, cum_b_all, li_b_all, ks, s_qks, inters = staged
    causal = lax.broadcasted_iota(jnp.int32, (L, L), 1) <= lax.broadcasted_iota(jnp.int32, (L, L), 0)
    ones_blk = jnp.ones((L, ML_DV), BF16)

    for h in range(H):
        v = main_ref[bb, :, 2 * nq + h * ML_DV:2 * nq + (h + 1) * ML_DV]
        op = main_ref[bb, :, 2 * nq + nv + h * ML_DV:2 * nq + nv + (h + 1) * ML_DV]
        vext = jnp.concatenate([v, ones_blk], axis=1)
        cum_b = cum_b_all[:, h * LANES:(h + 1) * LANES]
        li_b = li_b_all[:, h * LANES:(h + 1) * LANES]
        cum_r1 = cum_r[h:h + 1, :]
        li_r1 = li_r[h:h + 1, :]
        m_st = m_s[bb * H + h][:, 0:1]
        cext = c_s[bb * H + h]

        log_d = jnp.where(causal, cum_b[:, :L] - cum_r1 + li_r1, NEG)
        m_t = jnp.maximum(cum_b + m_st, jnp.max(log_d, axis=-1, keepdims=True))
        dmat = jnp.exp(log_d - m_t[:, :L])
        w_inter = jnp.exp(cum_b + m_st - m_t)
        sc = (s_qks[h] * (dmat * scale)).astype(BF16)
        inter = inters[h]
        intra = _dot(sc, vext)
        num = w_inter * inter[:, :ML_DV] + intra[:, :ML_DV]
        den = w_inter * inter[:, ML_DV:] + intra[:, ML_DV:]
        hh = num / jnp.maximum(jnp.abs(den), jnp.exp(-m_t))
        hn = hh * lax.rsqrt(jnp.mean(hh * hh, axis=-1, keepdims=True) + EPS)
        out = hn * ng_ref[:, h * ML_DV:(h + 1) * ML_DV] * _sigmoid(op.astype(F32))
        o_ref[bb, :, h * ML_DV:(h + 1) * ML_DV] = out.astype(o_ref.dtype)

        tot_f = cum_r1[:, L - 1:L]
        log_w_r = tot_f - cum_r1 + li_r1
        m_new = jnp.maximum(tot_f + m_st, jnp.max(log_w_r, axis=-1, keepdims=True))
        decay = jnp.exp(tot_f + m_st - m_new)
        w_b = jnp.exp(tot_f - cum_b + li_b - m_new) * scale
        wk = (ks[h].astype(F32) * w_b[:, :ML_DQK]).astype(BF16)
        c_s[bb * H + h] = decay * cext + _dot_tn(wk, vext)
        m_s[bb * H + h] = jnp.broadcast_to(m_new, (1, LANES))


def _mlstm_core(main, gates, b_if, norm_g, *, nb=2):
    b, s, _ = main.shape
    nc = s // CHUNK
    h = ML_HEADS
    gr = gates[:, :, :2 * h].reshape(b, nc, CHUNK, 2 * h).transpose(0, 1, 3, 2)
    bc = jnp.zeros((1, LANES), F32).at[0, :2 * h].set(b_if)
    br = b_if.reshape(2 * h, 1)
    return pl.pallas_call(
        functools.partial(_mlstm_kernel, nb=nb),
        grid=(b // nb, nc),
        in_specs=[
            pl.BlockSpec((nb, CHUNK, main.shape[2]), lambda i, c: (i, c, 0)),
            pl.BlockSpec((nb, CHUNK, LANES), lambda i, c: (i, c, 0)),
            pl.BlockSpec((nb, 1, 2 * h, CHUNK), lambda i, c: (i, c, 0, 0)),
            pl.BlockSpec((1, LANES), lambda i, c: (0, 0)),
            pl.BlockSpec((2 * h, 1), lambda i, c: (0, 0)),
            pl.BlockSpec((1, h * ML_DV), lambda i, c: (0, 0)),
            pl.BlockSpec((2, LANES, h * LANES), lambda i, c: (0, 0, 0)),
        ],
        out_specs=pl.BlockSpec((nb, CHUNK, h * ML_DV), lambda i, c: (i, c, 0)),
        out_shape=jax.ShapeDtypeStruct((b, s, h * ML_DV), BF16),
        scratch_shapes=[
            pltpu.VMEM((nb * h, ML_DQK, 2 * ML_DV), F32),
            pltpu.VMEM((nb * h, 1, LANES), F32),
        ],
        compiler_params=_cparams(("parallel", "arbitrary")),
        name="mlstm_core",
    )(main, gates, gr, bc, br, norm_g.reshape(1, -1), _mlstm_select_matrices())


def _sb_kernel(q_ref, k_ref, v_ref, o_ref, acc_s, carry_s, *, t, heads):
    qi = pl.program_id(2)
    lane = lax.broadcasted_iota(jnp.int32, (1, LANES), 1)
    lo = lane < SB_DH
    sels = (lo, jnp.logical_not(lo))
    row = lax.broadcasted_iota(jnp.int32, (t, t), 0)
    col = lax.broadcasted_iota(jnp.int32, (t, t), 1)
    valid = col < row
    lower = (row > col).astype(BF16)
    groups = [slice(g * LANES, (g + 1) * LANES) for g in range(heads // 2)]
    zero = jnp.zeros((), BF16)
    qms = [jnp.where(sels[h % 2], q_ref[0, :, groups[h // 2]], zero) for h in range(heads)]

    def tiles(kts, carries, from_diagonal):
        hs = range(heads)
        k0s = [pl.multiple_of(kt * t, t) for kt in kts]
        masks = [from_diagonal and i == 0 for i in range(len(kts))]
        zs = [[_dot_nt(qms[h], k_ref[0, pl.ds(k0, t), groups[h // 2]]) for h in hs] for k0 in k0s]
        nlps = []
        for zt, masked in zip(zs, masks):
            row_nlps = []
            for h in hs:
                nlp = jnp.maximum(jnp.log2(1.0 + jnp.exp2(jnp.minimum(zt[h], SB_Z_CLAMP))), zt[h])
                row_nlps.append(jnp.where(valid, nlp, 0.0) if masked else nlp)
            nlps.append(row_nlps)
        laters = [[_dot(nt[h].astype(BF16), lower) for h in hs] for nt in nlps]
        for zt, nt, lt, masked, k0 in zip(zs, nlps, laters, masks, k0s):
            weights = []
            for h in hs:
                a = jnp.exp2(zt[h] - nt[h] - lt[h] + carries[h])
                weights.append((jnp.where(valid, a, 0.0) if masked else a).astype(BF16))
            for g, lanes in enumerate(groups):
                vv = v_ref[0, pl.ds(k0, t), lanes]
                contrib = (_dot(weights[2 * g], jnp.where(sels[0], vv, zero))
                           + _dot(weights[2 * g + 1], jnp.where(sels[1], vv, zero)))
                if masked:
                    acc_s[:, lanes] = contrib
                else:
                    acc_s[:, lanes] += contrib
            carries = tuple(carries[h] - jnp.sum(nt[h], axis=-1, keepdims=True) for h in hs)
        return carries

    zero_carries = (jnp.zeros((t, 1), F32),) * heads

    @pl.when(qi > 0)
    def _():
        for h, c in enumerate(tiles([qi, qi - 1], zero_carries, True)):
            carry_s[h] = c

    @pl.when(qi == 0)
    def _():
        for h, c in enumerate(tiles([qi], zero_carries, True)):
            carry_s[h] = c

    carries = tuple(carry_s[h] for h in range(heads))

    def live(carries):
        return jnp.max(functools.reduce(jnp.maximum, carries))

    def cond(st):
        return jnp.logical_and(st[0] >= 0, st[1] > SB_LOG2_ZERO)

    def body(st):
        carries = tiles([st[0]], st[2], False)
        return st[0] - 1, live(carries), carries

    lax.while_loop(cond, body, (qi - 2, live(carries), carries))
    o_ref[0] = acc_s[...].astype(o_ref.dtype)


def _sb_core(main, *, t=256, heads=16):
    b, s, _ = main.shape
    ng = SB_HEADS // heads
    w = heads * SB_DH
    return pl.pallas_call(
        functools.partial(_sb_kernel, t=t, heads=heads),
        grid=(b, ng, s // t),
        in_specs=[
            pl.BlockSpec((1, t, w), lambda i, p, j: (i, j, p)),
            pl.BlockSpec((1, s, w), lambda i, p, j: (i, 0, ng + p)),
            pl.BlockSpec((1, s, w), lambda i, p, j: (i, 0, 2 * ng + p)),
        ],
        out_specs=pl.BlockSpec((1, t, w), lambda i, p, j: (i, j, p)),
        out_shape=jax.ShapeDtypeStruct((b, s, SB_HEADS * SB_DH), BF16),
        scratch_shapes=[pltpu.VMEM((t, w), F32), pltpu.VMEM((heads, t, 1), F32)],
        compiler_params=_cparams(("parallel", "parallel", "arbitrary")),
        name="sb_core",
    )(main, main, main)


def _fox_scatter_matrices():
    pq = np.zeros((FX_HEADS // 2, LANES, 2 * LANES), np.float32)
    pk = np.zeros((FX_HEADS // 2, LANES, 2 * LANES), np.float32)
    for p in range(FX_HEADS // 2):
        for hh, base in ((0, FX_DH), (1, LANES)):
            head = 2 * p + hh
            for i in range(FX_TERMS):
                pq[p, FX_HEADS * i + head, base + i] = 1.0
                pq[p, FX_ONE_LANE, base + FX_TERMS + i] = 1.0
                pk[p, FX_ONE_LANE, base + i] = 1.0
                pk[p, FX_HEADS * i + head, base + FX_TERMS + i] = -1.0
    return jnp.asarray(pq, BF16), jnp.asarray(pk, BF16)


def _fox_prep_kernel(main_ref, fp_ref, bf_ref, gq_ref, gk_ref, sh_ref, pq_ref, pk_ref, qa_ref, ka_ref,
                     carry_s, *, ts):
    @pl.when(pl.program_id(1) == 0)
    def _():
        carry_s[...] = jnp.zeros_like(carry_s)

    row = lax.broadcasted_iota(jnp.int32, (ts, ts), 0)
    col = lax.broadcasted_iota(jnp.int32, (ts, ts), 1)
    tri = (col <= row).astype(BF16)
    lf = _log_sigmoid(fp_ref[0] + bf_ref[...])
    cum = _tri_dot(tri, lf) + carry_s[...]
    carry_s[...] = cum[ts - 1:ts, :]

    lane = lax.broadcasted_iota(jnp.int32, (1, LANES), 1)
    lo = lane < FX_DH
    head_lane = lane < FX_HEADS

    def packed(c):
        c = jnp.where(head_lane, c, 0.0)
        hi = c.astype(BF16).astype(F32)
        mid = (c - hi).astype(BF16).astype(F32)
        low = (c - hi - mid).astype(BF16).astype(F32)
        out = hi + pltpu.roll(mid, FX_HEADS, axis=1) + pltpu.roll(low, 2 * FX_HEADS, axis=1)
        return jnp.where(lane == FX_ONE_LANE, 1.0, out).astype(BF16)

    pack_q = packed(cum * LOG2E - sh_ref[...])
    pack_k = packed(cum * LOG2E)

    n_q = FX_HEADS * FX_DH // LANES
    for c in range(2 * n_q):
        is_q = c < n_q
        p = c % n_q
        xc = main_ref[0, :, c * LANES:(c + 1) * LANES].astype(F32)
        x2 = xc * xc
        s_lo = jnp.sum(jnp.where(lo, x2, 0.0), axis=-1, keepdims=True)
        s_hi = jnp.sum(jnp.where(lo, 0.0, x2), axis=-1, keepdims=True)
        r = jnp.where(lo, lax.rsqrt(s_lo / FX_DH + EPS), lax.rsqrt(s_hi / FX_DH + EPS))
        xn = xc * r * (gq_ref[...] if is_q else gk_ref[...])
        aug = _dot(pack_q, pq_ref[p]) if is_q else _dot(pack_k, pk_ref[p])
        out_ref = qa_ref if is_q else ka_ref
        out_ref[0, 2 * p] = jnp.where(lo, xn, aug[:, :LANES]).astype(out_ref.dtype)
        out_ref[0, 2 * p + 1] = jnp.where(lo, aug[:, LANES:], xn).astype(out_ref.dtype)


def _fox_prep(main, f_pre, b_f, q_norm_g, k_norm_g, shift, *, ts=256):
    b, s, _ = main.shape
    w = 2 * FX_HEADS * FX_DH
    bf = jnp.zeros((1, LANES), F32).at[0, :FX_HEADS].set(b_f)
    sh = jnp.full((1, LANES), shift, F32)
    gq = jnp.tile(q_norm_g, 2).reshape(1, LANES) * (FX_DH ** -0.5 * LOG2E)
    gk = jnp.tile(k_norm_g, 2).reshape(1, LANES)
    pq, pk = _fox_scatter_matrices()
    head_rows = pl.BlockSpec((1, FX_HEADS, ts, LANES), lambda i, j: (i, 0, j, 0))
    const_row = pl.BlockSpec((1, LANES), lambda i, j: (0, 0))
    scatter = pl.BlockSpec(pq.shape, lambda i, j: (0, 0, 0))
    return pl.pallas_call(
        functools.partial(_fox_prep_kernel, ts=ts),
        grid=(b, s // ts),
        in_specs=[
            pl.BlockSpec((1, ts, w), lambda i, j: (i, j, 0)),
            pl.BlockSpec((1, ts, LANES), lambda i, j: (i, j, 0)),
            const_row, const_row, const_row, const_row, scatter, scatter,
        ],
        out_specs=[head_rows, head_rows],
        out_shape=[jax.ShapeDtypeStruct((b, FX_HEADS, s, LANES), BF16)] * 2,
        scratch_shapes=[pltpu.VMEM((1, LANES), F32)],
        compiler_params=_cparams(("parallel", "arbitrary")),
        name="fox_prep",
    )(main, f_pre, bf, gq, gk, sh, pq, pk)


def _fox_kernel(bounded_ref, q_ref, k_ref, v_ref, op_ref, o_ref, acc_s, *, t, heads):
    qi = pl.program_id(2)
    hs = range(heads)
    lane = lax.broadcasted_iota(jnp.int32, (1, LANES), 1)
    lo = lane < FX_DH
    sels = (lo, jnp.logical_not(lo))
    row = lax.broadcasted_iota(jnp.int32, (t, t), 0)
    col = lax.broadcasted_iota(jnp.int32, (t, t), 1)
    causal = col <= row
    qs = [q_ref[0, hh] for hh in hs]
    bounded = bounded_ref[0] != 0

    def scores(hh, k0, width, masked):
        s = _dot_nt(qs[hh], k_ref[0, hh, pl.ds(k0, width), :])
        return jnp.where(causal, s, NEG) if masked else s

    def vext(hh, k0, width):
        v2 = v_ref[0, pl.ds(k0, width), (hh // 2) * LANES:(hh // 2 + 1) * LANES]
        return jnp.where(sels[hh % 2], v2, jnp.ones((), BF16))

    def plain_tile(k0, width):
        ss = [scores(hh, k0, width, False) for hh in hs]
        ps = [jnp.exp2(s).astype(BF16) for s in ss]
        for hh in hs:
            acc_s[hh] += _dot(ps[hh], vext(hh, k0, width))

    def plain_diagonal(k0, k0_prev=None):
        half = t // 2
        parts = [(hh, r0, width) for r0, width in ((0, half), (half, t)) for hh in hs]
        ss = [_dot_nt(qs[hh][r0:r0 + half], k_ref[0, hh, pl.ds(k0, width), :]) for hh, r0, width in parts]
        ss_prev = [] if k0_prev is None else [scores(hh, k0_prev, t, False) for hh in hs]
        ps = []
        for s, (hh, r0, width) in zip(ss, parts):
            keep = (lax.broadcasted_iota(jnp.int32, (half, width), 1)
                    <= lax.broadcasted_iota(jnp.int32, (half, width), 0) + r0)
            ps.append(jnp.exp2(jnp.where(keep, s, NEG)).astype(BF16))
        ps_prev = [jnp.exp2(s).astype(BF16) for s in ss_prev]
        for p, (hh, r0, width) in zip(ps, parts):
            acc_s[hh, r0:r0 + half] = _dot(p, vext(hh, k0, width))
        for hh, p in enumerate(ps_prev):
            acc_s[hh] += _dot(p, vext(hh, k0_prev, t))

    def online_tile(kt, ms, masked):
        k0 = pl.multiple_of(kt * t, t)
        new_ms = []
        for hh in hs:
            s = scores(hh, k0, t, masked)
            m_new = jnp.maximum(ms[hh], jnp.max(s, axis=-1, keepdims=True))
            alpha = jnp.exp2(ms[hh] - m_new)
            p = jnp.exp2(s - m_new).astype(BF16)
            acc_s[hh] = alpha * acc_s[hh] + _dot(p, vext(hh, k0, t))
            new_ms.append(m_new)
        return tuple(new_ms)

    @pl.when(bounded)
    def _():
        @pl.when(qi > 0)
        def _():
            plain_diagonal(pl.multiple_of(qi * t, t), pl.multiple_of((qi - 1) * t, t))

        @pl.when(qi == 0)
        def _():
            plain_diagonal(pl.multiple_of(qi * t, t))

        def step(kt, carry):
            plain_tile(pl.multiple_of(kt * t, t), t)
            return carry

        lax.fori_loop(0, qi - 1, step, 0)

    @pl.when(jnp.logical_not(bounded))
    def _():
        acc_s[...] = jnp.zeros_like(acc_s)
        m0 = jnp.full((t, 1), NEG, F32)
        ms = lax.fori_loop(0, qi, lambda kt, ms: online_tile(kt, ms, False), (m0,) * heads)
        online_tile(qi, ms, True)

    for g in range(heads // 2):
        lanes = slice(g * LANES, (g + 1) * LANES)
        acc0 = acc_s[2 * g]
        acc1 = acc_s[2 * g + 1]
        num = jnp.where(lo, acc0, acc1)
        den = pltpu.roll(jnp.where(lo, acc1, acc0), FX_DH, axis=1)
        o_ref[0, :, lanes] = (num / den * _sigmoid(op_ref[0, :, lanes].astype(F32))).astype(o_ref.dtype)


def _fox_logit_bound(q_norm_g, k_norm_g):
    return (FX_DH ** 0.5 * LOG2E) * jnp.max(jnp.abs(q_norm_g)) * jnp.max(jnp.abs(k_norm_g))


def _fox_core(main, qa, ka, bounded, *, t=512, heads=8):
    b, s, _ = main.shape
    ng = FX_HEADS // heads
    w = heads * FX_DH
    return pl.pallas_call(
        functools.partial(_fox_kernel, t=t, heads=heads),
        grid=(b, ng, s // t),
        in_specs=[
            pl.BlockSpec(memory_space=pltpu.SMEM),
            pl.BlockSpec((1, heads, t, LANES), lambda i, p, j: (i, p, j, 0)),
            pl.BlockSpec((1, heads, s, LANES), lambda i, p, j: (i, p, 0, 0)),
            pl.BlockSpec((1, s, w), lambda i, p, j: (i, 0, 2 * ng + p)),
            pl.BlockSpec((1, t, w), lambda i, p, j: (i, j, 3 * ng + p)),
        ],
        out_specs=pl.BlockSpec((1, t, w), lambda i, p, j: (i, j, p)),
        out_shape=jax.ShapeDtypeStruct((b, s, FX_HEADS * FX_DH), BF16),
        scratch_shapes=[pltpu.VMEM((heads, t, LANES), F32)],
        compiler_params=_cparams(("parallel", "parallel", "arbitrary")),
        name="fox_core",
    )(bounded.astype(jnp.int32).reshape(1), qa, ka, main, main)


def _gla_kernel(main_ref, gl_ref, wgu_ref, bg_ref, ng_ref, o_ref, st_s, cum_s, k_s, *, nb):
    L = CHUNK
    H = GLA_HEADS
    SUB = GLA_SUB
    nk = H * GLA_DK
    nv = H * GLA_DV
    scale = GLA_DK ** -0.5

    @pl.when(pl.program_id(1) == 0)
    def _():
        st_s[...] = jnp.zeros_like(st_s)

    row = lax.broadcasted_iota(jnp.int32, (L, L), 0)
    col = lax.broadcasted_iota(jnp.int32, (L, L), 1)
    tri = (col <= row).astype(BF16)
    eye = (lax.broadcasted_iota(jnp.int32, (GLA_DK, GLA_DK), 0)
           == lax.broadcasted_iota(jnp.int32, (GLA_DK, GLA_DK), 1))

    w_hi, w_lo = _split_bf16(wgu_ref[...])
    cum_alls = []
    for bb in range(nb):
        g_hi, g_lo = _split_bf16(gl_ref[bb])
        logits = _dot(g_hi, w_hi) + _dot(g_hi, w_lo) + _dot(g_lo, w_hi)
        la = _log_sigmoid(logits + bg_ref[...]) * (1.0 / GLA_TAU)
        cum_alls.append(_tri_dot(tri, la))

    units = [(bb, h) for bb in range(nb) for h in range(H)]
    nsub = L // SUB
    half = SUB // 2
    qs, v_bfs, cums, decays = [], [], [], []

    inters, updates, a_offs = [], [], []
    for u, (bb, h) in enumerate(units):
        ks = slice(h * GLA_DK, (h + 1) * GLA_DK)
        cum = cum_alls[bb][:, ks]
        tot = cum[L - 1:L, :]
        q = main_ref[bb, :, ks].astype(F32) * scale
        k = main_ref[bb, :, nk + h * GLA_DK:nk + (h + 1) * GLA_DK].astype(F32)
        v_bf = main_ref[bb, :, 2 * nk + h * GLA_DV:2 * nk + (h + 1) * GLA_DV]
        inters.append(_dot((q * jnp.exp(cum)).astype(BF16), st_s[u].astype(BF16)))
        updates.append(_dot_tn((k * jnp.exp(tot - cum)).astype(BF16), v_bf))
        offs = []
        for i in range(1, nsub):
            rs = slice(i * SUB, (i + 1) * SUB)
            ref_pt = cum[i * SUB - 1:i * SUB, :]
            q_rel = (q[rs] * jnp.exp(cum[rs] - ref_pt)).astype(BF16)
            k_rel = (k[:i * SUB] * jnp.exp(ref_pt - cum[:i * SUB])).astype(BF16)
            offs.append(_dot_nt(q_rel, k_rel))
        a_offs.append(offs)
        tot_col = jnp.sum(jnp.where(eye, jnp.broadcast_to(tot, (GLA_DK, GLA_DK)), 0.0),
                          axis=-1, keepdims=True)
        decays.append(jnp.exp(tot_col))
        cum_s[u] = cum
        k_s[u] = k
        qs.append(q)
        v_bfs.append(v_bf)
        cums.append(cum)

    h_offs = [[_dot(a_offs[u][i - 1].astype(BF16), v_bfs[u][:i * SUB]) for i in range(1, nsub)]
              for u in range(len(units))]

    t_half = lax.broadcasted_iota(jnp.int32, (half, 1), 0)
    lane = lax.broadcasted_iota(jnp.int32, (1, LANES), 1)
    for u, (bb, h) in enumerate(units):
        vs = slice(h * GLA_DV, (h + 1) * GLA_DV)
        for i in range(nsub):
            rs = slice(i * SUB, (i + 1) * SUB)
            q_top, q_bot = qs[u][i * SUB:i * SUB + half], qs[u][i * SUB + half:(i + 1) * SUB]
            c_top, c_bot = cums[u][i * SUB:i * SUB + half], cums[u][i * SUB + half:(i + 1) * SUB]
            a_top = jnp.zeros((half, LANES), F32)
            a_bot = jnp.zeros((half, LANES), F32)
            for j in range(SUB):
                rj = i * SUB + j
                kj = k_s[u, rj:rj + 1, :]
                cj = cum_s[u, rj:rj + 1, :]
                if j < half:
                    e_top = jnp.exp(jnp.where(t_half >= j, c_top - cj, NEG))
                    a_top = jnp.where(lane == j, jnp.sum(q_top * kj * e_top, axis=-1, keepdims=True), a_top)
                    e_bot = jnp.exp(c_bot - cj)
                else:
                    e_bot = jnp.exp(jnp.where(t_half >= j - half, c_bot - cj, NEG))
                a_bot = jnp.where(lane == j, jnp.sum(q_bot * kj * e_bot, axis=-1, keepdims=True), a_bot)
            a_diag = jnp.concatenate([a_top, a_bot], axis=0)[:, :SUB].astype(BF16)
            hb = inters[u][rs] + _dot(a_diag, v_bfs[u][rs])
            if i > 0:
                hb = hb + h_offs[u][i - 1]
            hn = hb * lax.rsqrt(jnp.mean(hb * hb, axis=-1, keepdims=True) + EPS)
            r = main_ref[bb, rs, 2 * nk + nv + h * GLA_DV:2 * nk + nv + (h + 1) * GLA_DV].astype(F32)
            out = hn * ng_ref[:, vs] * (r * _sigmoid(r))
            o_ref[bb, rs, vs] = out.astype(o_ref.dtype)

    for u in range(len(units)):
        st_s[u] = decays[u] * st_s[u] + updates[u]


def _gla_core(main, g_low, w_gate_up, b_gate, norm_g, *, nb=4):
    b, s, _ = main.shape
    nc = s // CHUNK
    nk = GLA_HEADS * GLA_DK
    nv = GLA_HEADS * GLA_DV
    wgu = jnp.zeros((LANES, nk), F32).at[:GLA_RANK].set(w_gate_up)
    return pl.pallas_call(
        functools.partial(_gla_kernel, nb=nb),
        grid=(b // nb, nc),
        in_specs=[
            pl.BlockSpec((nb, CHUNK, main.shape[2]), lambda i, c: (i, c, 0)),
            pl.BlockSpec((nb, CHUNK, LANES), lambda i, c: (i, c, 0)),
            pl.BlockSpec((LANES, nk), lambda i, c: (0, 0)),
            pl.BlockSpec((1, nk), lambda i, c: (0, 0)),
            pl.BlockSpec((1, nv), lambda i, c: (0, 0)),
        ],
        out_specs=pl.BlockSpec((nb, CHUNK, nv), lambda i, c: (i, c, 0)),
        out_shape=jax.ShapeDtypeStruct((b, s, nv), BF16),
        scratch_shapes=[
            pltpu.VMEM((nb * GLA_HEADS, GLA_DK, GLA_DV), F32),
            pltpu.VMEM((nb * GLA_HEADS, CHUNK, GLA_DK), F32),
            pltpu.VMEM((nb * GLA_HEADS, CHUNK, GLA_DK), F32),
        ],
        compiler_params=_cparams(("parallel", "arbitrary")),
        name="gla_core",
    )(main, g_low, wgu, b_gate.reshape(1, -1), norm_g.reshape(1, -1))


def _gated_in_proj(x, g, w_in, n_gate):
    n_main = w_in.shape[1] - n_gate
    w_bf = w_in.astype(BF16)
    w_gate = jnp.pad(w_bf[:, n_main:], ((0, 0), (0, LANES - n_gate)))
    return _in_proj(x, g, w_bf, n_main, w_gate)


def kernel(x, mix_norm_g, ffn_norm_g, ffn_w_gate, ffn_w_up, ffn_w_down, ml_w_in, ml_b_if, ml_norm_g, ml_w_out, sb_w_in, sb_w_out, fx_w_in, fx_b_f, fx_q_norm_g, fx_k_norm_g, fx_w_out, gla_w_in, gla_w_gate_up, gla_b_gate, gla_norm_g, gla_w_out):
    b, s, d = x.shape
    depth = mix_norm_g.shape[0]
    h = x.reshape(b * s, d)
    ffn_gate, ffn_up, ffn_down = (w.astype(BF16) for w in (ffn_w_gate, ffn_w_up, ffn_w_down))
    for layer in range(depth):
        kind = layer % 4
        j = layer // 4
        g_mix = mix_norm_g[layer].reshape(1, d)
        if kind == 0:
            main, gates = _gated_in_proj(h, g_mix, ml_w_in[j], 2 * ML_HEADS)
            a = _mlstm_core(main.reshape(b, s, -1), gates.reshape(b, s, LANES), ml_b_if[j], ml_norm_g[j])
            w_out = ml_w_out[j]
        elif kind == 1:
            nq = SB_HEADS * SB_DH
            w_sb = jnp.concatenate([sb_w_in[j][:, :nq] * (SB_DH ** -0.5 * LOG2E), sb_w_in[j][:, nq:]], axis=1)
            (main,) = _in_proj(h, g_mix, w_sb.astype(BF16), w_sb.shape[1])
            a = _sb_core(main.reshape(b, s, -1))
            w_out = sb_w_out[j]
        elif kind == 2:
            main, gates = _gated_in_proj(h, g_mix, fx_w_in[j], FX_HEADS)
            main = main.reshape(b, s, -1)
            bound = _fox_logit_bound(fx_q_norm_g[j], fx_k_norm_g[j])
            bounded = bound <= FX_MAX_SHIFT
            qa, ka = _fox_prep(main, gates.reshape(b, s, LANES), fx_b_f[j], fx_q_norm_g[j], fx_k_norm_g[j],
                               jnp.where(bounded, bound, 0.0))
            a = _fox_core(main, qa, ka, bounded)
            w_out = fx_w_out[j]
        else:
            main, gates = _gated_in_proj(h, g_mix, gla_w_in[j], GLA_RANK)
            a = _gla_core(main.reshape(b, s, -1), gates.reshape(b, s, LANES), gla_w_gate_up[j], gla_b_gate[j], gla_norm_g[j])
            w_out = gla_w_out[j]
        h = _out_ffn(h, a.reshape(b * s, -1), w_out.astype(BF16), ffn_norm_g[layer].reshape(1, d),
                     ffn_gate, ffn_up, ffn_down, layer)
    return h.reshape(b, s, d)
```

```python
import functools

import jax
import jax.numpy as jnp
import numpy as np
from jax import lax
from jax.experimental import pallas as pl
from jax.experimental.pallas import tpu as pltpu

F32 = jnp.float32
BF16 = jnp.bfloat16

EPS = 1e-6
CHUNK = 64
ML_HEADS, ML_DQK, ML_DV = 8, 64, 128
SB_HEADS, SB_DH = 16, 64
FX_HEADS, FX_DH = 16, 64
GLA_HEADS, GLA_DK, GLA_DV = 4, 128, 256
GLA_RANK = 16
GLA_TAU = 16.0
GLA_SUB = 16
NEG = -1e30
LOG2E = 1.4426950408889634
FX_MAX_SHIFT = 60.0
FX_TERMS = 3
FX_ONE_LANE = FX_TERMS * FX_HEADS
SB_LOG2_ZERO = -150.0
SB_Z_CLAMP = 100.0
LANES = 128
VMEM_LIMIT = 56 * 1024 * 1024


def _cparams(sem):
    return pltpu.CompilerParams(dimension_semantics=sem, vmem_limit_bytes=VMEM_LIMIT)


def _resident(shape):
    return pl.BlockSpec(shape, lambda i: (0,) * len(shape), pipeline_mode=pl.Buffered(1))


def _log_sigmoid(x):
    return jnp.minimum(x, 0.0) - jnp.log(1.0 + jnp.exp(-jnp.abs(x)))


def _sigmoid(x):
    return 1.0 / (1.0 + jnp.exp(-x))


def _split_bf16(x):
    hi = x.astype(BF16)
    lo = (x - hi.astype(F32)).astype(BF16)
    return hi, lo


def _dot(a, b):
    return jnp.dot(a, b, preferred_element_type=F32)


def _dot_nt(a, b):
    return lax.dot_general(a, b, (((1,), (1,)), ((), ())), preferred_element_type=F32)


def _dot_tn(a, b):
    return lax.dot_general(a, b, (((0,), (0,)), ((), ())), preferred_element_type=F32)


def _tri_dot(tri, x):
    hi, lo = _split_bf16(x)
    return _dot(tri, hi) + _dot(tri, lo)


def _dot_tri(x, tri):
    hi, lo = _split_bf16(x)
    return _dot(hi, tri) + _dot(lo, tri)


def _in_proj_kernel(x_ref, g_ref, w_ref, *rest, tr, tn):
    wg_ref, main_ref, gate_ref = rest if len(rest) == 3 else (None, rest[0], None)

    def normed(rows):
        x = x_ref[rows, :]
        ms = jnp.mean(x * x, axis=-1, keepdims=True)
        return (x * lax.rsqrt(ms + EPS) * g_ref[...]).astype(BF16)

    blocks = [slice(r * tr, (r + 1) * tr) for r in range(x_ref.shape[0] // tr)]
    u = normed(blocks[0])
    for r, rows in enumerate(blocks):
        u_next = None
        for c in range(main_ref.shape[-1] // tn):
            cs = slice(c * tn, (c + 1) * tn)
            main_ref[rows, cs] = _dot(u, w_ref[:, cs]).astype(main_ref.dtype)
            if c == 0 and r + 1 < len(blocks):
                u_next = normed(blocks[r + 1])
        if gate_ref is not None:
            gate_ref[rows, :] = _dot(u, wg_ref[...])
        u = u_next


def _in_proj(x, g, w_in, wn, w_gate=None, *, tm=1024, tr=256, tn=512):
    n, d = x.shape
    in_specs = [pl.BlockSpec((tm, d), lambda i: (i, 0)), _resident((1, d)), _resident(w_in.shape)]
    out_specs = [pl.BlockSpec((tm, wn), lambda i: (i, 0))]
    out_shape = [jax.ShapeDtypeStruct((n, wn), BF16)]
    operands = [x, g, w_in]
    if w_gate is not None:
        in_specs.append(_resident((d, LANES)))
        out_specs.append(pl.BlockSpec((tm, LANES), lambda i: (i, 0)))
        out_shape.append(jax.ShapeDtypeStruct((n, LANES), F32))
        operands.append(w_gate)
    return pl.pallas_call(
        functools.partial(_in_proj_kernel, tr=tr, tn=tn),
        grid=(n // tm,),
        in_specs=in_specs,
        out_specs=out_specs,
        out_shape=out_shape,
        compiler_params=_cparams(("parallel",)),
        name="in_proj",
    )(*operands)


def _out_ffn_kernel(h_ref, a_ref, wo_ref, g_ref, wg_ref, wu_ref, wd_ref, o_ref, *, tf):
    h1 = h_ref[...] + _dot(a_ref[...], wo_ref[...])
    ms = jnp.mean(h1 * h1, axis=-1, keepdims=True)
    u = (h1 * lax.rsqrt(ms + EPS) * g_ref[...]).astype(BF16)
    acc = h1
    for c in range(wg_ref.shape[1] // tf):
        cs = slice(c * tf, (c + 1) * tf)
        gt = _dot(u, wg_ref[:, cs])
        up = _dot(u, wu_ref[:, cs])
        hid = (gt * _sigmoid(gt) * up).astype(BF16)
        acc = acc + _dot(hid, wd_ref[cs, :])
    o_ref[...] = acc


def _out_ffn(h, a, w_out, g, w_gate, w_up, w_down, layer, *, tm=1024, tf=256):
    n, d = h.shape
    dff = w_gate.shape[2]

    def slab(rows, cols):
        return pl.BlockSpec((None, rows, cols), lambda i: (layer, 0, 0), pipeline_mode=pl.Buffered(1))

    return pl.pallas_call(
        functools.partial(_out_ffn_kernel, tf=tf),
        grid=(n // tm,),
        in_specs=[
            pl.BlockSpec((tm, d), lambda i: (i, 0)),
            pl.BlockSpec((tm, d), lambda i: (i, 0)),
            _resident((d, d)),
            _resident((1, d)),
            slab(d, dff),
            slab(d, dff),
            slab(dff, d),
        ],
        out_specs=pl.BlockSpec((tm, d), lambda i: (i, 0)),
        out_shape=jax.ShapeDtypeStruct((n, d), F32),
        compiler_params=_cparams(("parallel",)),
        name="out_ffn",
    )(h, a, w_out, g, w_gate, w_up, w_down)


def _mlstm_select_matrices():
    sel = np.zeros((2, LANES, ML_HEADS * LANES), np.float32)
    for h in range(ML_HEADS):
        sel[0, h, h * LANES:(h + 1) * LANES] = 1.0
        sel[1, ML_HEADS + h, h * LANES:(h + 1) * LANES] = 1.0
    return jnp.asarray(sel, BF16)


def _mlstm_kernel(main_ref, gc_ref, gr_ref, bc_ref, br_ref, ng_ref, sel_ref, o_ref, c_s, m_s, *, nb):
    @pl.when(pl.program_id(1) == 0)
    def _():
        c_s[...] = jnp.zeros_like(c_s)
        m_s[...] = jnp.zeros_like(m_s)

    staged = [_mlstm_gates(bb, main_ref, gc_ref, gr_ref, bc_ref, br_ref, sel_ref, c_s) for bb in range(nb)]
    for bb in range(nb):
        _mlstm_heads(bb, staged[bb], main_ref, ng_ref, o_ref, c_s, m_s)


def _mlstm_gates(bb, main_ref, gc_ref, gr_ref, bc_ref, br_ref, sel_ref, c_s):
    L = CHUNK
    H = ML_HEADS
    row = lax.broadcasted_iota(jnp.int32, (L, L), 0)
    col = lax.broadcasted_iota(jnp.int32, (L, L), 1)
    tri = (col <= row).astype(BF16)
    tri_t = (row <= col).astype(BF16)

    gc = gc_ref[bb] + bc_ref[...]
    gr = gr_ref[bb, 0] + br_ref[...]
    cum_c = _tri_dot(tri, _log_sigmoid(gc))
    li_r = gr[0:H]
    cum_r = _dot_tri(_log_sigmoid(gr[H:2 * H]), tri_t)

    li_b_all = _dot_tri(gc, sel_ref[0])
    cum_b_all = _dot_tri(cum_c, sel_ref[1])
    nq = H * ML_DQK
    qs = [main_ref[bb, :, h * ML_DQK:(h + 1) * ML_DQK] for h in range(H)]
    ks = [main_ref[bb, :, nq + h * ML_DQK:nq + (h + 1) * ML_DQK] for h in range(H)]
    s_qks = [_dot_nt(qs[h], ks[h]) for h in range(H)]
    inters = [_dot(qs[h], c_s[bb * H + h].astype(BF16)) for h in range(H)]
    return cum_r, li_r, cum_b_all, li_b_all, ks, s_qks, inters


def _mlstm_heads(bb, staged, main_ref, ng_ref, o_ref, c_s, m_s):
    L = CHUNK
    H = ML_HEADS
    scale = ML_DQK ** -0.5
    nq = H * ML_DQK
    nv = H * ML_DV
    cum_r, li_r, cum_b_all, li_b_all, ks, s_qks, inters = staged
    causal = lax.broadcasted_iota(jnp.int32, (L, L), 1) <= lax.broadcasted_iota(jnp.int32, (L, L), 0)
    ones_blk = jnp.ones((L, ML_DV), BF16)

    for h in range(H):
        v = main_ref[bb, :, 2 * nq + h * ML_DV:2 * nq + (h + 1) * ML_DV]
        op = main_ref[bb, :, 2 * nq + nv + h * ML_DV:2 * nq + nv + (h + 1) * ML_DV]
        vext = jnp.concatenate([v, ones_blk], axis=1)
        cum_b = cum_b_all[:, h * LANES:(h + 1) * LANES]
        li_b = li_b_all[:, h * LANES:(h + 1) * LANES]
        cum_r1 = cum_r[h:h + 1, :]
        li_r1 = li_r[h:h + 1, :]
        m_st = m_s[bb * H + h][:, 0:1]
        cext = c_s[bb * H + h]

        log_d = jnp.where(causal, cum_b[:, :L] - cum_r1 + li_r1, NEG)
        m_t = jnp.maximum(cum_b + m_st, jnp.max(log_d, axis=-1, keepdims=True))
        dmat = jnp.exp(log_d - m_t[:, :L])
        w_inter = jnp.exp(cum_b + m_st - m_t)
        sc = (s_qks[h] * (dmat * scale)).astype(BF16)
        inter = inters[h]
        intra = _dot(sc, vext)
        num = w_inter * inter[:, :ML_DV] + intra[:, :ML_DV]
        den = w_inter * inter[:, ML_DV:] + intra[:, ML_DV:]
        hh = num / jnp.maximum(jnp.abs(den), jnp.exp(-m_t))
        hn = hh * lax.rsqrt(jnp.mean(hh * hh, axis=-1, keepdims=True) + EPS)
        out = hn * ng_ref[:, h * ML_DV:(h + 1) * ML_DV] * _sigmoid(op.astype(F32))
        o_ref[bb, :, h * ML_DV:(h + 1) * ML_DV] = out.astype(o_ref.dtype)

        tot_f = cum_r1[:, L - 1:L]
        log_w_r = tot_f - cum_r1 + li_r1
        m_new = jnp.maximum(tot_f + m_st, jnp.max(log_w_r, axis=-1, keepdims=True))
        decay = jnp.exp(tot_f + m_st - m_new)
        w_b = jnp.exp(tot_f - cum_b + li_b - m_new) * scale
        wk = (ks[h].astype(F32) * w_b[:, :ML_DQK]).astype(BF16)
        c_s[bb * H + h] = decay * cext + _dot_tn(wk, vext)
        m_s[bb * H + h] = jnp.broadcast_to(m_new, (1, LANES))


def _mlstm_core(main, gates, b_if, norm_g, *, nb=2):
    b, s, _ = main.shape
    nc = s // CHUNK
    h = ML_HEADS
    gr = gates[:, :, :2 * h].reshape(b, nc, CHUNK, 2 * h).transpose(0, 1, 3, 2)
    bc = jnp.zeros((1, LANES), F32).at[0, :2 * h].set(b_if)
    br = b_if.reshape(2 * h, 1)
    return pl.pallas_call(
        functools.partial(_mlstm_kernel, nb=nb),
        grid=(b // nb, nc),
        in_specs=[
            pl.BlockSpec((nb, CHUNK, main.shape[2]), lambda i, c: (i, c, 0)),
            pl.BlockSpec((nb, CHUNK, LANES), lambda i, c: (i, c, 0)),
            pl.BlockSpec((nb, 1, 2 * h, CHUNK), lambda i, c: (i, c, 0, 0)),
            pl.BlockSpec((1, LANES), lambda i, c: (0, 0)),
            pl.BlockSpec((2 * h, 1), lambda i, c: (0, 0)),
            pl.BlockSpec((1, h * ML_DV), lambda i, c: (0, 0)),
            pl.BlockSpec((2, LANES, h * LANES), lambda i, c: (0, 0, 0)),
        ],
        out_specs=pl.BlockSpec((nb, CHUNK, h * ML_DV), lambda i, c: (i, c, 0)),
        out_shape=jax.ShapeDtypeStruct((b, s, h * ML_DV), BF16),
        scratch_shapes=[
            pltpu.VMEM((nb * h, ML_DQK, 2 * ML_DV), F32),
            pltpu.VMEM((nb * h, 1, LANES), F32),
        ],
        compiler_params=_cparams(("parallel", "arbitrary")),
        name="mlstm_core",
    )(main, gates, gr, bc, br, norm_g.reshape(1, -1), _mlstm_select_matrices())


def _sb_kernel(q_ref, k_ref, v_ref, o_ref, acc_s, carry_s, *, t, heads):
    qi = pl.program_id(2)
    lane = lax.broadcasted_iota(jnp.int32, (1, LANES), 1)
    lo = lane < SB_DH
    sels = (lo, jnp.logical_not(lo))
    row = lax.broadcasted_iota(jnp.int32, (t, t), 0)
    col = lax.broadcasted_iota(jnp.int32, (t, t), 1)
    valid = col < row
    lower = (row > col).astype(BF16)
    groups = [slice(g * LANES, (g + 1) * LANES) for g in range(heads // 2)]
    zero = jnp.zeros((), BF16)
    qms = [jnp.where(sels[h % 2], q_ref[0, :, groups[h // 2]], zero) for h in range(heads)]

    def tiles(kts, carries, from_diagonal):
        hs = range(heads)
        k0s = [pl.multiple_of(kt * t, t) for kt in kts]
        masks = [from_diagonal and i == 0 for i in range(len(kts))]
        zs = [[_dot_nt(qms[h], k_ref[0, pl.ds(k0, t), groups[h // 2]]) for h in hs] for k0 in k0s]
        nlps = []
        for zt, masked in zip(zs, masks):
            row_nlps = []
            for h in hs:
                nlp = jnp.maximum(jnp.log2(1.0 + jnp.exp2(jnp.minimum(zt[h], SB_Z_CLAMP))), zt[h])
                row_nlps.append(jnp.where(valid, nlp, 0.0) if masked else nlp)
            nlps.append(row_nlps)
        laters = [[_dot(nt[h].astype(BF16), lower) for h in hs] for nt in nlps]
        for zt, nt, lt, masked, k0 in zip(zs, nlps, laters, masks, k0s):
            weights = []
            for h in hs:
                a = jnp.exp2(zt[h] - nt[h] - lt[h] + carries[h])
                weights.append((jnp.where(valid, a, 0.0) if masked else a).astype(BF16))
            for g, lanes in enumerate(groups):
                vv = v_ref[0, pl.ds(k0, t), lanes]
                contrib = (_dot(weights[2 * g], jnp.where(sels[0], vv, zero))
                           + _dot(weights[2 * g + 1], jnp.where(sels[1], vv, zero)))
                if masked:
                    acc_s[:, lanes] = contrib
                else:
                    acc_s[:, lanes] += contrib
            carries = tuple(carries[h] - jnp.sum(nt[h], axis=-1, keepdims=True) for h in hs)
        return carries

    zero_carries = (jnp.zeros((t, 1), F32),) * heads

    @pl.when(qi > 0)
    def _():
        for h, c in enumerate(tiles([qi, qi - 1], zero_carries, True)):
            carry_s[h] = c

    @pl.when(qi == 0)
    def _():
        for h, c in enumerate(tiles([qi], zero_carries, True)):
            carry_s[h] = c

    carries = tuple(carry_s[h] for h in range(heads))

    def live(carries):
        return jnp.max(functools.reduce(jnp.maximum, carries))

    def cond(st):
        return jnp.logical_and(st[0] >= 0, st[1] > SB_LOG2_ZERO)

    def body(st):
        carries = tiles([st[0]], st[2], False)
        return st[0] - 1, live(carries), carries

    lax.while_loop(cond, body, (qi - 2, live(carries), carries))
    o_ref[0] = acc_s[...].astype(o_ref.dtype)


def _sb_core(main, *, t=256, heads=16):
    b, s, _ = main.shape
    ng = SB_HEADS // heads
    w = heads * SB_DH
    return pl.pallas_call(
        functools.partial(_sb_kernel, t=t, heads=heads),
        grid=(b, ng, s // t),
        in_specs=[
            pl.BlockSpec((1, t, w), lambda i, p, j: (i, j, p)),
            pl.BlockSpec((1, s, w), lambda i, p, j: (i, 0, ng + p)),
            pl.BlockSpec((1, s, w), lambda i, p, j: (i, 0, 2 * ng + p)),
        ],
        out_specs=pl.BlockSpec((1, t, w), lambda i, p, j: (i, j, p)),
        out_shape=jax.ShapeDtypeStruct((b, s, SB_HEADS * SB_DH), BF16),
        scratch_shapes=[pltpu.VMEM((t, w), F32), pltpu.VMEM((heads, t, 1), F32)],
        compiler_params=_cparams(("parallel", "parallel", "arbitrary")),
        name="sb_core",
    )(main, main, main)


def _fox_scatter_matrices():
    pq = np.zeros((FX_HEADS // 2, LANES, 2 * LANES), np.float32)
    pk = np.zeros((FX_HEADS // 2, LANES, 2 * LANES), np.float32)
    for p in range(FX_HEADS // 2):
        for hh, base in ((0, FX_DH), (1, LANES)):
            head = 2 * p + hh
            for i in range(FX_TERMS):
                pq[p, FX_HEADS * i + head, base + i] = 1.0
                pq[p, FX_ONE_LANE, base + FX_TERMS + i] = 1.0
                pk[p, FX_ONE_LANE, base + i] = 1.0
                pk[p, FX_HEADS * i + head, base + FX_TERMS + i] = -1.0
    return jnp.asarray(pq, BF16), jnp.asarray(pk, BF16)


def _fox_prep_kernel(main_ref, fp_ref, bf_ref, gq_ref, gk_ref, sh_ref, pq_ref, pk_ref, hs_ref, qa_ref, ka_ref,
                     carry_s, *, ts):
    @pl.when(pl.program_id(1) == 0)
    def _():
        carry_s[...] = jnp.zeros_like(carry_s)

    row = lax.broadcasted_iota(jnp.int32, (ts, ts), 0)
    col = lax.broadcasted_iota(jnp.int32, (ts, ts), 1)
    tri = (col <= row).astype(BF16)
    lf = _log_sigmoid(fp_ref[0] + bf_ref[...])
    cum = _tri_dot(tri, lf) + carry_s[...]
    carry_s[...] = cum[ts - 1:ts, :]

    lane = lax.broadcasted_iota(jnp.int32, (1, LANES), 1)
    lo = lane < FX_DH
    head_lane = lane < FX_HEADS

    def packed(c):
        c = jnp.where(head_lane, c, 0.0)
        hi = c.astype(BF16).astype(F32)
        mid = (c - hi).astype(BF16).astype(F32)
        low = (c - hi - mid).astype(BF16).astype(F32)
        out = hi + pltpu.roll(mid, FX_HEADS, axis=1) + pltpu.roll(low, 2 * FX_HEADS, axis=1)
        return jnp.where(lane == FX_ONE_LANE, 1.0, out).astype(BF16)

    pack_q = packed(cum * LOG2E - sh_ref[...])
    pack_k = packed(cum * LOG2E)

    n_q = FX_HEADS * FX_DH // LANES
    for c in range(2 * n_q):
        is_q = c < n_q
        p = c % n_q
        xc = main_ref[0, :, c * LANES:(c + 1) * LANES].astype(F32)
        x2 = xc * xc
        ssq = _dot(x2.astype(BF16), hs_ref[...])
        xn = xc * lax.rsqrt(ssq / FX_DH + EPS) * (gq_ref[...] if is_q else gk_ref[...])
        aug = _dot(pack_q, pq_ref[p]) if is_q else _dot(pack_k, pk_ref[p])
        out_ref = qa_ref if is_q else ka_ref
        out_ref[0, 2 * p] = jnp.where(lo, xn, aug[:, :LANES]).astype(out_ref.dtype)
        out_ref[0, 2 * p + 1] = jnp.where(lo, aug[:, LANES:], xn).astype(out_ref.dtype)


def _fox_prep(main, f_pre, b_f, q_norm_g, k_norm_g, shift, *, ts=256):
    b, s, _ = main.shape
    w = 2 * FX_HEADS * FX_DH
    bf = jnp.zeros((1, LANES), F32).at[0, :FX_HEADS].set(b_f)
    sh = jnp.full((1, LANES), shift, F32)
    gq = jnp.tile(q_norm_g, 2).reshape(1, LANES) * (FX_DH ** -0.5 * LOG2E)
    gk = jnp.tile(k_norm_g, 2).reshape(1, LANES)
    pq, pk = _fox_scatter_matrices()
    lane_head = np.arange(LANES) // FX_DH
    head_sum = jnp.asarray(lane_head[:, None] == lane_head[None, :], BF16)
    head_rows = pl.BlockSpec((1, FX_HEADS, ts, LANES), lambda i, j: (i, 0, j, 0))
    const_row = pl.BlockSpec((1, LANES), lambda i, j: (0, 0))
    scatter = pl.BlockSpec(pq.shape, lambda i, j: (0, 0, 0))
    return pl.pallas_call(
        functools.partial(_fox_prep_kernel, ts=ts),
        grid=(b, s // ts),
        in_specs=[
            pl.BlockSpec((1, ts, w), lambda i, j: (i, j, 0)),
            pl.BlockSpec((1, ts, LANES), lambda i, j: (i, j, 0)),
            const_row, const_row, const_row, const_row, scatter, scatter,
            pl.BlockSpec((LANES, LANES), lambda i, j: (0, 0)),
        ],
        out_specs=[head_rows, head_rows],
        out_shape=[jax.ShapeDtypeStruct((b, FX_HEADS, s, LANES), BF16)] * 2,
        scratch_shapes=[pltpu.VMEM((1, LANES), F32)],
        compiler_params=_cparams(("parallel", "arbitrary")),
        name="fox_prep",
    )(main, f_pre, bf, gq, gk, sh, pq, pk, head_sum)


def _fox_kernel(bounded_ref, q_ref, k_ref, v_ref, op_ref, o_ref, acc_s, *, t, heads):
    qi = pl.program_id(2)
    hs = range(heads)
    lane = lax.broadcasted_iota(jnp.int32, (1, LANES), 1)
    lo = lane < FX_DH
    sels = (lo, jnp.logical_not(lo))
    row = lax.broadcasted_iota(jnp.int32, (t, t), 0)
    col = lax.broadcasted_iota(jnp.int32, (t, t), 1)
    causal = col <= row
    qs = [q_ref[0, hh] for hh in hs]
    bounded = bounded_ref[0] != 0

    def scores(hh, k0, width, masked):
        s = _dot_nt(qs[hh], k_ref[0, hh, pl.ds(k0, width), :])
        return jnp.where(causal, s, NEG) if masked else s

    def vext(hh, k0, width):
        v2 = v_ref[0, pl.ds(k0, width), (hh // 2) * LANES:(hh // 2 + 1) * LANES]
        return jnp.where(sels[hh % 2], v2, jnp.ones((), BF16))

    def plain_tile(k0, width):
        ss = [scores(hh, k0, width, False) for hh in hs]
        ps = [jnp.exp2(s).astype(BF16) for s in ss]
        for hh in hs:
            acc_s[hh] += _dot(ps[hh], vext(hh, k0, width))

    def plain_diagonal(k0, k0_prev=None):
        half = t // 2
        parts = [(hh, r0, width) for r0, width in ((0, half), (half, t)) for hh in hs]
        ss = [_dot_nt(qs[hh][r0:r0 + half], k_ref[0, hh, pl.ds(k0, width), :]) for hh, r0, width in parts]
        ss_prev = [] if k0_prev is None else [scores(hh, k0_prev, t, False) for hh in hs]
        ps = []
        for s, (hh, r0, width) in zip(ss, parts):
            keep = (lax.broadcasted_iota(jnp.int32, (half, width), 1)
                    <= lax.broadcasted_iota(jnp.int32, (half, width), 0) + r0)
            ps.append(jnp.exp2(jnp.where(keep, s, NEG)).astype(BF16))
        ps_prev = [jnp.exp2(s).astype(BF16) for s in ss_prev]
        for p, (hh, r0, width) in zip(ps, parts):
            acc_s[hh, r0:r0 + half] = _dot(p, vext(hh, k0, width))
        for hh, p in enumerate(ps_prev):
            acc_s[hh] += _dot(p, vext(hh, k0_prev, t))

    def online_tile(kt, ms, masked):
        k0 = pl.multiple_of(kt * t, t)
        new_ms = []
        for hh in hs:
            s = scores(hh, k0, t, masked)
            m_new = jnp.maximum(ms[hh], jnp.max(s, axis=-1, keepdims=True))
            alpha = jnp.exp2(ms[hh] - m_new)
            p = jnp.exp2(s - m_new).astype(BF16)
            acc_s[hh] = alpha * acc_s[hh] + _dot(p, vext(hh, k0, t))
            new_ms.append(m_new)
        return tuple(new_ms)

    @pl.when(bounded)
    def _():
        @pl.when(qi > 0)
        def _():
            plain_diagonal(pl.multiple_of(qi * t, t), pl.multiple_of((qi - 1) * t, t))

        @pl.when(qi == 0)
        def _():
            plain_diagonal(pl.multiple_of(qi * t, t))

        def step(kt, carry):
            plain_tile(pl.multiple_of(kt * t, t), t)
            return carry

        lax.fori_loop(0, qi - 1, step, 0)

    @pl.when(jnp.logical_not(bounded))
    def _():
        acc_s[...] = jnp.zeros_like(acc_s)
        m0 = jnp.full((t, 1), NEG, F32)
        ms = lax.fori_loop(0, qi, lambda kt, ms: online_tile(kt, ms, False), (m0,) * heads)
        online_tile(qi, ms, True)

    for g in range(heads // 2):
        lanes = slice(g * LANES, (g + 1) * LANES)
        acc0 = acc_s[2 * g]
        acc1 = acc_s[2 * g + 1]
        num = jnp.where(lo, acc0, acc1)
        den = pltpu.roll(jnp.where(lo, acc1, acc0), FX_DH, axis=1)
        o_ref[0, :, lanes] = (num / den * _sigmoid(op_ref[0, :, lanes].astype(F32))).astype(o_ref.dtype)


def _fox_logit_bound(q_norm_g, k_norm_g):
    return (FX_DH ** 0.5 * LOG2E) * jnp.max(jnp.abs(q_norm_g)) * jnp.max(jnp.abs(k_norm_g))


def _fox_core(main, qa, ka, bounded, *, t=512, heads=8):
    b, s, _ = main.shape
    ng = FX_HEADS // heads
    w = heads * FX_DH
    return pl.pallas_call(
        functools.partial(_fox_kernel, t=t, heads=heads),
        grid=(b, ng, s // t),
        in_specs=[
            pl.BlockSpec(memory_space=pltpu.SMEM),
            pl.BlockSpec((1, heads, t, LANES), lambda i, p, j: (i, p, j, 0)),
            pl.BlockSpec((1, heads, s, LANES), lambda i, p, j: (i, p, 0, 0)),
            pl.BlockSpec((1, s, w), lambda i, p, j: (i, 0, 2 * ng + p)),
            pl.BlockSpec((1, t, w), lambda i, p, j: (i, j, 3 * ng + p)),
        ],
        out_specs=pl.BlockSpec((1, t, w), lambda i, p, j: (i, j, p)),
        out_shape=jax.ShapeDtypeStruct((b, s, FX_HEADS * FX_DH), BF16),
        scratch_shapes=[pltpu.VMEM((heads, t, LANES), F32)],
        compiler_params=_cparams(("parallel", "parallel", "arbitrary")),
        name="fox_core",
    )(bounded.astype(jnp.int32).reshape(1), qa, ka, main, main)


def _gla_kernel(main_ref, gl_ref, wgu_ref, bg_ref, ng_ref, o_ref, st_s, cum_s, k_s, *, nb):
    L = CHUNK
    H = GLA_HEADS
    SUB = GLA_SUB
    nk = H * GLA_DK
    nv = H * GLA_DV
    scale = GLA_DK ** -0.5

    @pl.when(pl.program_id(1) == 0)
    def _():
        st_s[...] = jnp.zeros_like(st_s)

    row = lax.broadcasted_iota(jnp.int32, (L, L), 0)
    col = lax.broadcasted_iota(jnp.int32, (L, L), 1)
    tri = (col <= row).astype(BF16)
    eye = (lax.broadcasted_iota(jnp.int32, (GLA_DK, GLA_DK), 0)
           == lax.broadcasted_iota(jnp.int32, (GLA_DK, GLA_DK), 1))

    w_hi, w_lo = _split_bf16(wgu_ref[...])
    cum_alls = []
    for bb in range(nb):
        g_hi, g_lo = _split_bf16(gl_ref[bb])
        logits = _dot(g_hi, w_hi) + _dot(g_hi, w_lo) + _dot(g_lo, w_hi)
        la = _log_sigmoid(logits + bg_ref[...]) * (1.0 / GLA_TAU)
        cum_alls.append(_tri_dot(tri, la))

    units = [(bb, h) for bb in range(nb) for h in range(H)]
    nsub = L // SUB
    half = SUB // 2
    qs, v_bfs, cums, decays = [], [], [], []

    inters, updates, a_offs = [], [], []
    for u, (bb, h) in enumerate(units):
        ks = slice(h * GLA_DK, (h + 1) * GLA_DK)
        cum = cum_alls[bb][:, ks]
        tot = cum[L - 1:L, :]
        q = main_ref[bb, :, ks].astype(F32) * scale
        k = main_ref[bb, :, nk + h * GLA_DK:nk + (h + 1) * GLA_DK].astype(F32)
        v_bf = main_ref[bb, :, 2 * nk + h * GLA_DV:2 * nk + (h + 1) * GLA_DV]
        inters.append(_dot((q * jnp.exp(cum)).astype(BF16), st_s[u].astype(BF16)))
        updates.append(_dot_tn((k * jnp.exp(tot - cum)).astype(BF16), v_bf))
        offs = []
        for i in range(1, nsub):
            rs = slice(i * SUB, (i + 1) * SUB)
            ref_pt = cum[i * SUB - 1:i * SUB, :]
            q_rel = (q[rs] * jnp.exp(cum[rs] - ref_pt)).astype(BF16)
            k_rel = (k[:i * SUB] * jnp.exp(ref_pt - cum[:i * SUB])).astype(BF16)
            offs.append(_dot_nt(q_rel, k_rel))
        a_offs.append(offs)
        tot_col = jnp.sum(jnp.where(eye, jnp.broadcast_to(tot, (GLA_DK, GLA_DK)), 0.0),
                          axis=-1, keepdims=True)
        decays.append(jnp.exp(tot_col))
        cum_s[u] = cum
        k_s[u] = k
        qs.append(q)
        v_bfs.append(v_bf)
        cums.append(cum)

    h_offs = [[_dot(a_offs[u][i - 1].astype(BF16), v_bfs[u][:i * SUB]) for i in range(1, nsub)]
              for u in range(len(units))]

    t_half = lax.broadcasted_iota(jnp.int32, (half, 1), 0)
    lane = lax.broadcasted_iota(jnp.int32, (1, LANES), 1)
    for u, (bb, h) in enumerate(units):
        vs = slice(h * GLA_DV, (h + 1) * GLA_DV)
        for i in range(nsub):
            rs = slice(i * SUB, (i + 1) * SUB)
            q_top, q_bot = qs[u][i * SUB:i * SUB + half], qs[u][i * SUB + half:(i + 1) * SUB]
            c_top, c_bot = cums[u][i * SUB:i * SUB + half], cums[u][i * SUB + half:(i + 1) * SUB]
            a_top = jnp.zeros((half, LANES), F32)
            a_bot = jnp.zeros((half, LANES), F32)
            for j in range(SUB):
                rj = i * SUB + j
                kj = k_s[u, rj:rj + 1, :]
                cj = cum_s[u, rj:rj + 1, :]
                if j < half:
                    e_top = jnp.exp(jnp.where(t_half >= j, c_top - cj, NEG))
                    a_top = jnp.where(lane == j, jnp.sum(q_top * kj * e_top, axis=-1, keepdims=True), a_top)
                    e_bot = jnp.exp(c_bot - cj)
                else:
                    e_bot = jnp.exp(jnp.where(t_half >= j - half, c_bot - cj, NEG))
                a_bot = jnp.where(lane == j, jnp.sum(q_bot * kj * e_bot, axis=-1, keepdims=True), a_bot)
            a_diag = jnp.concatenate([a_top, a_bot], axis=0)[:, :SUB].astype(BF16)
            hb = inters[u][rs] + _dot(a_diag, v_bfs[u][rs])
            if i > 0:
                hb = hb + h_offs[u][i - 1]
            hn = hb * lax.rsqrt(jnp.mean(hb * hb, axis=-1, keepdims=True) + EPS)
            r = main_ref[bb, rs, 2 * nk + nv + h * GLA_DV:2 * nk + nv + (h + 1) * GLA_DV].astype(F32)
            out = hn * ng_ref[:, vs] * (r * _sigmoid(r))
            o_ref[bb, rs, vs] = out.astype(o_ref.dtype)

    for u in range(len(units)):
        st_s[u] = decays[u] * st_s[u] + updates[u]


def _gla_core(main, g_low, w_gate_up, b_gate, norm_g, *, nb=4):
    b, s, _ = main.shape
    nc = s // CHUNK
    nk = GLA_HEADS * GLA_DK
    nv = GLA_HEADS * GLA_DV
    wgu = jnp.zeros((LANES, nk), F32).at[:GLA_RANK].set(w_gate_up)
    return pl.pallas_call(
        functools.partial(_gla_kernel, nb=nb),
        grid=(b // nb, nc),
        in_specs=[
            pl.BlockSpec((nb, CHUNK, main.shape[2]), lambda i, c: (i, c, 0)),
            pl.BlockSpec((nb, CHUNK, LANES), lambda i, c: (i, c, 0)),
            pl.BlockSpec((LANES, nk), lambda i, c: (0, 0)),
            pl.BlockSpec((1, nk), lambda i, c: (0, 0)),
            pl.BlockSpec((1, nv), lambda i, c: (0, 0)),
        ],
        out_specs=pl.BlockSpec((nb, CHUNK, nv), lambda i, c: (i, c, 0)),
        out_shape=jax.ShapeDtypeStruct((b, s, nv), BF16),
        scratch_shapes=[
            pltpu.VMEM((nb * GLA_HEADS, GLA_DK, GLA_DV), F32),
            pltpu.VMEM((nb * GLA_HEADS, CHUNK, GLA_DK), F32),
            pltpu.VMEM((nb * GLA_HEADS, CHUNK, GLA_DK), F32),
        ],
        compiler_params=_cparams(("parallel", "arbitrary")),
        name="gla_core",
    )(main, g_low, wgu, b_gate.reshape(1, -1), norm_g.reshape(1, -1))


def _gated_in_proj(x, g, w_in, n_gate):
    n_main = w_in.shape[1] - n_gate
    w_bf = w_in.astype(BF16)
    w_gate = jnp.pad(w_bf[:, n_main:], ((0, 0), (0, LANES - n_gate)))
    return _in_proj(x, g, w_bf, n_main, w_gate)


def kernel(x, mix_norm_g, ffn_norm_g, ffn_w_gate, ffn_w_up, ffn_w_down, ml_w_in, ml_b_if, ml_norm_g, ml_w_out, sb_w_in, sb_w_out, fx_w_in, fx_b_f, fx_q_norm_g, fx_k_norm_g, fx_w_out, gla_w_in, gla_w_gate_up, gla_b_gate, gla_norm_g, gla_w_out):
    b, s, d = x.shape
    depth = mix_norm_g.shape[0]
    h = x.reshape(b * s, d)
    ffn_gate, ffn_up, ffn_down = (w.astype(BF16) for w in (ffn_w_gate, ffn_w_up, ffn_w_down))
    for layer in range(depth):
        kind = layer % 4
        j = layer // 4
        g_mix = mix_norm_g[layer].reshape(1, d)
        if kind == 0:
            main, gates = _gated_in_proj(h, g_mix, ml_w_in[j], 2 * ML_HEADS)
            a = _mlstm_core(main.reshape(b, s, -1), gates.reshape(b, s, LANES), ml_b_if[j], ml_norm_g[j])
            w_out = ml_w_out[j]
        elif kind == 1:
            nq = SB_HEADS * SB_DH
            w_sb = jnp.concatenate([sb_w_in[j][:, :nq] * (SB_DH ** -0.5 * LOG2E), sb_w_in[j][:, nq:]], axis=1)
            (main,) = _in_proj(h, g_mix, w_sb.astype(BF16), w_sb.shape[1])
            a = _sb_core(main.reshape(b, s, -1))
            w_out = sb_w_out[j]
        elif kind == 2:
            main, gates = _gated_in_proj(h, g_mix, fx_w_in[j], FX_HEADS)
            main = main.reshape(b, s, -1)
            bound = _fox_logit_bound(fx_q_norm_g[j], fx_k_norm_g[j])
            bounded = bound <= FX_MAX_SHIFT
            qa, ka = _fox_prep(main, gates.reshape(b, s, LANES), fx_b_f[j], fx_q_norm_g[j], fx_k_norm_g[j],
                               jnp.where(bounded, bound, 0.0))
            a = _fox_core(main, qa, ka, bounded)
            w_out = fx_w_out[j]
        else:
            main, gates = _gated_in_proj(h, g_mix, gla_w_in[j], GLA_RANK)
            a = _gla_core(main.reshape(b, s, -1), gates.reshape(b, s, LANES), gla_w_gate_up[j], gla_b_gate[j], gla_norm_g[j])
            w_out = gla_w_out[j]
        h = _out_ffn(h, a.reshape(b * s, -1), w_out.astype(BF16), ffn_norm_g[layer].reshape(1, d),
                     ffn_gate, ffn_up, ffn_down, layer)
    return h.reshape(b, s, d)
```
